```python
import math
import jax, jax.numpy as jnp
from jax import lax
import numpy as np

D_MODEL = 1024
BATCH = 16
SEQ = 256
DEPTH = 2
DEC_BATCH = 4
DEC_SEQ = 1024
PAST_LEN = 512

GRID_W = 64
ROPE_THETA = 10000.0
ALPHA_DN = (2 * DEPTH) ** 0.25
BETA_DN = (8 * DEPTH) ** -0.25
LN_EPS = 1e-5
RMS_EPS = 1e-6
NEG_INF = -1e30
POOL_WINDOWS = (2, 4, 8, 16)
POOL_GROUP = D_MODEL // 8
POOL_WIDTH = 4 * POOL_GROUP
MLA_HEADS = 8
MLA_NOPE = 64
MLA_ROPE = 32
MLA_V = 64
MLA_QK = MLA_NOPE + MLA_ROPE
MLA_Q_LORA = 3 * D_MODEL // 8
MLA_KV_LORA = D_MODEL // 4
MLA_SCALE = 1.0 / math.sqrt(MLA_QK)
GQA_HEADS = 8
GQA_KV = 2
GQA_HD = 64
GQA_GROUP = GQA_HEADS // GQA_KV
GQA_SCALE = 1.0 / math.sqrt(GQA_HD)
WINDOW = 128
WBLK = 128
QBLK = 128
N_EXPERTS = 32
TOP_K = 4
D_FF = D_MODEL
SWIGLU_LIMIT = 7.0
SWIGLU_ALPHA = 1.702
IN_SIZES = (POOL_WIDTH, MLA_Q_LORA, MLA_KV_LORA, MLA_ROPE,
            GQA_HEADS * GQA_HD, GQA_KV * GQA_HD, GQA_KV * GQA_HD, 3 * D_MODEL)
IN_COLS = sum(IN_SIZES)
IN_SPLITS = tuple(int(s) for s in np.cumsum(IN_SIZES)[:-1])

kernel_name = "hybrid_pool_mla_swa_moe_diffusion_step"


def layer_norm(x, g, b):
    xf = x.astype(jnp.float32)
    mu = jnp.mean(xf, -1, keepdims=True)
    var = jnp.mean(jnp.square(xf - mu), -1, keepdims=True)
    return ((xf - mu) * lax.rsqrt(var + LN_EPS) * g.astype(jnp.float32) + b.astype(jnp.float32)).astype(x.dtype)


def rms_norm(x, g):
    xf = x.astype(jnp.float32)
    return (xf * lax.rsqrt(jnp.mean(jnp.square(xf), -1, keepdims=True) + RMS_EPS) * g.astype(jnp.float32)).astype(x.dtype)


def grid_positions(n_tokens):
    rows = n_tokens // GRID_W
    r = jnp.repeat(jnp.arange(rows), GRID_W).astype(jnp.float32)
    col = jnp.tile(jnp.arange(GRID_W), rows).astype(jnp.float32)
    return r, col


def rope_1d(x, pos):
    half = x.shape[-1] // 2
    freqs = jnp.power(ROPE_THETA, -jnp.arange(half, dtype=jnp.float32) / half)
    ang = pos[:, None] * freqs[None, :]
    cos = jnp.cos(ang)[:, None, :]
    sin = jnp.sin(ang)[:, None, :]
    xf = x.astype(jnp.float32)
    x1, x2 = xf[..., :half], xf[..., half:]
    return jnp.concatenate([x1 * cos - x2 * sin, x1 * sin + x2 * cos], -1).astype(x.dtype)


def rope_2d(x, rows, cols):
    d = x.shape[-1] // 2
    return jnp.concatenate([rope_1d(x[..., :d], rows), rope_1d(x[..., d:], cols)], -1)


def adaln_params(cvec, w, b):
    m = jax.nn.silu(cvec) @ w + b
    return jnp.split(m[..., None, :], 6, axis=-1)


def dense_attention(q, k, v, scale, sink=None):
    B, T, H, D = q.shape
    nb = T // QBLK
    qb = q.reshape(B, nb, QBLK, H, D).swapaxes(0, 1)

    def block(qblk):
        s = jnp.einsum('bqhd,bkhd->bhqk', qblk, k).astype(jnp.float32) * scale
        if sink is not None:
            sk = jnp.broadcast_to(sink.astype(jnp.float32)[None, :, None, None], s.shape[:-1] + (1,))
            p = jax.nn.softmax(jnp.concatenate([s, sk], -1), -1)[..., :-1]
        else:
            p = jax.nn.softmax(s, -1)
        return jnp.einsum('bhqk,bkhd->bqhd', p.astype(v.dtype), v)

    o = lax.map(block, qb)
    return o.swapaxes(0, 1).reshape(B, T, H, v.shape[-1])


def window_attention(q, k, v, kc, vc, sink, scale):
    B, T, H, D = q.shape
    nb = T // WBLK
    qb = q.reshape(B, nb, WBLK, GQA_KV, GQA_GROUP, D)

    def band(a):
        ap = jnp.pad(a, ((0, 0), (WBLK, WBLK), (0, 0), (0, 0))).reshape(B, nb + 2, WBLK, GQA_KV, a.shape[-1])
        return jnp.concatenate([ap[:, :-2], ap[:, 1:-1], ap[:, 2:]], axis=2)

    kb, vb = band(k), band(v)
    qpos = jnp.arange(nb)[:, None] * WBLK + jnp.arange(WBLK)[None, :]
    kpos = jnp.arange(nb)[:, None] * WBLK + jnp.arange(3 * WBLK)[None, :] - WBLK
    rel = kpos[:, None, :] - qpos[:, :, None]
    ok = (jnp.abs(rel) <= WINDOW) & (kpos[:, None, :] >= 0) & (kpos[:, None, :] < T)
    s_band = jnp.einsum('bnqkgd,bnskd->bnkgqs', qb, kb).astype(jnp.float32) * scale
    s_band = jnp.where(ok[None, :, None, None], s_band, NEG_INF)
    s_ctx = jnp.einsum('bnqkgd,bpkd->bnkgqp', qb, kc).astype(jnp.float32) * scale
    sk = jnp.broadcast_to(sink.astype(jnp.float32).reshape(GQA_KV, GQA_GROUP)[None, None, :, :, None, None],
                          s_band.shape[:-1] + (1,))
    p = jax.nn.softmax(jnp.concatenate([s_band, s_ctx, sk], -1), -1).astype(v.dtype)
    nband = 3 * WBLK
    n_ctx = kc.shape[1]
    o = (jnp.einsum('bnkgqs,bnskd->bnqkgd', p[..., :nband], vb)
         + jnp.einsum('bnkgqp,bpkd->bnqkgd', p[..., nband:nband + n_ctx], vc))
    return o.reshape(B, T, H, D)


def pool_mix(h, pw, ps):
    B, T, _ = h.shape
    hf = h.astype(jnp.float32).reshape(B, T, 4, POOL_GROUP)
    cs = jnp.concatenate([jnp.zeros((B, 1, 4, POOL_GROUP), jnp.float32), jnp.cumsum(hf, axis=1)], axis=1)
    t = jnp.arange(T)
    outs = []
    for g, w in enumerate(POOL_WINDOWS):
        lo = jnp.clip(t - w // 2, 0, T - 1)
        hi = jnp.clip(t + w // 2 - 1, 0, T - 1)
        s = cs[:, hi + 1, g] - cs[:, lo, g]
        cnt = (hi - lo + 1).astype(jnp.float32)[None, :, None]
        outs.append(s / cnt - hf[:, :, g])
    d = jnp.stack(outs, axis=2)
    y = jnp.einsum('btgc,gcd->btgd', d, pw.astype(jnp.float32)).reshape(B, T, POOL_WIDTH) * ps.astype(jnp.float32)
    return y.astype(h.dtype)


def mla_expand_kv(ckv, k_rope, wukv):
    B, S, _ = ckv.shape
    kv = (ckv @ wukv).reshape(B, S, MLA_HEADS, MLA_NOPE + MLA_V)
    k_nope, v = kv[..., :MLA_NOPE], kv[..., MLA_NOPE:]
    k = jnp.concatenate([k_nope, jnp.broadcast_to(k_rope[:, :, None, :], (B, S, MLA_HEADS, MLA_ROPE))], -1)
    return k, v


def merge_branches(hg, y_pool, y_mla, y_gqa, wbp, wbm, wbg, wo):
    B, T, _ = hg.shape
    g = jax.nn.sigmoid(hg.reshape(B, T, 3, D_MODEL))
    m = g[:, :, 0] * (y_pool @ wbp) + g[:, :, 1] * (y_mla @ wbm) + g[:, :, 2] * (y_gqa @ wbg)
    return m @ wo


def context_mixer(u, mw):
    (w_in_l, qn, kvn, wuq, wukv, sink, pw, ps, wbp, wbm, wbg, wo) = mw
    B, S, _ = u.shape
    hp, hcq, hckv, hkr, hq, hk, hv, hg = jnp.split(u @ w_in_l, IN_SPLITS, axis=-1)
    y_pool = pool_mix(hp, pw, ps)
    q = (rms_norm(hcq, qn) @ wuq).reshape(B, S, MLA_HEADS, MLA_QK)
    ckv = rms_norm(hckv, kvn)
    k, v = mla_expand_kv(ckv, hkr, wukv)
    y_mla = dense_attention(q, k, v, MLA_SCALE).reshape(B, S, MLA_HEADS * MLA_V)
    gq = hq.reshape(B, S, GQA_HEADS, GQA_HD)
    gk = hk.reshape(B, S, GQA_KV, GQA_HD)
    gv = hv.reshape(B, S, GQA_KV, GQA_HD)
    y_gqa = dense_attention(gq, jnp.repeat(gk, GQA_GROUP, axis=2), jnp.repeat(gv, GQA_GROUP, axis=2),
                            GQA_SCALE, sink).reshape(B, S, GQA_HEADS * GQA_HD)
    out = merge_branches(hg, y_pool, y_mla, y_gqa, wbp, wbm, wbg, wo)
    return out, ckv, hkr, gk, gv


def latent_mixer(u, rows, cols, ckv_c, kr_c, gk_c, gv_c, mw):
    (w_in_l, qn, kvn, wuq, wukv, sink, pw, ps, wbp, wbm, wbg, wo) = mw
    B, T, _ = u.shape
    hp, hcq, hckv, hkr, hq, hk, hv, hg = jnp.split(u @ w_in_l, IN_SPLITS, axis=-1)
    y_pool = pool_mix(hp, pw, ps)
    q = (rms_norm(hcq, qn) @ wuq).reshape(B, T, MLA_HEADS, MLA_QK)
    q = jnp.concatenate([q[..., :MLA_NOPE], rope_2d(q[..., MLA_NOPE:], rows, cols)], -1)
    ckv = rms_norm(hckv, kvn)
    kr = rope_2d(hkr[:, :, None, :], rows, cols)[:, :, 0, :]
    k_lat, v_lat = mla_expand_kv(ckv, kr, wukv)
    k_ctx, v_ctx = mla_expand_kv(ckv_c, kr_c, wukv)
    y_mla = dense_attention(q, jnp.concatenate([k_ctx, k_lat], 1), jnp.concatenate([v_ctx, v_lat], 1),
                            MLA_SCALE).reshape(B, T, MLA_HEADS * MLA_V)
    gq = rope_2d(hq.reshape(B, T, GQA_HEADS, GQA_HD), rows, cols)
    gk = rope_2d(hk.reshape(B, T, GQA_KV, GQA_HD), rows, cols)
    gv = hv.reshape(B, T, GQA_KV, GQA_HD)
    y_gqa = window_attention(gq, gk, gv, gk_c, gv_c, sink, GQA_SCALE).reshape(B, T, GQA_HEADS * GQA_HD)
    return merge_branches(hg, y_pool, y_mla, y_gqa, wbp, wbm, wbg, wo)


def moe(u, rw, rb, wgu, bgu, wd, bd):
    B, T, D = u.shape
    xf = u.reshape(-1, D)
    logits = (xf @ rw + rb).astype(jnp.float32)
    top_v, top_i = lax.top_k(logits, TOP_K)
    top_w = jax.nn.softmax(top_v, -1)
    gates = jnp.einsum('nk,nke->ne', top_w, jax.nn.one_hot(top_i, N_EXPERTS, dtype=jnp.float32)).astype(u.dtype)

    def expert(acc, params):
        wgu_e, bgu_e, wd_e, bd_e, g_e = params
        h = xf @ wgu_e + bgu_e
        glu = jnp.minimum(h[:, :D_FF], SWIGLU_LIMIT)
        lin = jnp.clip(h[:, D_FF:], -SWIGLU_LIMIT, SWIGLU_LIMIT)
        a = glu * jax.nn.sigmoid(SWIGLU_ALPHA * glu) * (lin + 1.0)
        return acc + g_e[:, None] * (a @ wd_e + bd_e), None

    acc, _ = lax.scan(expert, jnp.zeros_like(xf), (wgu, bgu, wd, bd, gates.T))
    return acc.reshape(B, T, D)


def setup_inputs(seed: int = 0) -> dict:
    key = jax.random.key(seed)
    ks = iter(jax.random.split(key, 40))

    def nrm(shape, scale):
        return jax.random.normal(next(ks), shape, jnp.float32) * scale

    L, D = DEPTH, D_MODEL
    return {
        "x_prompt": nrm((BATCH, SEQ, D), 1.0),
        "x_sample": nrm((DEC_BATCH, DEC_SEQ, D), 1.0),
        "c": nrm((DEC_BATCH, D), 1.0),
        "cache_mla_ckv": nrm((DEC_BATCH, L, PAST_LEN, MLA_KV_LORA), 1.0),
        "cache_mla_krope": nrm((DEC_BATCH, L, PAST_LEN, MLA_ROPE), 1.0),
        "cache_gqa_k": nrm((DEC_BATCH, L, PAST_LEN, GQA_KV, GQA_HD), 1.0),
        "cache_gqa_v": nrm((DEC_BATCH, L, PAST_LEN, GQA_KV, GQA_HD), 1.0),
        "c_ctx": nrm((D,), 1.0),
        "w_ada": nrm((L, D, 6 * D), 0.5 * D ** -0.5),
        "b_ada": nrm((L, 6 * D), 0.01),
        "w_in": nrm((L, D, IN_COLS), D ** -0.5),
        "mla_q_norm": 1.0 + nrm((L, MLA_Q_LORA), 0.01),
        "mla_kv_norm": 1.0 + nrm((L, MLA_KV_LORA), 0.01),
        "w_mla_uq": nrm((L, MLA_Q_LORA, MLA_HEADS * MLA_QK), MLA_Q_LORA ** -0.5),
        "w_mla_ukv": nrm((L, MLA_KV_LORA, MLA_HEADS * (MLA_NOPE + MLA_V)), MLA_KV_LORA ** -0.5),
        "gqa_sink": nrm((L, GQA_HEADS), 0.5),
        "pool_w": nrm((L, 4, POOL_GROUP, POOL_GROUP), POOL_GROUP ** -0.5),
        "pool_scale": 1.0 + nrm((L, POOL_WIDTH), 0.1),
        "w_branch_pool": nrm((L, POOL_WIDTH, D), POOL_WIDTH ** -0.5),
        "w_branch_mla": nrm((L, MLA_HEADS * MLA_V, D), (MLA_HEADS * MLA_V) ** -0.5),
        "w_branch_gqa": nrm((L, GQA_HEADS * GQA_HD, D), (GQA_HEADS * GQA_HD) ** -0.5),
        "w_out": nrm((L, D, D), BETA_DN * D ** -0.5),
        "ln1_g": 1.0 + nrm((L, D), 0.01),
        "ln1_b": nrm((L, D), 0.01),
        "ln2_g": 1.0 + nrm((L, D), 0.01),
        "ln2_b": nrm((L, D), 0.01),
        "router_w": nrm((L, D, N_EXPERTS), D ** -0.5),
        "router_b": nrm((L, N_EXPERTS), 0.01),
        "w_gate_up": nrm((L, N_EXPERTS, D, 2 * D_FF), D ** -0.5),
        "b_gate_up": nrm((L, N_EXPERTS, 2 * D_FF), 0.01),
        "w_down": nrm((L, N_EXPERTS, D_FF, D), BETA_DN * D_FF ** -0.5),
        "b_down": nrm((L, N_EXPERTS, D), 0.01),
    }


def reference(x_prompt, x_sample, c, cache_mla_ckv, cache_mla_krope, cache_gqa_k, cache_gqa_v, c_ctx,
              w_ada, b_ada, w_in, mla_q_norm, mla_kv_norm, w_mla_uq, w_mla_ukv, gqa_sink, pool_w, pool_scale,
              w_branch_pool, w_branch_mla, w_branch_gqa, w_out, ln1_g, ln1_b, ln2_g, ln2_b,
              router_w, router_b, w_gate_up, b_gate_up, w_down, b_down):
    rows, cols = grid_positions(x_sample.shape[1])
    xp, xs = x_prompt, x_sample
    ckv_l, kr_l, gk_l, gv_l = [], [], [], []
    for l in range(DEPTH):
        mw = (w_in[l], mla_q_norm[l], mla_kv_norm[l], w_mla_uq[l], w_mla_ukv[l], gqa_sink[l],
              pool_w[l], pool_scale[l], w_branch_pool[l], w_branch_mla[l], w_branch_gqa[l], w_out[l])
        ew = (router_w[l], router_b[l], w_gate_up[l], b_gate_up[l], w_down[l], b_down[l])

        sh1, sc1, g1, sh2, sc2, g2 = adaln_params(c_ctx, w_ada[l], b_ada[l])
        y, ckv, kr, gk, gv = context_mixer(xp * (1.0 + sc1) + sh1, mw)
        ckv_l.append(ckv); kr_l.append(kr); gk_l.append(gk); gv_l.append(gv)
        xp = layer_norm(ALPHA_DN * xp + g1 * y, ln1_g[l], ln1_b[l])
        xp = layer_norm(ALPHA_DN * xp + g2 * moe(xp * (1.0 + sc2) + sh2, *ew), ln2_g[l], ln2_b[l])

        sh1, sc1, g1, sh2, sc2, g2 = adaln_params(c, w_ada[l], b_ada[l])
        y = latent_mixer(xs * (1.0 + sc1) + sh1, rows, cols, cache_mla_ckv[:, l], cache_mla_krope[:, l],
                         cache_gqa_k[:, l], cache_gqa_v[:, l], mw)
        xs = layer_norm(ALPHA_DN * xs + g1 * y, ln1_g[l], ln1_b[l])
        xs = layer_norm(ALPHA_DN * xs + g2 * moe(xs * (1.0 + sc2) + sh2, *ew), ln2_g[l], ln2_b[l])

    new_mla_ckv = jnp.stack(ckv_l, axis=1)
    new_mla_krope = jnp.stack(kr_l, axis=1)
    new_gqa_k = jnp.stack(gk_l, axis=1)
    new_gqa_v = jnp.stack(gv_l, axis=1)
    return (xp, xs, new_mla_ckv, new_mla_krope, new_gqa_k, new_gqa_v)
```

```python
import functools
import math

import jax
import jax.numpy as jnp
from jax import lax
from jax.experimental import pallas as pl
from jax.experimental.pallas import tpu as pltpu

F32 = jnp.float32
BF16 = jnp.bfloat16
I32 = jnp.int32

D_MODEL = 1024
DEPTH = 2
B_CTX, S_CTX = 16, 256
B_LAT, S_LAT = 4, 1024
PAST_LEN = 512
GRID_W = 64
ROPE_THETA = 10000.0
ALPHA_DN = (2 * DEPTH) ** 0.25
LN_EPS = 1e-5
RMS_EPS = 1e-6
NEG_INF = -1e30
POOL_WINDOWS = (2, 4, 8, 16)
POOL_GROUP = 128
POOL_WIDTH = 512
MLA_HEADS = 8
MLA_NOPE = 64
MLA_ROPE = 32
MLA_V = 64
MLA_QK = MLA_NOPE + MLA_ROPE
MLA_Q_LORA = 384
MLA_KV_LORA = 256
MLA_SCALE = 1.0 / math.sqrt(MLA_QK)
GQA_HEADS = 8
GQA_KV = 2
GQA_HD = 64
GQA_GROUP = GQA_HEADS // GQA_KV
GQA_SCALE = 1.0 / math.sqrt(GQA_HD)
WBLK = 128
N_EXPERTS = 32
TOP_K = 4
D_FF = 1024
SWIGLU_LIMIT = 7.0
SWIGLU_ALPHA = 1.702

LANES = 128
SUBLANES = 8
VMEM_LIMIT = 56 * 1024 * 1024

T_CTX = B_CTX * S_CTX
T_LAT = B_LAT * S_LAT
T_ALL = T_CTX + T_LAT
TM = 256
N_TILES = T_ALL // TM
CTX_TILES = T_CTX // TM
LAT_TILES_PER_SEQ = S_LAT // TM
HALO = 16

C_HP = 0
C_CQ = C_HP + POOL_WIDTH
C_CKV = C_CQ + MLA_Q_LORA
C_KR = C_CKV + MLA_KV_LORA
C_GQ = C_KR + LANES
C_GK = C_GQ + GQA_HEADS * LANES
C_GV = C_GK + GQA_KV * LANES
C_NK = C_GV + GQA_KV * LANES
C_NV = C_NK + GQA_KV * GQA_HD
C_G = C_NV + GQA_KV * GQA_HD
C_END = C_G + 3 * D_MODEL

TME = 256
N_PAIRS = T_ALL * TOP_K
NT_E = N_PAIRS // TME + N_EXPERTS
P_ROWS = NT_E * TME


def _cparams(n_axes):
    return pltpu.CompilerParams(dimension_semantics=("arbitrary",) * n_axes, vmem_limit_bytes=VMEM_LIMIT)


def _dot(a, b):
    return jnp.dot(a, b, preferred_element_type=F32)


def _dot_t(a, b):
    return lax.dot_general(a, b, (((1,), (1,)), ((), ())), preferred_element_type=F32)


def _split(x):
    hi = x.astype(BF16)
    lo = (x - hi.astype(F32)).astype(BF16)
    return hi, lo


def _dot3(a, b):
    a_hi, a_lo = _split(a)
    b_hi, b_lo = _split(b)
    return _dot(a_hi, b_hi) + _dot(a_hi, b_lo) + _dot(a_lo, b_hi)


def _sigmoid(x):
    return 1.0 / (1.0 + jnp.exp(-x))


def _layer_norm(z, g, b):
    mu = jnp.mean(z, axis=-1, keepdims=True)
    zc = z - mu
    var = jnp.mean(zc * zc, axis=-1, keepdims=True)
    return zc * lax.rsqrt(var + LN_EPS) * g + b


def _mod_row(i):
    return jnp.where(i < CTX_TILES, 0, 1 + (i - CTX_TILES) // LAT_TILES_PER_SEQ)


def _pos_block(i):
    return jnp.where(i < CTX_TILES, LAT_TILES_PER_SEQ, (i - CTX_TILES) % LAT_TILES_PER_SEQ)


def _const_spec(shape):
    nd = len(shape)
    return pl.BlockSpec(shape, lambda *_: (0,) * nd)


def _ada_kernel(c_ref, w_ref, b_ref, o_ref):
    c = c_ref[...]
    s = c * _sigmoid(c)
    o_ref[0] = _dot3(s, w_ref[0]) + b_ref[0]


def _ada_params(cond8, w_ada, b_ada):
    n_col = 6 * D_MODEL
    blk = 1024
    return pl.pallas_call(
        _ada_kernel,
        grid=(DEPTH, n_col // blk),
        in_specs=[
            pl.BlockSpec((SUBLANES, D_MODEL), lambda l, j: (0, 0)),
            pl.BlockSpec((1, D_MODEL, blk), lambda l, j: (l, 0, j)),
            pl.BlockSpec((1, 1, blk), lambda l, j: (l, 0, j)),
        ],
        out_specs=pl.BlockSpec((1, SUBLANES, blk), lambda l, j: (l, 0, j)),
        out_shape=jax.ShapeDtypeStruct((DEPTH, SUBLANES, n_col), F32),
        compiler_params=_cparams(2),
        name="ada_params",
    )(cond8, w_ada, b_ada.reshape(DEPTH, 1, n_col))


def _rope(x, tab_ref, shift):
    return (x * tab_ref[0]
            + pltpu.roll(x, LANES - shift, 1) * tab_ref[1]
            + pltpu.roll(x, shift, 1) * tab_ref[2])


def _rms(h, g):
    return h * lax.rsqrt(jnp.mean(h * h, axis=-1, keepdims=True) + RMS_EPS) * g


def _in_kernel(x_ref, mod_ref, wp_ref, qn_ref, kvn_ref, wuq_ref, wk_ref, wv_ref, tg_ref, tq_ref, tk_ref,
               hp_ref, qm_ref, ckv_ref, kr_ref, km_ref, vm_ref, qg_ref, kg_ref, vg_ref, nk_ref, nv_ref, sg_ref):
    x = x_ref[...]
    sh = mod_ref[0, :, 0:D_MODEL]
    sc = mod_ref[0, :, D_MODEL:2 * D_MODEL]
    u = (x * (1.0 + sc) + sh).astype(BF16)

    def proj(lo, hi):
        return _dot(u, wp_ref[:, lo:hi])

    hp_ref[...] = proj(C_HP, C_CQ)

    qn = _rms(proj(C_CQ, C_CKV), qn_ref[...]).astype(BF16)
    q = _dot(qn, wuq_ref[...])
    for h in range(MLA_HEADS):
        qm_ref[:, h * LANES:(h + 1) * LANES] = _rope(q[:, h * LANES:(h + 1) * LANES], tq_ref, MLA_ROPE // 4).astype(BF16)

    ckv = _rms(proj(C_CKV, C_KR), kvn_ref[...])
    ckv_ref[...] = ckv
    kr = _rope(proj(C_KR, C_GQ), tk_ref, MLA_ROPE // 4)
    kr_ref[...] = kr
    ckv_b = ckv.astype(BF16)
    km_ref[...] = (_dot(ckv_b, wk_ref[0:MLA_KV_LORA, :]) + _dot(kr.astype(BF16), wk_ref[MLA_KV_LORA:, :])).astype(BF16)
    vm_ref[...] = _dot(ckv_b, wv_ref[...]).astype(BF16)

    hq = proj(C_GQ, C_GK)
    for h in range(GQA_HEADS):
        qg_ref[:, h * LANES:(h + 1) * LANES] = _rope(hq[:, h * LANES:(h + 1) * LANES], tg_ref, GQA_HD // 4).astype(BF16)
    hk = proj(C_GK, C_GV)
    for g in range(GQA_KV):
        kg_ref[:, g * LANES:(g + 1) * LANES] = _rope(hk[:, g * LANES:(g + 1) * LANES], tg_ref, GQA_HD // 4).astype(BF16)
    vg_ref[...] = proj(C_GV, C_NK).astype(BF16)
    nk_ref[...] = proj(C_NK, C_NV)
    nv_ref[...] = proj(C_NV, C_G)

    for j in range(3):
        sg_ref[:, j * D_MODEL:(j + 1) * D_MODEL] = _sigmoid(proj(C_G + j * D_MODEL, C_G + (j + 1) * D_MODEL)).astype(BF16)


def _in_proj(x, mod, wp, qn, kvn, wuq, wk, wv, tab_g, tab_q, tab_k):
    tok = lambda w: pl.BlockSpec((TM, w), lambda i: (i, 0))
    tab = pl.BlockSpec((3, TM, LANES), lambda i: (0, _pos_block(i), 0))
    out_widths = [(POOL_WIDTH, F32), (MLA_HEADS * LANES, BF16), (MLA_KV_LORA, F32), (LANES, F32),
                  (MLA_HEADS * LANES, BF16), (MLA_HEADS * MLA_V, BF16), (GQA_HEADS * LANES, BF16),
                  (GQA_KV * LANES, BF16), (GQA_KV * LANES, BF16), (GQA_KV * GQA_HD, F32), (GQA_KV * GQA_HD, F32),
                  (3 * D_MODEL, BF16)]
    return pl.pallas_call(
        _in_kernel,
        grid=(N_TILES,),
        in_specs=[
            tok(D_MODEL),
            pl.BlockSpec((1, 1, 6 * D_MODEL), lambda i: (_mod_row(i), 0, 0)),
            _const_spec((D_MODEL, C_END)),
            _const_spec((1, MLA_Q_LORA)),
            _const_spec((1, MLA_KV_LORA)),
            _const_spec((MLA_Q_LORA, MLA_HEADS * LANES)),
            _const_spec((MLA_KV_LORA + LANES, MLA_HEADS * LANES)),
            _const_spec((MLA_KV_LORA, MLA_HEADS * MLA_V)),
            tab, tab, tab,
        ],
        out_specs=[tok(w) for w, _ in out_widths],
        out_shape=[jax.ShapeDtypeStruct((T_ALL, w), dt) for w, dt in out_widths],
        compiler_params=_cparams(1),
        name="in_proj",
    )(x, mod, wp, qn, kvn, wuq, wk, wv, tab_g, tab_q, tab_k)


def _expand_kernel(ckv_ref, kr_ref, wk_ref, wv_ref, km_ref, vm_ref):
    ckv_b = ckv_ref[...].astype(BF16)
    km_ref[...] = (_dot(ckv_b, wk_ref[0:MLA_KV_LORA, :]) + _dot(kr_ref[...].astype(BF16), wk_ref[MLA_KV_LORA:, :])).astype(BF16)
    vm_ref[...] = _dot(ckv_b, wv_ref[...]).astype(BF16)


def _expand_cache(ckv, kr_pad, wk, wv):
    rows = ckv.shape[0]
    blk = PAST_LEN
    return pl.pallas_call(
        _expand_kernel,
        grid=(rows // blk,),
        in_specs=[
            pl.BlockSpec((blk, MLA_KV_LORA), lambda i: (i, 0)),
            pl.BlockSpec((blk, LANES), lambda i: (i, 0)),
            _const_spec((MLA_KV_LORA + LANES, MLA_HEADS * LANES)),
            _const_spec((MLA_KV_LORA, MLA_HEADS * MLA_V)),
        ],
        out_specs=[pl.BlockSpec((blk, MLA_HEADS * LANES), lambda i: (i, 0)),
                   pl.BlockSpec((blk, MLA_HEADS * MLA_V), lambda i: (i, 0))],
        out_shape=[jax.ShapeDtypeStruct((rows, MLA_HEADS * LANES), BF16),
                   jax.ShapeDtypeStruct((rows, MLA_HEADS * MLA_V), BF16)],
        compiler_params=_cparams(1),
        name="mla_expand_cache",
    )(ckv, kr_pad, wk, wv)


def _mla_kernel(n_sets, q_ref, *refs):
    k_refs = refs[:n_sets]
    v_refs = refs[n_sets:2 * n_sets]
    o_ref = refs[2 * n_sets]
    rows = q_ref.shape[0]
    low_half = lax.broadcasted_iota(I32, (rows, LANES), 1) < MLA_V
    for j in range(MLA_HEADS // 2):
        outs = []
        for h in (2 * j, 2 * j + 1):
            qh = q_ref[:, h * LANES:(h + 1) * LANES]
            ss = [_dot_t(qh, k[:, h * LANES:(h + 1) * LANES]) * MLA_SCALE for k in k_refs]
            m = functools.reduce(jnp.maximum, [jnp.max(s, axis=-1, keepdims=True) for s in ss])
            ps = [jnp.exp(s - m) for s in ss]
            den = functools.reduce(jnp.add, [jnp.sum(p, axis=-1, keepdims=True) for p in ps])
            o = functools.reduce(jnp.add, [_dot(p.astype(BF16), v[:, j * LANES:(j + 1) * LANES]) for p, v in zip(ps, v_refs)])
            outs.append(o / den)
        o_ref[:, j * LANES:(j + 1) * LANES] = jnp.where(low_half, outs[0], outs[1]).astype(BF16)


def _mla_ctx(qm, km, vm):
    blk = S_CTX
    return pl.pallas_call(
        functools.partial(_mla_kernel, 1),
        grid=(B_CTX,),
        in_specs=[pl.BlockSpec((blk, MLA_HEADS * LANES), lambda b: (b, 0)),
                  pl.BlockSpec((blk, MLA_HEADS * LANES), lambda b: (b, 0)),
                  pl.BlockSpec((blk, MLA_HEADS * MLA_V), lambda b: (b, 0))],
        out_specs=pl.BlockSpec((blk, MLA_HEADS * MLA_V), lambda b: (b, 0)),
        out_shape=jax.ShapeDtypeStruct((T_CTX, MLA_HEADS * MLA_V), BF16),
        compiler_params=_cparams(1),
        name="mla_attn_ctx",
    )(qm, km, vm)


def _mla_lat(qm, km, vm, km_c, vm_c):
    lat0 = T_CTX // S_LAT
    return pl.pallas_call(
        functools.partial(_mla_kernel, 2),
        grid=(B_LAT, LAT_TILES_PER_SEQ),
        in_specs=[pl.BlockSpec((TM, MLA_HEADS * LANES), lambda b, t: (CTX_TILES + b * LAT_TILES_PER_SEQ + t, 0)),
                  pl.BlockSpec((PAST_LEN, MLA_HEADS * LANES), lambda b, t: (b, 0)),
                  pl.BlockSpec((S_LAT, MLA_HEADS * LANES), lambda b, t: (lat0 + b, 0)),
                  pl.BlockSpec((PAST_LEN, MLA_HEADS * MLA_V), lambda b, t: (b, 0)),
                  pl.BlockSpec((S_LAT, MLA_HEADS * MLA_V), lambda b, t: (lat0 + b, 0))],
        out_specs=pl.BlockSpec((TM, MLA_HEADS * MLA_V), lambda b, t: (b * LAT_TILES_PER_SEQ + t, 0)),
        out_shape=jax.ShapeDtypeStruct((T_LAT, MLA_HEADS * MLA_V), BF16),
        compiler_params=_cparams(2),
        name="mla_attn_lat",
    )(qm, km_c, km, vm_c, vm)


def _gqa_kernel(band, sink_ref, q_ref, *refs):
    n_sets = 4 if band else 1
    k_refs = refs[:n_sets]
    v_refs = refs[n_sets:2 * n_sets]
    o_ref = refs[2 * n_sets]
    rows = q_ref.shape[0]
    low_half = lax.broadcasted_iota(I32, (rows, LANES), 1) < GQA_HD
    masks = [None] * n_sets
    if band:
        n = pl.program_id(1)
        n_blk = pl.num_programs(1)
        qi = lax.broadcasted_iota(I32, (GQA_GROUP * rows, WBLK), 0) % rows
        kj = lax.broadcasted_iota(I32, (GQA_GROUP * rows, WBLK), 1)
        masks[0] = kj >= qi + jnp.where(n > 0, 0, WBLK)
        masks[2] = kj <= qi - jnp.where(n < n_blk - 1, 0, WBLK)
    for g in range(GQA_KV):
        qs = jnp.concatenate([q_ref[:, (GQA_GROUP * g + i) * LANES:(GQA_GROUP * g + i + 1) * LANES]
                              for i in range(GQA_GROUP)], axis=0)
        sink = jnp.concatenate([jnp.full((rows, 1), sink_ref[GQA_GROUP * g + i], F32) for i in range(GQA_GROUP)], axis=0)
        ss = []
        for k, msk in zip(k_refs, masks):
            s = _dot_t(qs, k[:, g * LANES:(g + 1) * LANES]) * GQA_SCALE
            ss.append(s if msk is None else jnp.where(msk, s, NEG_INF))
        m = functools.reduce(jnp.maximum, [jnp.max(s, axis=-1, keepdims=True) for s in ss] + [sink])
        ps = [jnp.exp(s - m) for s in ss]
        den = functools.reduce(jnp.add, [jnp.sum(p, axis=-1, keepdims=True) for p in ps] + [jnp.exp(sink - m)])
        o = functools.reduce(jnp.add, [_dot(p.astype(BF16), v[:, g * LANES:(g + 1) * LANES]) for p, v in zip(ps, v_refs)])
        o = o / den
        for jj in range(GQA_GROUP // 2):
            pair = jnp.where(low_half, o[(2 * jj) * rows:(2 * jj + 1) * rows], o[(2 * jj + 1) * rows:(2 * jj + 2) * rows])
            col = (GQA_GROUP // 2 * g + jj) * LANES
            o_ref[:, col:col + LANES] = pair.astype(BF16)


def _gqa_ctx(sink, qg, kg, vg):
    blk = S_CTX
    grid_spec = pltpu.PrefetchScalarGridSpec(
        num_scalar_prefetch=1,
        grid=(B_CTX,),
        in_specs=[pl.BlockSpec((blk, GQA_HEADS * LANES), lambda b, s: (b, 0)),
                  pl.BlockSpec((blk, GQA_KV * LANES), lambda b, s: (b, 0)),
                  pl.BlockSpec((blk, GQA_KV * LANES), lambda b, s: (b, 0))],
        out_specs=pl.BlockSpec((blk, GQA_HEADS * GQA_HD), lambda b, s: (b, 0)),
    )
    return pl.pallas_call(
        functools.partial(_gqa_kernel, False),
        grid_spec=grid_spec,
        out_shape=jax.ShapeDtypeStruct((T_CTX, GQA_HEADS * GQA_HD), BF16),
        compiler_params=_cparams(1),
        name="gqa_attn_ctx",
    )(sink, qg, kg, vg)


def _gqa_lat(sink, qg, kg, vg, kg_c, vg_c):
    nb = S_LAT // WBLK
    first = T_CTX // WBLK

    def blk(off):
        def index_map(b, n, s):
            return (first + b * nb + jnp.clip(n + off, 0, nb - 1), 0)
        return index_map

    kv_w = GQA_KV * LANES
    grid_spec = pltpu.PrefetchScalarGridSpec(
        num_scalar_prefetch=1,
        grid=(B_LAT, nb),
        in_specs=[pl.BlockSpec((WBLK, GQA_HEADS * LANES), blk(0)),
                  pl.BlockSpec((WBLK, kv_w), blk(-1)), pl.BlockSpec((WBLK, kv_w), blk(0)), pl.BlockSpec((WBLK, kv_w), blk(1)),
                  pl.BlockSpec((PAST_LEN, kv_w), lambda b, n, s: (b, 0)),
                  pl.BlockSpec((WBLK, kv_w), blk(-1)), pl.BlockSpec((WBLK, kv_w), blk(0)), pl.BlockSpec((WBLK, kv_w), blk(1)),
                  pl.BlockSpec((PAST_LEN, kv_w), lambda b, n, s: (b, 0))],
        out_specs=pl.BlockSpec((WBLK, GQA_HEADS * GQA_HD), lambda b, n, s: (b * nb + n, 0)),
    )
    return pl.pallas_call(
        functools.partial(_gqa_kernel, True),
        grid_spec=grid_spec,
        out_shape=jax.ShapeDtypeStruct((T_LAT, GQA_HEADS * GQA_HD), BF16),
        compiler_params=_cparams(2),
        name="gqa_attn_lat",
    )(sink, qg, kg, kg, kg, kg_c, vg, vg, vg, vg_c)


def _pool_kernel(hp_p, hp_c, hp_n, pw_ref, ps_ref, o_ref):
    i = pl.program_id(0)
    is_ctx = i < CTX_TILES
    t4 = (i - CTX_TILES) % LAT_TILES_PER_SEQ
    seq_len = jnp.where(is_ctx, S_CTX, S_LAT)
    base = jnp.where(is_ctx, 0, t4 * TM)

    n_keys = TM + 2 * HALO
    first_key = jnp.where(jnp.logical_or(is_ctx, t4 == 0), HALO, 0)
    end_key = jnp.where(jnp.logical_or(is_ctx, t4 == LAT_TILES_PER_SEQ - 1), HALO + TM, n_keys)
    qi = lax.broadcasted_iota(I32, (TM, n_keys), 0)
    kj = lax.broadcasted_iota(I32, (TM, n_keys), 1)
    rel = kj - HALO - qi
    key_ok = (kj >= first_key) & (kj < end_key)

    cur = hp_c[...]
    keys = jnp.concatenate([hp_p[TM - HALO:TM, :], cur, hp_n[0:HALO, :]], axis=0)
    k_hi, k_lo = _split(keys)
    qpos = base + lax.broadcasted_iota(I32, (TM, 1), 0)
    for g, w in enumerate(POOL_WINDOWS):
        a = jnp.where((rel >= -(w // 2)) & (rel <= w // 2 - 1) & key_ok, 1.0, 0.0).astype(BF16)
        cols = slice(g * POOL_GROUP, (g + 1) * POOL_GROUP)
        s = _dot(a, k_hi[:, cols]) + _dot(a, k_lo[:, cols])
        lo = jnp.maximum(qpos - w // 2, 0)
        hi = jnp.minimum(qpos + w // 2 - 1, seq_len - 1)
        cnt = (hi - lo + 1).astype(F32)
        d = s / cnt - cur[:, cols]
        y = _dot(d.astype(BF16), pw_ref[g]) * ps_ref[:, cols]
        o_ref[:, cols] = y.astype(BF16)


def _pool(hp, pw, ps):
    return pl.pallas_call(
        _pool_kernel,
        grid=(N_TILES,),
        in_specs=[pl.BlockSpec((TM, POOL_WIDTH), lambda i: (jnp.maximum(i - 1, 0), 0)),
                  pl.BlockSpec((TM, POOL_WIDTH), lambda i: (i, 0)),
                  pl.BlockSpec((TM, POOL_WIDTH), lambda i: (jnp.minimum(i + 1, N_TILES - 1), 0)),
                  _const_spec((len(POOL_WINDOWS), POOL_GROUP, POOL_GROUP)),
                  _const_spec((1, POOL_WIDTH))],
        out_specs=pl.BlockSpec((TM, POOL_WIDTH), lambda i: (i, 0)),
        out_shape=jax.ShapeDtypeStruct((T_ALL, POOL_WIDTH), BF16),
        compiler_params=_cparams(1),
        name="pool_mix",
    )(hp, hp, hp, pw, ps)


def _merge_kernel(x_ref, mod_ref, yp_ref, ymc_ref, yml_ref, ygc_ref, ygl_ref, sg_ref,
                  wbp_ref, wbm_ref, wbg_ref, wo_ref, lg_ref, lb_ref, rw_ref, rb_ref,
                  x1_ref, u2_ref, ri_ref, rwt_ref, cnt_ref, carry):
    i = pl.program_id(0)

    @pl.when(i == 0)
    def _():
        carry[...] = jnp.zeros_like(carry)

    is_ctx = i < CTX_TILES
    ym = jnp.where(is_ctx, ymc_ref[...], yml_ref[...])
    yg = jnp.where(is_ctx, ygc_ref[...], ygl_ref[...])
    m = (sg_ref[:, 0:D_MODEL].astype(F32) * _dot(yp_ref[...], wbp_ref[...])
         + sg_ref[:, D_MODEL:2 * D_MODEL].astype(F32) * _dot(ym, wbm_ref[...])
         + sg_ref[:, 2 * D_MODEL:3 * D_MODEL].astype(F32) * _dot(yg, wbg_ref[...]))
    y = _dot(m.astype(BF16), wo_ref[...])
    gate1 = mod_ref[0, :, 2 * D_MODEL:3 * D_MODEL]
    x1 = _layer_norm(ALPHA_DN * x_ref[...] + gate1 * y, lg_ref[...], lb_ref[...])
    x1_ref[...] = x1
    sh2 = mod_ref[0, :, 3 * D_MODEL:4 * D_MODEL]
    sc2 = mod_ref[0, :, 4 * D_MODEL:5 * D_MODEL]
    u2 = x1 * (1.0 + sc2) + sh2
    u2_ref[...] = u2

    logits = _dot3(u2, rw_ref[...]) + rb_ref[...]
    lane = lax.broadcasted_iota(I32, (TM, LANES), 1)
    vals, idxs = [], []
    rest = logits
    for _ in range(TOP_K):
        mx = jnp.max(rest, axis=-1, keepdims=True)
        ix = jnp.min(jnp.where(rest == mx, lane, LANES), axis=-1, keepdims=True)
        vals.append(mx)
        idxs.append(ix)
        rest = jnp.where(lane == ix, -jnp.inf, rest)
    es = [jnp.exp(v - vals[0]) for v in vals]
    den = functools.reduce(jnp.add, es)

    sel = functools.reduce(jnp.add, [jnp.where(lane == ix, 1.0, 0.0) for ix in idxs])
    r_i = lax.broadcasted_iota(I32, (TM, TM), 0)
    c_i = lax.broadcasted_iota(I32, (TM, TM), 1)
    below = jnp.where(c_i < r_i, 1.0, 0.0).astype(BF16)
    rank = _dot(below, sel.astype(BF16)) + carry[...]
    carry[...] = carry[...] + jnp.sum(sel, axis=0, keepdims=True)
    cnt_ref[...] = carry[...]

    ri = jnp.zeros((TM, LANES), I32)
    rwt = jnp.zeros((TM, LANES), F32)
    for k in range(TOP_K):
        rk = jnp.sum(jnp.where(lane == idxs[k], rank, 0.0), axis=-1, keepdims=True).astype(I32)
        ri = jnp.where(lane == k, idxs[k], ri)
        ri = jnp.where(lane == TOP_K + k, rk, ri)
        rwt = jnp.where(lane == k, es[k] / den, rwt)
    ri_ref[...] = ri
    rwt_ref[...] = rwt


def _merge(x, mod, yp, ymc, yml, ygc, ygl, sg, wbp, wbm, wbg, wo, lg, lb, rw, rb):
    tok = lambda w: pl.BlockSpec((TM, w), lambda i: (i, 0))
    ctx = lambda w: pl.BlockSpec((TM, w), lambda i: (jnp.minimum(i, CTX_TILES - 1), 0))
    lat = lambda w: pl.BlockSpec((TM, w), lambda i: (jnp.maximum(i - CTX_TILES, 0), 0))
    return pl.pallas_call(
        _merge_kernel,
        grid=(N_TILES,),
        in_specs=[tok(D_MODEL),
                  pl.BlockSpec((1, 1, 6 * D_MODEL), lambda i: (_mod_row(i), 0, 0)),
                  tok(POOL_WIDTH), ctx(512), lat(512), ctx(512), lat(512), tok(3 * D_MODEL),
                  _const_spec((POOL_WIDTH, D_MODEL)), _const_spec((512, D_MODEL)), _const_spec((512, D_MODEL)),
                  _const_spec((D_MODEL, D_MODEL)), _const_spec((1, D_MODEL)), _const_spec((1, D_MODEL)),
                  _const_spec((D_MODEL, LANES)), _const_spec((1, LANES))],
        out_specs=[tok(D_MODEL), tok(D_MODEL), tok(LANES), tok(LANES), _const_spec((1, LANES))],
        out_shape=[jax.ShapeDtypeStruct((T_ALL, D_MODEL), F32), jax.ShapeDtypeStruct((T_ALL, D_MODEL), F32),
                   jax.ShapeDtypeStruct((T_ALL, LANES), I32), jax.ShapeDtypeStruct((T_ALL, LANES), F32),
                   jax.ShapeDtypeStruct((1, LANES), F32)],
        scratch_shapes=[pltpu.VMEM((1, LANES), F32)],
        compiler_params=_cparams(1),
        name="merge_route",
    )(x, mod, yp, ymc, yml, ygc, ygl, sg, wbp, wbm, wbg, wo, lg, lb, rw, rb)


def _row_copy(src_ref, src_row, dst_ref, dst_row, sem):
    return pltpu.make_async_copy(src_ref.at[pl.ds(src_row, 1), :], dst_ref.at[pl.ds(dst_row, 1), :], sem)


def _dispatch_kernel(dest_ref, u_ref, xs_in_ref, xs_ref, sem):
    del xs_in_ref
    base = pl.program_id(0) * (TM * TOP_K)

    def issue(j, c):
        _row_copy(u_ref, j // TOP_K, xs_ref, dest_ref[base + j], sem).start()
        return c

    lax.fori_loop(0, TM * TOP_K, issue, 0)

    def drain(j, c):
        _row_copy(u_ref, 0, xs_ref, 0, sem).wait()
        return c

    lax.fori_loop(0, TM * TOP_K, drain, 0)


def _dispatch(dest, u2, xs_zero):
    grid_spec = pltpu.PrefetchScalarGridSpec(
        num_scalar_prefetch=1,
        grid=(N_TILES,),
        in_specs=[pl.BlockSpec((TM, D_MODEL), lambda i, d: (i, 0)),
                  pl.BlockSpec(memory_space=pl.ANY)],
        out_specs=pl.BlockSpec(memory_space=pl.ANY),
        scratch_shapes=[pltpu.SemaphoreType.DMA(())],
    )
    return pl.pallas_call(
        _dispatch_kernel,
        grid_spec=grid_spec,
        out_shape=jax.ShapeDtypeStruct((P_ROWS, D_MODEL), F32),
        input_output_aliases={2: 0},
        compiler_params=_cparams(1),
        name="moe_dispatch",
    )(dest, u2, xs_zero)


def _expert_kernel(te_ref, nu_ref, xs_ref, wgu_ref, bgu_ref, wd_ref, bd_ref, ys_ref, wgu_s, wd_s):
    i = pl.program_id(0)
    e = te_ref[i]
    prev = te_ref[jnp.maximum(i - 1, 0)]

    @pl.when(jnp.logical_or(i == 0, e != prev))
    def _():
        wgu_s[...] = wgu_ref[0, 0].astype(BF16)
        wd_s[...] = wd_ref[0, 0].astype(BF16)

    @pl.when(i < nu_ref[0])
    def _():
        h = _dot(xs_ref[...].astype(BF16), wgu_s[...]) + bgu_ref[0, 0]
        glu = jnp.minimum(h[:, 0:D_FF], SWIGLU_LIMIT)
        lin = jnp.clip(h[:, D_FF:2 * D_FF], -SWIGLU_LIMIT, SWIGLU_LIMIT)
        a = glu * _sigmoid(SWIGLU_ALPHA * glu) * (lin + 1.0)
        ys_ref[...] = _dot(a.astype(BF16), wd_s[...]) + bd_ref[0, 0]

    @pl.when(i >= nu_ref[0])
    def _():
        ys_ref[...] = jnp.zeros_like(ys_ref)


def _experts(layer, tile_expert, n_used, xs, w_gate_up, b_gate_up, w_down, b_down):
    row = lambda i, te, nu: (jnp.minimum(i, nu[0] - 1), 0)
    grid_spec = pltpu.PrefetchScalarGridSpec(
        num_scalar_prefetch=2,
        grid=(NT_E,),
        in_specs=[pl.BlockSpec((TME, D_MODEL), row),
                  pl.BlockSpec((1, 1, D_MODEL, 2 * D_FF), lambda i, te, nu: (layer, te[i], 0, 0)),
                  pl.BlockSpec((1, 1, 1, 2 * D_FF), lambda i, te, nu: (layer, te[i], 0, 0)),
                  pl.BlockSpec((1, 1, D_FF, D_MODEL), lambda i, te, nu: (layer, te[i], 0, 0)),
                  pl.BlockSpec((1, 1, 1, D_MODEL), lambda i, te, nu: (layer, te[i], 0, 0))],
        out_specs=pl.BlockSpec((TME, D_MODEL), lambda i, te, nu: (i, 0)),
        scratch_shapes=[pltpu.VMEM((D_MODEL, 2 * D_FF), BF16), pltpu.VMEM((D_FF, D_MODEL), BF16)],
    )
    return pl.pallas_call(
        _expert_kernel,
        grid_spec=grid_spec,
        out_shape=jax.ShapeDtypeStruct((P_ROWS, D_MODEL), F32),
        compiler_params=_cparams(1),
        name="moe_experts",
    )(tile_expert, n_used, xs, w_gate_up,
      b_gate_up.reshape(DEPTH, N_EXPERTS, 1, 2 * D_FF), w_down, b_down.reshape(DEPTH, N_EXPERTS, 1, D_MODEL))


def _combine_kernel(dest_ref, ys_ref, rwt_ref, x1_ref, mod_ref, lg_ref, lb_ref, o_ref, buf, sem):
    base = pl.program_id(0) * (TM * TOP_K)

    def issue(j, c):
        _row_copy(ys_ref, dest_ref[base + j], buf.at[j % TOP_K], j // TOP_K, sem).start()
        return c

    lax.fori_loop(0, TM * TOP_K, issue, 0)

    def drain(j, c):
        _row_copy(ys_ref, 0, buf.at[0], 0, sem).wait()
        return c

    lax.fori_loop(0, TM * TOP_K, drain, 0)

    moe = functools.reduce(jnp.add, [rwt_ref[:, k:k + 1] * buf[k] for k in range(TOP_K)])
    gate2 = mod_ref[0, :, 5 * D_MODEL:6 * D_MODEL]
    o_ref[...] = _layer_norm(ALPHA_DN * x1_ref[...] + gate2 * moe, lg_ref[...], lb_ref[...])


def _combine(dest, ys, rwt, x1, mod, lg, lb):
    grid_spec = pltpu.PrefetchScalarGridSpec(
        num_scalar_prefetch=1,
        grid=(N_TILES,),
        in_specs=[pl.BlockSpec(memory_space=pl.ANY),
                  pl.BlockSpec((TM, LANES), lambda i, d: (i, 0)),
                  pl.BlockSpec((TM, D_MODEL), lambda i, d: (i, 0)),
                  pl.BlockSpec((1, 1, 6 * D_MODEL), lambda i, d: (_mod_row(i), 0, 0)),
                  pl.BlockSpec((1, D_MODEL), lambda i, d: (0, 0)),
                  pl.BlockSpec((1, D_MODEL), lambda i, d: (0, 0))],
        out_specs=pl.BlockSpec((TM, D_MODEL), lambda i, d: (i, 0)),
        scratch_shapes=[pltpu.VMEM((TOP_K, TM, D_MODEL), F32), pltpu.SemaphoreType.DMA(())],
    )
    return pl.pallas_call(
        _combine_kernel,
        grid_spec=grid_spec,
        out_shape=jax.ShapeDtypeStruct((T_ALL, D_MODEL), F32),
        compiler_params=_cparams(1),
        name="moe_combine",
    )(dest, ys, rwt, x1, mod, lg, lb)


def _pad_heads(w, n_heads, width):
    rows = w.shape[0]
    w = w.reshape(rows, n_heads, width)
    return jnp.pad(w, ((0, 0), (0, 0), (0, LANES - width))).reshape(rows, n_heads * LANES)


def _pack_w_in(w):
    cuts = [0, 512, 896, 1152, 1184, 1696, 1824, 1952, 5024]
    hp, cq, ckv, kr, q, k, v, g = [w[:, a:b] for a, b in zip(cuts[:-1], cuts[1:])]
    v_dup = jnp.concatenate([v.reshape(D_MODEL, GQA_KV, 1, GQA_HD)] * 2, axis=2).reshape(D_MODEL, GQA_KV * LANES)
    packed = jnp.concatenate([hp, cq, ckv, _pad_heads(kr, 1, MLA_ROPE), _pad_heads(q, GQA_HEADS, GQA_HD),
                              _pad_heads(k, GQA_KV, GQA_HD), v_dup, k, v, g], axis=1)
    return packed.astype(BF16)


def _pack_mla(wuq, wukv):
    wuq_p = _pad_heads(wuq, MLA_HEADS, MLA_QK).astype(BF16)
    kv = wukv.reshape(MLA_KV_LORA, MLA_HEADS, MLA_NOPE + MLA_V)
    wk_top = _pad_heads(kv[:, :, :MLA_NOPE].reshape(MLA_KV_LORA, MLA_HEADS * MLA_NOPE), MLA_HEADS, MLA_NOPE)
    place = jnp.zeros((LANES, MLA_HEADS, LANES), F32)
    r = jnp.arange(MLA_ROPE)
    place = place.at[r, :, MLA_NOPE + r].set(1.0).reshape(LANES, MLA_HEADS * LANES)
    wk = jnp.concatenate([wk_top, place], axis=0).astype(BF16)
    wv = kv[:, :, MLA_NOPE:].reshape(MLA_KV_LORA, MLA_HEADS * MLA_V).astype(BF16)
    return wuq_p, wk, wv


def _rope_table(dim, lane0):
    quarter = dim // 4
    pos = jnp.arange(S_LAT)
    rows = (pos // GRID_W).astype(F32)
    cols = (pos % GRID_W).astype(F32)
    freqs = jnp.power(ROPE_THETA, -jnp.arange(quarter, dtype=F32) / quarter)
    ang_r = rows[:, None] * freqs[None, :]
    ang_c = cols[:, None] * freqs[None, :]
    zero = jnp.zeros((S_LAT, quarter), F32)
    cos = jnp.concatenate([jnp.cos(ang_r), jnp.cos(ang_r), jnp.cos(ang_c), jnp.cos(ang_c)], axis=1)
    s_up = jnp.concatenate([-jnp.sin(ang_r), zero, -jnp.sin(ang_c), zero], axis=1)
    s_dn = jnp.concatenate([zero, jnp.sin(ang_r), zero, jnp.sin(ang_c)], axis=1)

    def place(t, fill):
        full = jnp.full((S_LAT, LANES), fill, F32)
        return full.at[:, lane0:lane0 + dim].set(t)

    planes = jnp.stack([place(cos, 1.0), place(s_up, 0.0), place(s_dn, 0.0)])
    ident = jnp.stack([jnp.ones((TM, LANES), F32), jnp.zeros((TM, LANES), F32), jnp.zeros((TM, LANES), F32)])
    return jnp.concatenate([planes, ident], axis=1)


def _routing_plan(ri, counts):
    idx = ri[:, 0:TOP_K]
    rank = ri[:, TOP_K:2 * TOP_K]
    cnt = counts[0, :N_EXPERTS].astype(I32)
    padded = (cnt + TME - 1) // TME * TME
    ends = jnp.cumsum(padded)
    starts = ends - padded
    dest = (starts[idx] + rank).reshape(-1).astype(I32)
    n_used = (ends[-1] // TME).astype(I32)
    tile_row = jnp.arange(NT_E, dtype=I32) * TME
    tile_expert = jnp.sum((tile_row[:, None] >= ends[None, :]).astype(I32), axis=1)
    last = jnp.sum((jnp.maximum(ends[-1] - TME, 0) >= ends).astype(I32))
    tile_expert = jnp.minimum(tile_expert, last).astype(I32)
    return dest, tile_expert, n_used.reshape(1)


def kernel(x_prompt, x_sample, c, cache_mla_ckv, cache_mla_krope, cache_gqa_k, cache_gqa_v, c_ctx, w_ada, b_ada, w_in, mla_q_norm, mla_kv_norm, w_mla_uq, w_mla_ukv, gqa_sink, pool_w, pool_scale, w_branch_pool, w_branch_mla, w_branch_gqa, w_out, ln1_g, ln1_b, ln2_g, ln2_b, router_w, router_b, w_gate_up, b_gate_up, w_down, b_down):
    x = jnp.concatenate([x_prompt.reshape(T_CTX, D_MODEL), x_sample.reshape(T_LAT, D_MODEL)], axis=0)
    cond8 = jnp.concatenate([c_ctx[None, :], c, jnp.zeros((SUBLANES - 1 - B_LAT, D_MODEL), F32)], axis=0)
    mod_all = _ada_params(cond8, w_ada, b_ada)

    tab_g = _rope_table(GQA_HD, 0)
    tab_q = _rope_table(MLA_ROPE, MLA_NOPE)
    tab_k = _rope_table(MLA_ROPE, 0)

    outs = {"ckv": [], "kr": [], "gk": [], "gv": []}
    for l in range(DEPTH):
        mod = mod_all[l].reshape(SUBLANES, 1, 6 * D_MODEL)
        wp = _pack_w_in(w_in[l])
        wuq_p, wk, wv = _pack_mla(w_mla_uq[l], w_mla_ukv[l])

        (hp, qm, ckv, kr, km, vm, qg, kg, vg, nk, nv, sg) = _in_proj(
            x, mod, wp, mla_q_norm[l][None, :], mla_kv_norm[l][None, :], wuq_p, wk, wv, tab_g, tab_q, tab_k)
        outs["ckv"].append(ckv[:T_CTX].reshape(B_CTX, S_CTX, MLA_KV_LORA))
        outs["kr"].append(kr[:T_CTX, :MLA_ROPE].reshape(B_CTX, S_CTX, MLA_ROPE))
        outs["gk"].append(nk[:T_CTX].reshape(B_CTX, S_CTX, GQA_KV, GQA_HD))
        outs["gv"].append(nv[:T_CTX].reshape(B_CTX, S_CTX, GQA_KV, GQA_HD))

        ckv_c = cache_mla_ckv[:, l].reshape(B_LAT * PAST_LEN, MLA_KV_LORA)
        kr_c = jnp.pad(cache_mla_krope[:, l].reshape(B_LAT * PAST_LEN, MLA_ROPE), ((0, 0), (0, LANES - MLA_ROPE)))
        km_c, vm_c = _expand_cache(ckv_c, kr_c, wk, wv)
        gk_c = _pad_heads(cache_gqa_k[:, l].reshape(B_LAT * PAST_LEN, GQA_KV * GQA_HD), GQA_KV, GQA_HD).astype(BF16)
        gv_c = cache_gqa_v[:, l].reshape(B_LAT * PAST_LEN, GQA_KV, 1, GQA_HD)
        gv_c = jnp.concatenate([gv_c, gv_c], axis=2).reshape(B_LAT * PAST_LEN, GQA_KV * LANES).astype(BF16)

        yp = _pool(hp, pool_w[l].astype(BF16), pool_scale[l][None, :])
        ymc = _mla_ctx(qm, km, vm)
        yml = _mla_lat(qm, km, vm, km_c, vm_c)
        ygc = _gqa_ctx(gqa_sink[l], qg, kg, vg)
        ygl = _gqa_lat(gqa_sink[l], qg, kg, vg, gk_c, gv_c)

        rw = jnp.pad(router_w[l], ((0, 0), (0, LANES - N_EXPERTS)))
        rb = jnp.concatenate([router_b[l], jnp.full((LANES - N_EXPERTS,), -jnp.inf, F32)])[None, :]
        x1, u2, ri, rwt, counts = _merge(
            x, mod, yp, ymc, yml, ygc, ygl, sg,
            w_branch_pool[l].astype(BF16), w_branch_mla[l].astype(BF16), w_branch_gqa[l].astype(BF16),
            w_out[l].astype(BF16), ln1_g[l][None, :], ln1_b[l][None, :], rw, rb)

        dest, tile_expert, n_used = _routing_plan(ri, counts)
        xs = _dispatch(dest, u2, jnp.zeros((P_ROWS, D_MODEL), F32))
        ys = _experts(l, tile_expert, n_used, xs, w_gate_up, b_gate_up, w_down, b_down)
        x = _combine(dest, ys, rwt, x1, mod, ln2_g[l][None, :], ln2_b[l][None, :])

    y_prompt = x[:T_CTX].reshape(B_CTX, S_CTX, D_MODEL)
    y_sample = x[T_CTX:].reshape(B_LAT, S_LAT, D_MODEL)
    return (y_prompt, y_sample, jnp.stack(outs["ckv"], axis=1), jnp.stack(outs["kr"], axis=1),
            jnp.stack(outs["gk"], axis=1), jnp.stack(outs["gv"], axis=1))
```

```python
import functools
import math

import jax
import jax.numpy as jnp
from jax import lax
from jax.experimental import pallas as pl
from jax.experimental.pallas import tpu as pltpu

F32 = jnp.float32
BF16 = jnp.bfloat16
I32 = jnp.int32

D_MODEL = 1024
DEPTH = 2
B_CTX, S_CTX = 16, 256
B_LAT, S_LAT = 4, 1024
PAST_LEN = 512
GRID_W = 64
ROPE_THETA = 10000.0
ALPHA_DN = (2 * DEPTH) ** 0.25
LN_EPS = 1e-5
RMS_EPS = 1e-6
NEG_INF = -1e30
POOL_WINDOWS = (2, 4, 8, 16)
POOL_GROUP = 128
POOL_WIDTH = 512
MLA_HEADS = 8
MLA_NOPE = 64
MLA_ROPE = 32
MLA_V = 64
MLA_QK = MLA_NOPE + MLA_ROPE
MLA_Q_LORA = 384
MLA_KV_LORA = 256
MLA_SCALE = 1.0 / math.sqrt(MLA_QK)
GQA_HEADS = 8
GQA_KV = 2
GQA_HD = 64
GQA_GROUP = GQA_HEADS // GQA_KV
GQA_SCALE = 1.0 / math.sqrt(GQA_HD)
WBLK = 128
N_EXPERTS = 32
TOP_K = 4
D_FF = 1024
SWIGLU_LIMIT = 7.0
SWIGLU_ALPHA = 1.702

LANES = 128
SUBLANES = 8
VMEM_LIMIT = 56 * 1024 * 1024

T_CTX = B_CTX * S_CTX
T_LAT = B_LAT * S_LAT
T_ALL = T_CTX + T_LAT
TM = 256
N_TILES = T_ALL // TM
CTX_TILES = T_CTX // TM
LAT_TILES_PER_SEQ = S_LAT // TM
HALO = 16

C_HP = 0
C_CQ = C_HP + POOL_WIDTH
C_CKV = C_CQ + MLA_Q_LORA
C_KR = C_CKV + MLA_KV_LORA
C_GQ = C_KR + LANES
C_GK = C_GQ + GQA_HEADS * LANES
C_GV = C_GK + GQA_KV * LANES
C_NK = C_GV + GQA_KV * LANES
C_NV = C_NK + GQA_KV * GQA_HD
C_G = C_NV + GQA_KV * GQA_HD
C_END = C_G + 3 * D_MODEL

TME = 256
N_PAIRS = T_ALL * TOP_K
NT_E = N_PAIRS // TME + N_EXPERTS
P_ROWS = NT_E * TME


def _cparams(n_axes):
    return pltpu.CompilerParams(dimension_semantics=("arbitrary",) * n_axes, vmem_limit_bytes=VMEM_LIMIT)


def _dot(a, b):
    return jnp.dot(a, b, preferred_element_type=F32)


def _dot_t(a, b):
    return lax.dot_general(a, b, (((1,), (1,)), ((), ())), preferred_element_type=F32)


def _split(x):
    hi = x.astype(BF16)
    lo = (x - hi.astype(F32)).astype(BF16)
    return hi, lo


def _dot3(a, b):
    a_hi, a_lo = _split(a)
    b_hi, b_lo = _split(b)
    return _dot(a_hi, b_hi) + _dot(a_hi, b_lo) + _dot(a_lo, b_hi)


def _sigmoid(x):
    return 1.0 / (1.0 + jnp.exp(-x))


def _layer_norm(z, g, b):
    mu = jnp.mean(z, axis=-1, keepdims=True)
    zc = z - mu
    var = jnp.mean(zc * zc, axis=-1, keepdims=True)
    return zc * lax.rsqrt(var + LN_EPS) * g + b


def _mod_row(i):
    return jnp.where(i < CTX_TILES, 0, 1 + (i - CTX_TILES) // LAT_TILES_PER_SEQ)


def _pos_block(i):
    return jnp.where(i < CTX_TILES, LAT_TILES_PER_SEQ, (i - CTX_TILES) % LAT_TILES_PER_SEQ)


def _const_spec(shape):
    nd = len(shape)
    return pl.BlockSpec(shape, lambda *_: (0,) * nd)


def _ada_kernel(c_ref, w_ref, b_ref, o_ref):
    c = c_ref[...]
    s = c * _sigmoid(c)
    o_ref[0] = _dot3(s, w_ref[0]) + b_ref[0]


def _ada_params(cond8, w_ada, b_ada):
    n_col = 6 * D_MODEL
    blk = 1024
    return pl.pallas_call(
        _ada_kernel,
        grid=(DEPTH, n_col // blk),
        in_specs=[
            pl.BlockSpec((SUBLANES, D_MODEL), lambda l, j: (0, 0)),
            pl.BlockSpec((1, D_MODEL, blk), lambda l, j: (l, 0, j)),
            pl.BlockSpec((1, 1, blk), lambda l, j: (l, 0, j)),
        ],
        out_specs=pl.BlockSpec((1, SUBLANES, blk), lambda l, j: (l, 0, j)),
        out_shape=jax.ShapeDtypeStruct((DEPTH, SUBLANES, n_col), F32),
        compiler_params=_cparams(2),
        name="ada_params",
    )(cond8, w_ada, b_ada.reshape(DEPTH, 1, n_col))


def _rope(x, tab_ref, shift):
    return (x * tab_ref[0]
            + pltpu.roll(x, LANES - shift, 1) * tab_ref[1]
            + pltpu.roll(x, shift, 1) * tab_ref[2])


def _rms(h, g):
    return h * lax.rsqrt(jnp.mean(h * h, axis=-1, keepdims=True) + RMS_EPS) * g


def _in_kernel(x_ref, mod_ref, wp_ref, qn_ref, kvn_ref, wuq_ref, wk_ref, wv_ref, tg_ref, tq_ref, tk_ref,
               hp_ref, qm_ref, ckv_ref, kr_ref, km_ref, vm_ref, qg_ref, kg_ref, vg_ref, nk_ref, nv_ref, sg_ref):
    x = x_ref[...]
    sh = mod_ref[0, :, 0:D_MODEL]
    sc = mod_ref[0, :, D_MODEL:2 * D_MODEL]
    u = (x * (1.0 + sc) + sh).astype(BF16)

    def proj(lo, hi):
        return _dot(u, wp_ref[:, lo:hi])

    hp_ref[...] = proj(C_HP, C_CQ)

    qn = _rms(proj(C_CQ, C_CKV), qn_ref[...]).astype(BF16)
    q = _dot(qn, wuq_ref[...])
    for h in range(MLA_HEADS):
        qm_ref[:, h * LANES:(h + 1) * LANES] = _rope(q[:, h * LANES:(h + 1) * LANES], tq_ref, MLA_ROPE // 4).astype(BF16)

    ckv = _rms(proj(C_CKV, C_KR), kvn_ref[...])
    ckv_ref[...] = ckv
    kr = _rope(proj(C_KR, C_GQ), tk_ref, MLA_ROPE // 4)
    kr_ref[...] = kr
    ckv_b = ckv.astype(BF16)
    km_ref[...] = (_dot(ckv_b, wk_ref[0:MLA_KV_LORA, :]) + _dot(kr.astype(BF16), wk_ref[MLA_KV_LORA:, :])).astype(BF16)
    vm_ref[...] = _dot(ckv_b, wv_ref[...]).astype(BF16)

    hq = proj(C_GQ, C_GK)
    for h in range(GQA_HEADS):
        qg_ref[:, h * LANES:(h + 1) * LANES] = _rope(hq[:, h * LANES:(h + 1) * LANES], tg_ref, GQA_HD // 4).astype(BF16)
    hk = proj(C_GK, C_GV)
    for g in range(GQA_KV):
        kg_ref[:, g * LANES:(g + 1) * LANES] = _rope(hk[:, g * LANES:(g + 1) * LANES], tg_ref, GQA_HD // 4).astype(BF16)
    vg_ref[...] = proj(C_GV, C_NK).astype(BF16)
    nk_ref[...] = proj(C_NK, C_NV)
    nv_ref[...] = proj(C_NV, C_G)

    for j in range(3):
        sg_ref[:, j * D_MODEL:(j + 1) * D_MODEL] = _sigmoid(proj(C_G + j * D_MODEL, C_G + (j + 1) * D_MODEL)).astype(BF16)


def _in_proj(x, mod, wp, qn, kvn, wuq, wk, wv, tab_g, tab_q, tab_k):
    tok = lambda w: pl.BlockSpec((TM, w), lambda i: (i, 0))
    tab = pl.BlockSpec((3, TM, LANES), lambda i: (0, _pos_block(i), 0))
    out_widths = [(POOL_WIDTH, F32), (MLA_HEADS * LANES, BF16), (MLA_KV_LORA, F32), (LANES, F32),
                  (MLA_HEADS * LANES, BF16), (MLA_HEADS * MLA_V, BF16), (GQA_HEADS * LANES, BF16),
                  (GQA_KV * LANES, BF16), (GQA_KV * LANES, BF16), (GQA_KV * GQA_HD, F32), (GQA_KV * GQA_HD, F32),
                  (3 * D_MODEL, BF16)]
    return pl.pallas_call(
        _in_kernel,
        grid=(N_TILES,),
        in_specs=[
            tok(D_MODEL),
            pl.BlockSpec((1, 1, 6 * D_MODEL), lambda i: (_mod_row(i), 0, 0)),
            _const_spec((D_MODEL, C_END)),
            _const_spec((1, MLA_Q_LORA)),
            _const_spec((1, MLA_KV_LORA)),
            _const_spec((MLA_Q_LORA, MLA_HEADS * LANES)),
            _const_spec((MLA_KV_LORA + LANES, MLA_HEADS * LANES)),
            _const_spec((MLA_KV_LORA, MLA_HEADS * MLA_V)),
            tab, tab, tab,
        ],
        out_specs=[tok(w) for w, _ in out_widths],
        out_shape=[jax.ShapeDtypeStruct((T_ALL, w), dt) for w, dt in out_widths],
        compiler_params=_cparams(1),
        name="in_proj",
    )(x, mod, wp, qn, kvn, wuq, wk, wv, tab_g, tab_q, tab_k)


def _expand_kernel(ckv_ref, kr_ref, wk_ref, wv_ref, km_ref, vm_ref):
    ckv_b = ckv_ref[...].astype(BF16)
    km_ref[...] = (_dot(ckv_b, wk_ref[0:MLA_KV_LORA, :]) + _dot(kr_ref[...].astype(BF16), wk_ref[MLA_KV_LORA:, :])).astype(BF16)
    vm_ref[...] = _dot(ckv_b, wv_ref[...]).astype(BF16)


def _expand_cache(ckv, kr_pad, wk, wv):
    rows = ckv.shape[0]
    blk = PAST_LEN
    return pl.pallas_call(
        _expand_kernel,
        grid=(rows // blk,),
        in_specs=[
            pl.BlockSpec((blk, MLA_KV_LORA), lambda i: (i, 0)),
            pl.BlockSpec((blk, LANES), lambda i: (i, 0)),
            _const_spec((MLA_KV_LORA + LANES, MLA_HEADS * LANES)),
            _const_spec((MLA_KV_LORA, MLA_HEADS * MLA_V)),
        ],
        out_specs=[pl.BlockSpec((blk, MLA_HEADS * LANES), lambda i: (i, 0)),
                   pl.BlockSpec((blk, MLA_HEADS * MLA_V), lambda i: (i, 0))],
        out_shape=[jax.ShapeDtypeStruct((rows, MLA_HEADS * LANES), BF16),
                   jax.ShapeDtypeStruct((rows, MLA_HEADS * MLA_V), BF16)],
        compiler_params=_cparams(1),
        name="mla_expand_cache",
    )(ckv, kr_pad, wk, wv)


def _mla_kernel(n_sets, q_ref, *refs):
    k_refs = refs[:n_sets]
    v_refs = refs[n_sets:2 * n_sets]
    o_ref = refs[2 * n_sets]
    rows = q_ref.shape[0]
    low_half = lax.broadcasted_iota(I32, (rows, LANES), 1) < MLA_V
    for j in range(MLA_HEADS // 2):
        outs = []
        for h in (2 * j, 2 * j + 1):
            qh = q_ref[:, h * LANES:(h + 1) * LANES]
            ss = [_dot_t(qh, k[:, h * LANES:(h + 1) * LANES]) * MLA_SCALE for k in k_refs]
            m = functools.reduce(jnp.maximum, [jnp.max(s, axis=-1, keepdims=True) for s in ss])
            ps = [jnp.exp(s - m) for s in ss]
            den = functools.reduce(jnp.add, [jnp.sum(p, axis=-1, keepdims=True) for p in ps])
            o = functools.reduce(jnp.add, [_dot(p.astype(BF16), v[:, j * LANES:(j + 1) * LANES]) for p, v in zip(ps, v_refs)])
            outs.append(o / den)
        o_ref[:, j * LANES:(j + 1) * LANES] = jnp.where(low_half, outs[0], outs[1]).astype(BF16)


def _mla_ctx(qm, km, vm):
    blk = S_CTX
    return pl.pallas_call(
        functools.partial(_mla_kernel, 1),
        grid=(B_CTX,),
        in_specs=[pl.BlockSpec((blk, MLA_HEADS * LANES), lambda b: (b, 0)),
                  pl.BlockSpec((blk, MLA_HEADS * LANES), lambda b: (b, 0)),
                  pl.BlockSpec((blk, MLA_HEADS * MLA_V), lambda b: (b, 0))],
        out_specs=pl.BlockSpec((blk, MLA_HEADS * MLA_V), lambda b: (b, 0)),
        out_shape=jax.ShapeDtypeStruct((T_CTX, MLA_HEADS * MLA_V), BF16),
        compiler_params=_cparams(1),
        name="mla_attn_ctx",
    )(qm, km, vm)


def _mla_lat(qm, km, vm, km_c, vm_c):
    lat0 = T_CTX // S_LAT
    return pl.pallas_call(
        functools.partial(_mla_kernel, 2),
        grid=(B_LAT, LAT_TILES_PER_SEQ),
        in_specs=[pl.BlockSpec((TM, MLA_HEADS * LANES), lambda b, t: (CTX_TILES + b * LAT_TILES_PER_SEQ + t, 0)),
                  pl.BlockSpec((PAST_LEN, MLA_HEADS * LANES), lambda b, t: (b, 0)),
                  pl.BlockSpec((S_LAT, MLA_HEADS * LANES), lambda b, t: (lat0 + b, 0)),
                  pl.BlockSpec((PAST_LEN, MLA_HEADS * MLA_V), lambda b, t: (b, 0)),
                  pl.BlockSpec((S_LAT, MLA_HEADS * MLA_V), lambda b, t: (lat0 + b, 0))],
        out_specs=pl.BlockSpec((TM, MLA_HEADS * MLA_V), lambda b, t: (b * LAT_TILES_PER_SEQ + t, 0)),
        out_shape=jax.ShapeDtypeStruct((T_LAT, MLA_HEADS * MLA_V), BF16),
        compiler_params=_cparams(2),
        name="mla_attn_lat",
    )(qm, km_c, km, vm_c, vm)


def _gqa_kernel(band, sink_ref, q_ref, *refs):
    n_sets = 4 if band else 1
    k_refs = refs[:n_sets]
    v_refs = refs[n_sets:2 * n_sets]
    o_ref = refs[2 * n_sets]
    rows = q_ref.shape[0]
    low_half = lax.broadcasted_iota(I32, (rows, LANES), 1) < GQA_HD
    masks = [None] * n_sets
    if band:
        n = pl.program_id(1)
        n_blk = pl.num_programs(1)
        qi = lax.broadcasted_iota(I32, (GQA_GROUP * rows, WBLK), 0) % rows
        kj = lax.broadcasted_iota(I32, (GQA_GROUP * rows, WBLK), 1)
        masks[0] = kj >= qi + jnp.where(n > 0, 0, WBLK)
        masks[2] = kj <= qi - jnp.where(n < n_blk - 1, 0, WBLK)
    for g in range(GQA_KV):
        qs = jnp.concatenate([q_ref[:, (GQA_GROUP * g + i) * LANES:(GQA_GROUP * g + i + 1) * LANES]
                              for i in range(GQA_GROUP)], axis=0)
        sink = jnp.concatenate([jnp.full((rows, 1), sink_ref[GQA_GROUP * g + i], F32) for i in range(GQA_GROUP)], axis=0)
        ss = []
        for k, msk in zip(k_refs, masks):
            s = _dot_t(qs, k[:, g * LANES:(g + 1) * LANES]) * GQA_SCALE
            ss.append(s if msk is None else jnp.where(msk, s, NEG_INF))
        m = functools.reduce(jnp.maximum, [jnp.max(s, axis=-1, keepdims=True) for s in ss] + [sink])
        ps = [jnp.exp(s - m) for s in ss]
        den = functools.reduce(jnp.add, [jnp.sum(p, axis=-1, keepdims=True) for p in ps] + [jnp.exp(sink - m)])
        o = functools.reduce(jnp.add, [_dot(p.astype(BF16), v[:, g * LANES:(g + 1) * LANES]) for p, v in zip(ps, v_refs)])
        o = o / den
        for jj in range(GQA_GROUP // 2):
            pair = jnp.where(low_half, o[(2 * jj) * rows:(2 * jj + 1) * rows], o[(2 * jj + 1) * rows:(2 * jj + 2) * rows])
            col = (GQA_GROUP // 2 * g + jj) * LANES
            o_ref[:, col:col + LANES] = pair.astype(BF16)


def _gqa_ctx(sink, qg, kg, vg):
    blk = S_CTX
    grid_spec = pltpu.PrefetchScalarGridSpec(
        num_scalar_prefetch=1,
        grid=(B_CTX,),
        in_specs=[pl.BlockSpec((blk, GQA_HEADS * LANES), lambda b, s: (b, 0)),
                  pl.BlockSpec((blk, GQA_KV * LANES), lambda b, s: (b, 0)),
                  pl.BlockSpec((blk, GQA_KV * LANES), lambda b, s: (b, 0))],
        out_specs=pl.BlockSpec((blk, GQA_HEADS * GQA_HD), lambda b, s: (b, 0)),
    )
    return pl.pallas_call(
        functools.partial(_gqa_kernel, False),
        grid_spec=grid_spec,
        out_shape=jax.ShapeDtypeStruct((T_CTX, GQA_HEADS * GQA_HD), BF16),
        compiler_params=_cparams(1),
        name="gqa_attn_ctx",
    )(sink, qg, kg, vg)


def _gqa_lat(sink, qg, kg, vg, kg_c, vg_c):
    nb = S_LAT // WBLK
    first = T_CTX // WBLK

    def blk(off):
        def index_map(b, n, s):
            return (first + b * nb + jnp.clip(n + off, 0, nb - 1), 0)
        return index_map

    kv_w = GQA_KV * LANES
    grid_spec = pltpu.PrefetchScalarGridSpec(
        num_scalar_prefetch=1,
        grid=(B_LAT, nb),
        in_specs=[pl.BlockSpec((WBLK, GQA_HEADS * LANES), blk(0)),
                  pl.BlockSpec((WBLK, kv_w), blk(-1)), pl.BlockSpec((WBLK, kv_w), blk(0)), pl.BlockSpec((WBLK, kv_w), blk(1)),
                  pl.BlockSpec((PAST_LEN, kv_w), lambda b, n, s: (b, 0)),
                  pl.BlockSpec((WBLK, kv_w), blk(-1)), pl.BlockSpec((WBLK, kv_w), blk(0)), pl.BlockSpec((WBLK, kv_w), blk(1)),
                  pl.BlockSpec((PAST_LEN, kv_w), lambda b, n, s: (b, 0))],
        out_specs=pl.BlockSpec((WBLK, GQA_HEADS * GQA_HD), lambda b, n, s: (b * nb + n, 0)),
    )
    return pl.pallas_call(
        functools.partial(_gqa_kernel, True),
        grid_spec=grid_spec,
        out_shape=jax.ShapeDtypeStruct((T_LAT, GQA_HEADS * GQA_HD), BF16),
        compiler_params=_cparams(2),
        name="gqa_attn_lat",
    )(sink, qg, kg, kg, kg, kg_c, vg, vg, vg, vg_c)


def _pool_kernel(hp_p, hp_c, hp_n, pw_ref, ps_ref, o_ref):
    i = pl.program_id(0)
    is_ctx = i < CTX_TILES
    t4 = (i - CTX_TILES) % LAT_TILES_PER_SEQ
    seq_len = jnp.where(is_ctx, S_CTX, S_LAT)
    base = jnp.where(is_ctx, 0, t4 * TM)

    n_keys = TM + 2 * HALO
    first_key = jnp.where(jnp.logical_or(is_ctx, t4 == 0), HALO, 0)
    end_key = jnp.where(jnp.logical_or(is_ctx, t4 == LAT_TILES_PER_SEQ - 1), HALO + TM, n_keys)
    qi = lax.broadcasted_iota(I32, (TM, n_keys), 0)
    kj = lax.broadcasted_iota(I32, (TM, n_keys), 1)
    rel = kj - HALO - qi
    key_ok = (kj >= first_key) & (kj < end_key)

    cur = hp_c[...]
    keys = jnp.concatenate([hp_p[TM - HALO:TM, :], cur, hp_n[0:HALO, :]], axis=0)
    k_hi, k_lo = _split(keys)
    qpos = base + lax.broadcasted_iota(I32, (TM, 1), 0)
    for g, w in enumerate(POOL_WINDOWS):
        a = jnp.where((rel >= -(w // 2)) & (rel <= w // 2 - 1) & key_ok, 1.0, 0.0).astype(BF16)
        cols = slice(g * POOL_GROUP, (g + 1) * POOL_GROUP)
        s = _dot(a, k_hi[:, cols]) + _dot(a, k_lo[:, cols])
        lo = jnp.maximum(qpos - w // 2, 0)
        hi = jnp.minimum(qpos + w // 2 - 1, seq_len - 1)
        cnt = (hi - lo + 1).astype(F32)
        d = s / cnt - cur[:, cols]
        y = _dot(d.astype(BF16), pw_ref[g]) * ps_ref[:, cols]
        o_ref[:, cols] = y.astype(BF16)


def _pool(hp, pw, ps):
    return pl.pallas_call(
        _pool_kernel,
        grid=(N_TILES,),
        in_specs=[pl.BlockSpec((TM, POOL_WIDTH), lambda i: (jnp.maximum(i - 1, 0), 0)),
                  pl.BlockSpec((TM, POOL_WIDTH), lambda i: (i, 0)),
                  pl.BlockSpec((TM, POOL_WIDTH), lambda i: (jnp.minimum(i + 1, N_TILES - 1), 0)),
                  _const_spec((len(POOL_WINDOWS), POOL_GROUP, POOL_GROUP)),
                  _const_spec((1, POOL_WIDTH))],
        out_specs=pl.BlockSpec((TM, POOL_WIDTH), lambda i: (i, 0)),
        out_shape=jax.ShapeDtypeStruct((T_ALL, POOL_WIDTH), BF16),
        compiler_params=_cparams(1),
        name="pool_mix",
    )(hp, hp, hp, pw, ps)


def _merge_kernel(x_ref, mod_ref, yp_ref, ymc_ref, yml_ref, ygc_ref, ygl_ref, sg_ref,
                  wbp_ref, wbm_ref, wbg_ref, wo_ref, lg_ref, lb_ref, rw_ref, rb_ref,
                  x1_ref, u2_ref, ri_ref, rwt_ref, cnt_ref, carry):
    i = pl.program_id(0)

    @pl.when(i == 0)
    def _():
        carry[...] = jnp.zeros_like(carry)

    is_ctx = i < CTX_TILES
    ym = jnp.where(is_ctx, ymc_ref[...], yml_ref[...])
    yg = jnp.where(is_ctx, ygc_ref[...], ygl_ref[...])
    m = (sg_ref[:, 0:D_MODEL].astype(F32) * _dot(yp_ref[...], wbp_ref[...])
         + sg_ref[:, D_MODEL:2 * D_MODEL].astype(F32) * _dot(ym, wbm_ref[...])
         + sg_ref[:, 2 * D_MODEL:3 * D_MODEL].astype(F32) * _dot(yg, wbg_ref[...]))
    y = _dot(m.astype(BF16), wo_ref[...])
    gate1 = mod_ref[0, :, 2 * D_MODEL:3 * D_MODEL]
    x1 = _layer_norm(ALPHA_DN * x_ref[...] + gate1 * y, lg_ref[...], lb_ref[...])
    x1_ref[...] = x1
    sh2 = mod_ref[0, :, 3 * D_MODEL:4 * D_MODEL]
    sc2 = mod_ref[0, :, 4 * D_MODEL:5 * D_MODEL]
    u2 = x1 * (1.0 + sc2) + sh2
    u2_ref[...] = u2

    logits = _dot3(u2, rw_ref[...]) + rb_ref[...]
    lane = lax.broadcasted_iota(I32, (TM, LANES), 1)
    lane_f = lane.astype(F32)
    vals, idxs = [], []
    rest = logits
    for _ in range(TOP_K):
        mx = jnp.max(rest, axis=-1, keepdims=True)
        ix = jnp.min(jnp.where(rest == mx, lane_f, float(LANES)), axis=-1, keepdims=True).astype(I32)
        vals.append(mx)
        idxs.append(ix)
        rest = jnp.where(lane == ix, -jnp.inf, rest)
    es = [jnp.exp(v - vals[0]) for v in vals]
    den = functools.reduce(jnp.add, es)

    sel = functools.reduce(jnp.add, [jnp.where(lane == ix, 1.0, 0.0) for ix in idxs])
    r_i = lax.broadcasted_iota(I32, (TM, TM), 0)
    c_i = lax.broadcasted_iota(I32, (TM, TM), 1)
    below = jnp.where(c_i < r_i, 1.0, 0.0).astype(BF16)
    rank = _dot(below, sel.astype(BF16)) + carry[...]
    carry[...] = carry[...] + jnp.sum(sel, axis=0, keepdims=True)
    cnt_ref[...] = carry[...]

    ri = jnp.zeros((TM, LANES), I32)
    rwt = jnp.zeros((TM, LANES), F32)
    for k in range(TOP_K):
        rk = jnp.sum(jnp.where(lane == idxs[k], rank, 0.0), axis=-1, keepdims=True).astype(I32)
        ri = jnp.where(lane == k, idxs[k], ri)
        ri = jnp.where(lane == TOP_K + k, rk, ri)
        rwt = jnp.where(lane == k, es[k] / den, rwt)
    ri_ref[...] = ri
    rwt_ref[...] = rwt


def _merge(x, mod, yp, ymc, yml, ygc, ygl, sg, wbp, wbm, wbg, wo, lg, lb, rw, rb):
    tok = lambda w: pl.BlockSpec((TM, w), lambda i: (i, 0))
    ctx = lambda w: pl.BlockSpec((TM, w), lambda i: (jnp.minimum(i, CTX_TILES - 1), 0))
    lat = lambda w: pl.BlockSpec((TM, w), lambda i: (jnp.maximum(i - CTX_TILES, 0), 0))
    return pl.pallas_call(
        _merge_kernel,
        grid=(N_TILES,),
        in_specs=[tok(D_MODEL),
                  pl.BlockSpec((1, 1, 6 * D_MODEL), lambda i: (_mod_row(i), 0, 0)),
                  tok(POOL_WIDTH), ctx(512), lat(512), ctx(512), lat(512), tok(3 * D_MODEL),
                  _const_spec((POOL_WIDTH, D_MODEL)), _const_spec((512, D_MODEL)), _const_spec((512, D_MODEL)),
                  _const_spec((D_MODEL, D_MODEL)), _const_spec((1, D_MODEL)), _const_spec((1, D_MODEL)),
                  _const_spec((D_MODEL, LANES)), _const_spec((1, LANES))],
        out_specs=[tok(D_MODEL), tok(D_MODEL), tok(LANES), tok(LANES), _const_spec((1, LANES))],
        out_shape=[jax.ShapeDtypeStruct((T_ALL, D_MODEL), F32), jax.ShapeDtypeStruct((T_ALL, D_MODEL), F32),
                   jax.ShapeDtypeStruct((T_ALL, LANES), I32), jax.ShapeDtypeStruct((T_ALL, LANES), F32),
                   jax.ShapeDtypeStruct((1, LANES), F32)],
        scratch_shapes=[pltpu.VMEM((1, LANES), F32)],
        compiler_params=_cparams(1),
        name="merge_route",
    )(x, mod, yp, ymc, yml, ygc, ygl, sg, wbp, wbm, wbg, wo, lg, lb, rw, rb)


def _row_copy(src_ref, src_row, dst_ref, dst_row, sem):
    return pltpu.make_async_copy(src_ref.at[pl.ds(src_row, 1), :], dst_ref.at[pl.ds(dst_row, 1), :], sem)


ROWS_PER_ISSUE = 4
WAITS_PER_TRIP = 64


def _drain_rows(wait_one, n_rows):
    def trip(_, c):
        for _ in range(WAITS_PER_TRIP):
            wait_one()
        return c

    lax.fori_loop(0, n_rows // WAITS_PER_TRIP, trip, 0)


def _dispatch_kernel(dest_ref, u_ref, xs_in_ref, xs_ref, sem):
    del xs_in_ref
    base = pl.program_id(0) * (TM * TOP_K)

    def issue(g, c):
        for r in range(ROWS_PER_ISSUE):
            t = g * ROWS_PER_ISSUE + r
            for k in range(TOP_K):
                _row_copy(u_ref, t, xs_ref, dest_ref[base + t * TOP_K + k], sem).start()
        return c

    lax.fori_loop(0, TM // ROWS_PER_ISSUE, issue, 0)
    _drain_rows(lambda: _row_copy(u_ref, 0, xs_ref, 0, sem).wait(), TM * TOP_K)


def _dispatch(dest, u2, xs_zero):
    grid_spec = pltpu.PrefetchScalarGridSpec(
        num_scalar_prefetch=1,
        grid=(N_TILES,),
        in_specs=[pl.BlockSpec((TM, D_MODEL), lambda i, d: (i, 0)),
                  pl.BlockSpec(memory_space=pl.ANY)],
        out_specs=pl.BlockSpec(memory_space=pl.ANY),
        scratch_shapes=[pltpu.SemaphoreType.DMA(())],
    )
    return pl.pallas_call(
        _dispatch_kernel,
        grid_spec=grid_spec,
        out_shape=jax.ShapeDtypeStruct((P_ROWS, D_MODEL), F32),
        input_output_aliases={2: 0},
        compiler_params=_cparams(1),
        name="moe_dispatch",
    )(dest, u2, xs_zero)


def _expert_kernel(te_ref, nu_ref, xs_ref, wgu_ref, bgu_ref, wd_ref, bd_ref, ys_ref, wgu_s, wd_s):
    i = pl.program_id(0)
    e = te_ref[i]
    prev = te_ref[jnp.maximum(i - 1, 0)]

    @pl.when(jnp.logical_or(i == 0, e != prev))
    def _():
        wgu_s[...] = wgu_ref[0, 0].astype(BF16)
        wd_s[...] = wd_ref[0, 0].astype(BF16)

    @pl.when(i < nu_ref[0])
    def _():
        h = _dot(xs_ref[...].astype(BF16), wgu_s[...]) + bgu_ref[0, 0]
        glu = jnp.minimum(h[:, 0:D_FF], SWIGLU_LIMIT)
        lin = jnp.clip(h[:, D_FF:2 * D_FF], -SWIGLU_LIMIT, SWIGLU_LIMIT)
        a = glu * _sigmoid(SWIGLU_ALPHA * glu) * (lin + 1.0)
        ys_ref[...] = _dot(a.astype(BF16), wd_s[...]) + bd_ref[0, 0]

    @pl.when(i >= nu_ref[0])
    def _():
        ys_ref[...] = jnp.zeros_like(ys_ref)


def _experts(layer, tile_expert, n_used, xs, w_gate_up, b_gate_up, w_down, b_down):
    row = lambda i, te, nu: (jnp.minimum(i, nu[0] - 1), 0)
    grid_spec = pltpu.PrefetchScalarGridSpec(
        num_scalar_prefetch=2,
        grid=(NT_E,),
        in_specs=[pl.BlockSpec((TME, D_MODEL), row),
                  pl.BlockSpec((1, 1, D_MODEL, 2 * D_FF), lambda i, te, nu: (layer, te[i], 0, 0)),
                  pl.BlockSpec((1, 1, 1, 2 * D_FF), lambda i, te, nu: (layer, te[i], 0, 0)),
                  pl.BlockSpec((1, 1, D_FF, D_MODEL), lambda i, te, nu: (layer, te[i], 0, 0)),
                  pl.BlockSpec((1, 1, 1, D_MODEL), lambda i, te, nu: (layer, te[i], 0, 0))],
        out_specs=pl.BlockSpec((TME, D_MODEL), lambda i, te, nu: (i, 0)),
        scratch_shapes=[pltpu.VMEM((D_MODEL, 2 * D_FF), BF16), pltpu.VMEM((D_FF, D_MODEL), BF16)],
    )
    return pl.pallas_call(
        _expert_kernel,
        grid_spec=grid_spec,
        out_shape=jax.ShapeDtypeStruct((P_ROWS, D_MODEL), F32),
        compiler_params=_cparams(1),
        name="moe_experts",
    )(tile_expert, n_used, xs, w_gate_up,
      b_gate_up.reshape(DEPTH, N_EXPERTS, 1, 2 * D_FF), w_down, b_down.reshape(DEPTH, N_EXPERTS, 1, D_MODEL))


def _combine_kernel(dest_ref, ys_ref, rwt_ref, x1_ref, mod_ref, lg_ref, lb_ref, o_ref, buf, sem):
    base = pl.program_id(0) * (TM * TOP_K)

    def issue(g, c):
        for r in range(ROWS_PER_ISSUE):
            t = g * ROWS_PER_ISSUE + r
            for k in range(TOP_K):
                _row_copy(ys_ref, dest_ref[base + t * TOP_K + k], buf.at[k], t, sem).start()
        return c

    lax.fori_loop(0, TM // ROWS_PER_ISSUE, issue, 0)
    _drain_rows(lambda: _row_copy(ys_ref, 0, buf.at[0], 0, sem).wait(), TM * TOP_K)

    moe = functools.reduce(jnp.add, [rwt_ref[:, k:k + 1] * buf[k] for k in range(TOP_K)])
    gate2 = mod_ref[0, :, 5 * D_MODEL:6 * D_MODEL]
    o_ref[...] = _layer_norm(ALPHA_DN * x1_ref[...] + gate2 * moe, lg_ref[...], lb_ref[...])


def _combine(dest, ys, rwt, x1, mod, lg, lb):
    grid_spec = pltpu.PrefetchScalarGridSpec(
        num_scalar_prefetch=1,
        grid=(N_TILES,),
        in_specs=[pl.BlockSpec(memory_space=pl.ANY),
                  pl.BlockSpec((TM, LANES), lambda i, d: (i, 0)),
                  pl.BlockSpec((TM, D_MODEL), lambda i, d: (i, 0)),
                  pl.BlockSpec((1, 1, 6 * D_MODEL), lambda i, d: (_mod_row(i), 0, 0)),
                  pl.BlockSpec((1, D_MODEL), lambda i, d: (0, 0)),
                  pl.BlockSpec((1, D_MODEL), lambda i, d: (0, 0))],
        out_specs=pl.BlockSpec((TM, D_MODEL), lambda i, d: (i, 0)),
        scratch_shapes=[pltpu.VMEM((TOP_K, TM, D_MODEL), F32), pltpu.SemaphoreType.DMA(())],
    )
    return pl.pallas_call(
        _combine_kernel,
        grid_spec=grid_spec,
        out_shape=jax.ShapeDtypeStruct((T_ALL, D_MODEL), F32),
        compiler_params=_cparams(1),
        name="moe_combine",
    )(dest, ys, rwt, x1, mod, lg, lb)


def _pad_heads(w, n_heads, width):
    rows = w.shape[0]
    w = w.reshape(rows, n_heads, width)
    return jnp.pad(w, ((0, 0), (0, 0), (0, LANES - width))).reshape(rows, n_heads * LANES)


def _pack_w_in(w):
    cuts = [0, 512, 896, 1152, 1184, 1696, 1824, 1952, 5024]
    hp, cq, ckv, kr, q, k, v, g = [w[:, a:b] for a, b in zip(cuts[:-1], cuts[1:])]
    v_dup = jnp.concatenate([v.reshape(D_MODEL, GQA_KV, 1, GQA_HD)] * 2, axis=2).reshape(D_MODEL, GQA_KV * LANES)
    packed = jnp.concatenate([hp, cq, ckv, _pad_heads(kr, 1, MLA_ROPE), _pad_heads(q, GQA_HEADS, GQA_HD),
                              _pad_heads(k, GQA_KV, GQA_HD), v_dup, k, v, g], axis=1)
    return packed.astype(BF16)


def _pack_mla(wuq, wukv):
    wuq_p = _pad_heads(wuq, MLA_HEADS, MLA_QK).astype(BF16)
    kv = wukv.reshape(MLA_KV_LORA, MLA_HEADS, MLA_NOPE + MLA_V)
    wk_top = _pad_heads(kv[:, :, :MLA_NOPE].reshape(MLA_KV_LORA, MLA_HEADS * MLA_NOPE), MLA_HEADS, MLA_NOPE)
    place = jnp.zeros((LANES, MLA_HEADS, LANES), F32)
    r = jnp.arange(MLA_ROPE)
    place = place.at[r, :, MLA_NOPE + r].set(1.0).reshape(LANES, MLA_HEADS * LANES)
    wk = jnp.concatenate([wk_top, place], axis=0).astype(BF16)
    wv = kv[:, :, MLA_NOPE:].reshape(MLA_KV_LORA, MLA_HEADS * MLA_V).astype(BF16)
    return wuq_p, wk, wv


def _rope_table(dim, lane0):
    quarter = dim // 4
    pos = jnp.arange(S_LAT)
    rows = (pos // GRID_W).astype(F32)
    cols = (pos % GRID_W).astype(F32)
    freqs = jnp.power(ROPE_THETA, -jnp.arange(quarter, dtype=F32) / quarter)
    ang_r = rows[:, None] * freqs[None, :]
    ang_c = cols[:, None] * freqs[None, :]
    zero = jnp.zeros((S_LAT, quarter), F32)
    cos = jnp.concatenate([jnp.cos(ang_r), jnp.cos(ang_r), jnp.cos(ang_c), jnp.cos(ang_c)], axis=1)
    s_up = jnp.concatenate([-jnp.sin(ang_r), zero, -jnp.sin(ang_c), zero], axis=1)
    s_dn = jnp.concatenate([zero, jnp.sin(ang_r), zero, jnp.sin(ang_c)], axis=1)

    def place(t, fill):
        full = jnp.full((S_LAT, LANES), fill, F32)
        return full.at[:, lane0:lane0 + dim].set(t)

    planes = jnp.stack([place(cos, 1.0), place(s_up, 0.0), place(s_dn, 0.0)])
    ident = jnp.stack([jnp.ones((TM, LANES), F32), jnp.zeros((TM, LANES), F32), jnp.zeros((TM, LANES), F32)])
    return jnp.concatenate([planes, ident], axis=1)


def _routing_plan(ri, counts):
    idx = ri[:, 0:TOP_K]
    rank = ri[:, TOP_K:2 * TOP_K]
    cnt = counts[0, :N_EXPERTS].astype(I32)
    padded = (cnt + TME - 1) // TME * TME
    ends = jnp.cumsum(padded)
    starts = ends - padded
    dest = (starts[idx] + rank).reshape(-1).astype(I32)
    n_used = (ends[-1] // TME).astype(I32)
    tile_row = jnp.arange(NT_E, dtype=I32) * TME
    tile_expert = jnp.sum((tile_row[:, None] >= ends[None, :]).astype(I32), axis=1)
    last = jnp.sum((jnp.maximum(ends[-1] - TME, 0) >= ends).astype(I32))
    tile_expert = jnp.minimum(tile_expert, last).astype(I32)
    return dest, tile_expert, n_used.reshape(1)


def kernel(x_prompt, x_sample, c, cache_mla_ckv, cache_mla_krope, cache_gqa_k, cache_gqa_v, c_ctx, w_ada, b_ada, w_in, mla_q_norm, mla_kv_norm, w_mla_uq, w_mla_ukv, gqa_sink, pool_w, pool_scale, w_branch_pool, w_branch_mla, w_branch_gqa, w_out, ln1_g, ln1_b, ln2_g, ln2_b, router_w, router_b, w_gate_up, b_gate_up, w_down, b_down):
    x = jnp.concatenate([x_prompt.reshape(T_CTX, D_MODEL), x_sample.reshape(T_LAT, D_MODEL)], axis=0)
    cond8 = jnp.concatenate([c_ctx[None, :], c, jnp.zeros((SUBLANES - 1 - B_LAT, D_MODEL), F32)], axis=0)
    mod_all = _ada_params(cond8, w_ada, b_ada)

    tab_g = _rope_table(GQA_HD, 0)
    tab_q = _rope_table(MLA_ROPE, MLA_NOPE)
    tab_k = _rope_table(MLA_ROPE, 0)

    outs = {"ckv": [], "kr": [], "gk": [], "gv": []}
    for l in range(DEPTH):
        mod = mod_all[l].reshape(SUBLANES, 1, 6 * D_MODEL)
        wp = _pack_w_in(w_in[l])
        wuq_p, wk, wv = _pack_mla(w_mla_uq[l], w_mla_ukv[l])

        (hp, qm, ckv, kr, km, vm, qg, kg, vg, nk, nv, sg) = _in_proj(
            x, mod, wp, mla_q_norm[l][None, :], mla_kv_norm[l][None, :], wuq_p, wk, wv, tab_g, tab_q, tab_k)
        outs["ckv"].append(ckv[:T_CTX].reshape(B_CTX, S_CTX, MLA_KV_LORA))
        outs["kr"].append(kr[:T_CTX, :MLA_ROPE].reshape(B_CTX, S_CTX, MLA_ROPE))
        outs["gk"].append(nk[:T_CTX].reshape(B_CTX, S_CTX, GQA_KV, GQA_HD))
        outs["gv"].append(nv[:T_CTX].reshape(B_CTX, S_CTX, GQA_KV, GQA_HD))

        ckv_c = cache_mla_ckv[:, l].reshape(B_LAT * PAST_LEN, MLA_KV_LORA)
        kr_c = jnp.pad(cache_mla_krope[:, l].reshape(B_LAT * PAST_LEN, MLA_ROPE), ((0, 0), (0, LANES - MLA_ROPE)))
        km_c, vm_c = _expand_cache(ckv_c, kr_c, wk, wv)
        gk_c = _pad_heads(cache_gqa_k[:, l].reshape(B_LAT * PAST_LEN, GQA_KV * GQA_HD), GQA_KV, GQA_HD).astype(BF16)
        gv_c = cache_gqa_v[:, l].reshape(B_LAT * PAST_LEN, GQA_KV, 1, GQA_HD)
        gv_c = jnp.concatenate([gv_c, gv_c], axis=2).reshape(B_LAT * PAST_LEN, GQA_KV * LANES).astype(BF16)

        yp = _pool(hp, pool_w[l].astype(BF16), pool_scale[l][None, :])
        ymc = _mla_ctx(qm, km, vm)
        yml = _mla_lat(qm, km, vm, km_c, vm_c)
        ygc = _gqa_ctx(gqa_sink[l], qg, kg, vg)
        ygl = _gqa_lat(gqa_sink[l], qg, kg, vg, gk_c, gv_c)

        rw = jnp.pad(router_w[l], ((0, 0), (0, LANES - N_EXPERTS)))
        rb = jnp.concatenate([router_b[l], jnp.full((LANES - N_EXPERTS,), -jnp.inf, F32)])[None, :]
        x1, u2, ri, rwt, counts = _merge(
            x, mod, yp, ymc, yml, ygc, ygl, sg,
            w_branch_pool[l].astype(BF16), w_branch_mla[l].astype(BF16), w_branch_gqa[l].astype(BF16),
            w_out[l].astype(BF16), ln1_g[l][None, :], ln1_b[l][None, :], rw, rb)

        dest, tile_expert, n_used = _routing_plan(ri, counts)
        xs = _dispatch(dest, u2, jnp.zeros((P_ROWS, D_MODEL), F32))
        ys = _experts(l, tile_expert, n_used, xs, w_gate_up, b_gate_up, w_down, b_down)
        x = _combine(dest, ys, rwt, x1, mod, ln2_g[l][None, :], ln2_b[l][None, :])

    y_prompt = x[:T_CTX].reshape(B_CTX, S_CTX, D_MODEL)
    y_sample = x[T_CTX:].reshape(B_LAT, S_LAT, D_MODEL)
    return (y_prompt, y_sample, jnp.stack(outs["ckv"], axis=1), jnp.stack(outs["kr"], axis=1),
            jnp.stack(outs["gk"], axis=1), jnp.stack(outs["gv"], axis=1))
```

```python
import functools
import math

import jax
import jax.numpy as jnp
from jax import lax
from jax.experimental import pallas as pl
from jax.experimental.pallas import tpu as pltpu

F32 = jnp.float32
BF16 = jnp.bfloat16
I32 = jnp.int32

D_MODEL = 1024
DEPTH = 2
B_CTX, S_CTX = 16, 256
B_LAT, S_LAT = 4, 1024
PAST_LEN = 512
GRID_W = 64
ROPE_THETA = 10000.0
ALPHA_DN = (2 * DEPTH) ** 0.25
LN_EPS = 1e-5
RMS_EPS = 1e-6
NEG_INF = -1e30
POOL_WINDOWS = (2, 4, 8, 16)
POOL_GROUP = 128
POOL_WIDTH = 512
MLA_HEADS = 8
MLA_NOPE = 64
MLA_ROPE = 32
MLA_V = 64
MLA_QK = MLA_NOPE + MLA_ROPE
MLA_Q_LORA = 384
MLA_KV_LORA = 256
MLA_SCALE = 1.0 / math.sqrt(MLA_QK)
GQA_HEADS = 8
GQA_KV = 2
GQA_HD = 64
GQA_GROUP = GQA_HEADS // GQA_KV
GQA_SCALE = 1.0 / math.sqrt(GQA_HD)
WBLK = 128
N_EXPERTS = 32
TOP_K = 4
D_FF = 1024
SWIGLU_LIMIT = 7.0
SWIGLU_ALPHA = 1.702

LANES = 128
SUBLANES = 8
VMEM_LIMIT = 56 * 1024 * 1024

T_CTX = B_CTX * S_CTX
T_LAT = B_LAT * S_LAT
T_ALL = T_CTX + T_LAT
TM = 256
N_TILES = T_ALL // TM
CTX_TILES = T_CTX // TM
LAT_TILES_PER_SEQ = S_LAT // TM
HALO = 16

C_HP = 0
C_CQ = C_HP + POOL_WIDTH
C_CKV = C_CQ + MLA_Q_LORA
C_KR = C_CKV + MLA_KV_LORA
C_GQ = C_KR + LANES
C_GK = C_GQ + GQA_HEADS * GQA_HD
C_GV = C_GK + GQA_KV * LANES
C_G = C_GV + GQA_KV * LANES
C_END = C_G + 3 * D_MODEL

TME = 256
N_PAIRS = T_ALL * TOP_K
NT_E = N_PAIRS // TME + N_EXPERTS
P_ROWS = NT_E * TME


def _cparams(n_axes):
    return pltpu.CompilerParams(dimension_semantics=("arbitrary",) * n_axes, vmem_limit_bytes=VMEM_LIMIT)


def _dot(a, b):
    return jnp.dot(a, b, preferred_element_type=F32)


def _dot_t(a, b):
    return lax.dot_general(a, b, (((1,), (1,)), ((), ())), preferred_element_type=F32)


def _split(x):
    hi = x.astype(BF16)
    lo = (x - hi.astype(F32)).astype(BF16)
    return hi, lo


def _dot3(a, b):
    a_hi, a_lo = _split(a)
    b_hi, b_lo = _split(b)
    return _dot(a_hi, b_hi) + _dot(a_hi, b_lo) + _dot(a_lo, b_hi)


def _sigmoid(x):
    return 1.0 / (1.0 + jnp.exp(-x))


def _layer_norm(z, g, b):
    mu = jnp.mean(z, axis=-1, keepdims=True)
    zc = z - mu
    var = jnp.mean(zc * zc, axis=-1, keepdims=True)
    return zc * lax.rsqrt(var + LN_EPS) * g + b


def _mod_row(i):
    return jnp.where(i < CTX_TILES, 0, 1 + (i - CTX_TILES) // LAT_TILES_PER_SEQ)


def _pos_block(i):
    return jnp.where(i < CTX_TILES, LAT_TILES_PER_SEQ, (i - CTX_TILES) % LAT_TILES_PER_SEQ)


def _const_spec(shape):
    nd = len(shape)
    return pl.BlockSpec(shape, lambda *_: (0,) * nd)


def _ada_kernel(c_ref, w_ref, b_ref, o_ref):
    c = c_ref[...]
    s = c * _sigmoid(c)
    o_ref[0] = _dot3(s, w_ref[0]) + b_ref[0]


def _ada_params(cond8, w_ada, b_ada):
    n_col = 6 * D_MODEL
    blk = 1024
    return pl.pallas_call(
        _ada_kernel,
        grid=(DEPTH, n_col // blk),
        in_specs=[
            pl.BlockSpec((SUBLANES, D_MODEL), lambda l, j: (0, 0)),
            pl.BlockSpec((1, D_MODEL, blk), lambda l, j: (l, 0, j)),
            pl.BlockSpec((1, 1, blk), lambda l, j: (l, 0, j)),
        ],
        out_specs=pl.BlockSpec((1, SUBLANES, blk), lambda l, j: (l, 0, j)),
        out_shape=jax.ShapeDtypeStruct((DEPTH, SUBLANES, n_col), F32),
        compiler_params=_cparams(2),
        name="ada_params",
    )(cond8, w_ada, b_ada.reshape(DEPTH, 1, n_col))


def _rope(x, tab_ref, shift):
    return (x * tab_ref[0]
            + pltpu.roll(x, LANES - shift, 1) * tab_ref[1]
            + pltpu.roll(x, shift, 1) * tab_ref[2])


def _rms(h, g):
    return h * lax.rsqrt(jnp.mean(h * h, axis=-1, keepdims=True) + RMS_EPS) * g


def _in_kernel(x_ref, mod_ref, wp_ref, qn_ref, kvn_ref, wuq_ref, wk_ref, wv_ref, tg_ref, tq_ref, tk_ref,
               hp_ref, qm_ref, ckv_ref, kr_ref, km_ref, vm_ref, qg_ref, kg_ref, vg_ref, nk_ref, nv_ref, sg_ref):
    x = x_ref[...]
    sh = mod_ref[0, :, 0:D_MODEL]
    sc = mod_ref[0, :, D_MODEL:2 * D_MODEL]
    u = (x * (1.0 + sc) + sh).astype(BF16)

    def proj(lo, hi):
        return _dot(u, wp_ref[:, lo:hi])

    hp_ref[...] = proj(C_HP, C_CQ)

    qn = _rms(proj(C_CQ, C_CKV), qn_ref[...]).astype(BF16)
    q = _dot(qn, wuq_ref[...]) * MLA_SCALE
    for h in range(MLA_HEADS):
        qm_ref[:, h * LANES:(h + 1) * LANES] = _rope(q[:, h * LANES:(h + 1) * LANES], tq_ref, MLA_ROPE // 4).astype(BF16)

    ckv = _rms(proj(C_CKV, C_KR), kvn_ref[...])
    ckv_ref[...] = ckv
    kr = _rope(proj(C_KR, C_GQ), tk_ref, MLA_ROPE // 4)
    kr_ref[...] = kr
    ckv_b = ckv.astype(BF16)
    km_ref[...] = (_dot(ckv_b, wk_ref[0:MLA_KV_LORA, :]) + _dot(kr.astype(BF16), wk_ref[MLA_KV_LORA:, :])).astype(BF16)
    vm_ref[...] = _dot(ckv_b, wv_ref[...]).astype(BF16)

    hq = proj(C_GQ, C_GK) * GQA_SCALE
    for j in range(GQA_HEADS * GQA_HD // LANES):
        qg_ref[:, j * LANES:(j + 1) * LANES] = _rope(hq[:, j * LANES:(j + 1) * LANES], tg_ref, GQA_HD // 4).astype(BF16)
    hk = proj(C_GK, C_GV)
    hv = proj(C_GV, C_G)
    for g in range(GQA_KV):
        kg_ref[:, g * LANES:(g + 1) * LANES] = _rope(hk[:, g * LANES:(g + 1) * LANES], tg_ref, GQA_HD // 4).astype(BF16)
    vg_ref[...] = hv.astype(BF16)
    low_half = lax.broadcasted_iota(I32, (TM, LANES), 1) < GQA_HD
    nk_ref[...] = jnp.where(low_half, hk[:, 0:LANES], hk[:, LANES:2 * LANES])
    nv_ref[...] = jnp.where(low_half, hv[:, 0:LANES], hv[:, LANES:2 * LANES])

    for j in range(3):
        sg_ref[:, j * D_MODEL:(j + 1) * D_MODEL] = _sigmoid(proj(C_G + j * D_MODEL, C_G + (j + 1) * D_MODEL)).astype(BF16)


def _in_proj(x, mod, wp, qn, kvn, wuq, wk, wv, tab_g, tab_q, tab_k):
    tok = lambda w: pl.BlockSpec((TM, w), lambda i: (i, 0))
    tab = pl.BlockSpec((3, TM, LANES), lambda i: (0, _pos_block(i), 0))
    out_widths = [(POOL_WIDTH, F32), (MLA_HEADS * LANES, BF16), (MLA_KV_LORA, F32), (LANES, F32),
                  (MLA_HEADS * LANES, BF16), (MLA_HEADS * MLA_V, BF16), (GQA_HEADS * GQA_HD, BF16),
                  (GQA_KV * LANES, BF16), (GQA_KV * LANES, BF16), (GQA_KV * GQA_HD, F32), (GQA_KV * GQA_HD, F32),
                  (3 * D_MODEL, BF16)]
    return pl.pallas_call(
        _in_kernel,
        grid=(N_TILES,),
        in_specs=[
            tok(D_MODEL),
            pl.BlockSpec((1, 1, 6 * D_MODEL), lambda i: (_mod_row(i), 0, 0)),
            _const_spec((D_MODEL, C_END)),
            _const_spec((1, MLA_Q_LORA)),
            _const_spec((1, MLA_KV_LORA)),
            _const_spec((MLA_Q_LORA, MLA_HEADS * LANES)),
            _const_spec((MLA_KV_LORA + LANES, MLA_HEADS * LANES)),
            _const_spec((MLA_KV_LORA, MLA_HEADS * MLA_V)),
            tab, tab, tab,
        ],
        out_specs=[tok(w) for w, _ in out_widths],
        out_shape=[jax.ShapeDtypeStruct((T_ALL, w), dt) for w, dt in out_widths],
        compiler_params=_cparams(1),
        name="in_proj",
    )(x, mod, wp, qn, kvn, wuq, wk, wv, tab_g, tab_q, tab_k)


def _expand_kernel(ckv_ref, kr_ref, wk_ref, wv_ref, km_ref, vm_ref):
    ckv_b = ckv_ref[...].astype(BF16)
    km_ref[...] = (_dot(ckv_b, wk_ref[0:MLA_KV_LORA, :]) + _dot(kr_ref[...].astype(BF16), wk_ref[MLA_KV_LORA:, :])).astype(BF16)
    vm_ref[...] = _dot(ckv_b, wv_ref[...]).astype(BF16)


def _expand_cache(ckv, kr_pad, wk, wv):
    rows = ckv.shape[0]
    blk = PAST_LEN
    return pl.pallas_call(
        _expand_kernel,
        grid=(rows // blk,),
        in_specs=[
            pl.BlockSpec((blk, MLA_KV_LORA), lambda i: (i, 0)),
            pl.BlockSpec((blk, LANES), lambda i: (i, 0)),
            _const_spec((MLA_KV_LORA + LANES, MLA_HEADS * LANES)),
            _const_spec((MLA_KV_LORA, MLA_HEADS * MLA_V)),
        ],
        out_specs=[pl.BlockSpec((blk, MLA_HEADS * LANES), lambda i: (i, 0)),
                   pl.BlockSpec((blk, MLA_HEADS * MLA_V), lambda i: (i, 0))],
        out_shape=[jax.ShapeDtypeStruct((rows, MLA_HEADS * LANES), BF16),
                   jax.ShapeDtypeStruct((rows, MLA_HEADS * MLA_V), BF16)],
        compiler_params=_cparams(1),
        name="mla_expand_cache",
    )(ckv, kr_pad, wk, wv)


def _lane_chunks(x):
    return [x[:, c:c + LANES] for c in range(0, x.shape[1], LANES)]


def _softmax_sets(ss, extra=None):
    m = jnp.max(functools.reduce(jnp.maximum, [c for s in ss for c in _lane_chunks(s)]), axis=-1, keepdims=True)
    if extra is not None:
        m = jnp.maximum(m, extra)
    ps = [jnp.exp(s - m) for s in ss]
    den = jnp.sum(functools.reduce(jnp.add, [c for p in ps for c in _lane_chunks(p)]), axis=-1, keepdims=True)
    if extra is not None:
        den = den + jnp.exp(extra - m)
    return ps, den


def _mla_kernel(n_sets, q_ref, *refs):
    k_refs = refs[:n_sets]
    v_refs = refs[n_sets:2 * n_sets]
    o_ref = refs[2 * n_sets]
    rows = q_ref.shape[0]
    low_half = lax.broadcasted_iota(I32, (rows, LANES), 1) < MLA_V
    for j in range(MLA_HEADS // 2):
        outs = []
        for h in (2 * j, 2 * j + 1):
            qh = q_ref[:, h * LANES:(h + 1) * LANES]
            ps, den = _softmax_sets([_dot_t(qh, k[:, h * LANES:(h + 1) * LANES]) for k in k_refs])
            o = functools.reduce(jnp.add, [_dot(p.astype(BF16), v[:, j * LANES:(j + 1) * LANES]) for p, v in zip(ps, v_refs)])
            outs.append(o / den)
        o_ref[:, j * LANES:(j + 1) * LANES] = jnp.where(low_half, outs[0], outs[1]).astype(BF16)


def _mla_ctx(qm, km, vm):
    blk = S_CTX
    return pl.pallas_call(
        functools.partial(_mla_kernel, 1),
        grid=(B_CTX,),
        in_specs=[pl.BlockSpec((blk, MLA_HEADS * LANES), lambda b: (b, 0)),
                  pl.BlockSpec((blk, MLA_HEADS * LANES), lambda b: (b, 0)),
                  pl.BlockSpec((blk, MLA_HEADS * MLA_V), lambda b: (b, 0))],
        out_specs=pl.BlockSpec((blk, MLA_HEADS * MLA_V), lambda b: (b, 0)),
        out_shape=jax.ShapeDtypeStruct((T_CTX, MLA_HEADS * MLA_V), BF16),
        compiler_params=_cparams(1),
        name="mla_attn_ctx",
    )(qm, km, vm)


def _mla_lat(qm, km, vm, km_c, vm_c):
    lat0 = T_CTX // S_LAT
    return pl.pallas_call(
        functools.partial(_mla_kernel, 2),
        grid=(B_LAT, LAT_TILES_PER_SEQ),
        in_specs=[pl.BlockSpec((TM, MLA_HEADS * LANES), lambda b, t: (CTX_TILES + b * LAT_TILES_PER_SEQ + t, 0)),
                  pl.BlockSpec((PAST_LEN, MLA_HEADS * LANES), lambda b, t: (b, 0)),
                  pl.BlockSpec((S_LAT, MLA_HEADS * LANES), lambda b, t: (lat0 + b, 0)),
                  pl.BlockSpec((PAST_LEN, MLA_HEADS * MLA_V), lambda b, t: (b, 0)),
                  pl.BlockSpec((S_LAT, MLA_HEADS * MLA_V), lambda b, t: (lat0 + b, 0))],
        out_specs=pl.BlockSpec((TM, MLA_HEADS * MLA_V), lambda b, t: (b * LAT_TILES_PER_SEQ + t, 0)),
        out_shape=jax.ShapeDtypeStruct((T_LAT, MLA_HEADS * MLA_V), BF16),
        compiler_params=_cparams(2),
        name="mla_attn_lat",
    )(qm, km_c, km, vm_c, vm)


def _gqa_kernel(band, sink_ref, q_ref, *refs):
    n_sets = 4 if band else 1
    k_refs = refs[:n_sets]
    v_refs = refs[n_sets:2 * n_sets]
    o_ref = refs[2 * n_sets]
    rows = q_ref.shape[0]
    low_half = lax.broadcasted_iota(I32, (rows, LANES), 1) < GQA_HD
    masks = [None] * n_sets
    if band:
        n = pl.program_id(1)
        n_blk = pl.num_programs(1)
        qi = lax.broadcasted_iota(I32, (rows, WBLK), 0)
        kj = lax.broadcasted_iota(I32, (rows, WBLK), 1)
        masks[0] = kj >= qi + jnp.where(n > 0, 0, WBLK)
        masks[2] = kj <= qi - jnp.where(n < n_blk - 1, 0, WBLK)
    zero = jnp.zeros((rows, LANES), BF16)
    for j in range(GQA_HEADS // 2):
        g = j // (GQA_GROUP // 2)
        blk = q_ref[:, j * LANES:(j + 1) * LANES]
        outs = []
        for half, qh in enumerate((jnp.where(low_half, blk, zero), jnp.where(low_half, zero, blk))):
            sink = jnp.full((rows, 1), sink_ref[2 * j + half], F32)
            ss = []
            for k, msk in zip(k_refs, masks):
                s = _dot_t(qh, k[:, g * LANES:(g + 1) * LANES])
                ss.append(s if msk is None else jnp.where(msk, s, NEG_INF))
            ps, den = _softmax_sets(ss, sink)
            o = functools.reduce(jnp.add, [_dot(p.astype(BF16), v[:, g * LANES:(g + 1) * LANES]) for p, v in zip(ps, v_refs)])
            outs.append(o / den)
        o_ref[:, j * LANES:(j + 1) * LANES] = jnp.where(low_half, outs[0], outs[1]).astype(BF16)


def _gqa_ctx(sink, qg, kg, vg):
    blk = S_CTX
    grid_spec = pltpu.PrefetchScalarGridSpec(
        num_scalar_prefetch=1,
        grid=(B_CTX,),
        in_specs=[pl.BlockSpec((blk, GQA_HEADS * GQA_HD), lambda b, s: (b, 0)),
                  pl.BlockSpec((blk, GQA_KV * LANES), lambda b, s: (b, 0)),
                  pl.BlockSpec((blk, GQA_KV * LANES), lambda b, s: (b, 0))],
        out_specs=pl.BlockSpec((blk, GQA_HEADS * GQA_HD), lambda b, s: (b, 0)),
    )
    return pl.pallas_call(
        functools.partial(_gqa_kernel, False),
        grid_spec=grid_spec,
        out_shape=jax.ShapeDtypeStruct((T_CTX, GQA_HEADS * GQA_HD), BF16),
        compiler_params=_cparams(1),
        name="gqa_attn_ctx",
    )(sink, qg, kg, vg)


def _gqa_lat(sink, qg, kg, vg, kg_c, vg_c):
    nb = S_LAT // WBLK
    first = T_CTX // WBLK

    def blk(off):
        def index_map(b, n, s):
            return (first + b * nb + jnp.clip(n + off, 0, nb - 1), 0)
        return index_map

    kv_w = GQA_KV * LANES
    grid_spec = pltpu.PrefetchScalarGridSpec(
        num_scalar_prefetch=1,
        grid=(B_LAT, nb),
        in_specs=[pl.BlockSpec((WBLK, GQA_HEADS * GQA_HD), blk(0)),
                  pl.BlockSpec((WBLK, kv_w), blk(-1)), pl.BlockSpec((WBLK, kv_w), blk(0)), pl.BlockSpec((WBLK, kv_w), blk(1)),
                  pl.BlockSpec((PAST_LEN, kv_w), lambda b, n, s: (b, 0)),
                  pl.BlockSpec((WBLK, kv_w), blk(-1)), pl.BlockSpec((WBLK, kv_w), blk(0)), pl.BlockSpec((WBLK, kv_w), blk(1)),
                  pl.BlockSpec((PAST_LEN, kv_w), lambda b, n, s: (b, 0))],
        out_specs=pl.BlockSpec((WBLK, GQA_HEADS * GQA_HD), lambda b, n, s: (b * nb + n, 0)),
    )
    return pl.pallas_call(
        functools.partial(_gqa_kernel, True),
        grid_spec=grid_spec,
        out_shape=jax.ShapeDtypeStruct((T_LAT, GQA_HEADS * GQA_HD), BF16),
        compiler_params=_cparams(2),
        name="gqa_attn_lat",
    )(sink, qg, kg, kg, kg, kg_c, vg, vg, vg, vg_c)


def _pool_kernel(hp_p, hp_c, hp_n, pw_ref, ps_ref, o_ref):
    i = pl.program_id(0)
    is_ctx = i < CTX_TILES
    t4 = (i - CTX_TILES) % LAT_TILES_PER_SEQ
    seq_len = jnp.where(is_ctx, S_CTX, S_LAT)
    base = jnp.where(is_ctx, 0, t4 * TM)

    n_keys = TM + 2 * HALO
    first_key = jnp.where(jnp.logical_or(is_ctx, t4 == 0), HALO, 0)
    end_key = jnp.where(jnp.logical_or(is_ctx, t4 == LAT_TILES_PER_SEQ - 1), HALO + TM, n_keys)
    qi = lax.broadcasted_iota(I32, (TM, n_keys), 0)
    kj = lax.broadcasted_iota(I32, (TM, n_keys), 1)
    rel = kj - HALO - qi
    key_ok = (kj >= first_key) & (kj < end_key)

    cur = hp_c[...]
    keys = jnp.concatenate([hp_p[TM - HALO:TM, :], cur, hp_n[0:HALO, :]], axis=0)
    k_hi, k_lo = _split(keys)
    qpos = base + lax.broadcasted_iota(I32, (TM, 1), 0)
    for g, w in enumerate(POOL_WINDOWS):
        a = jnp.where((rel >= -(w // 2)) & (rel <= w // 2 - 1) & key_ok, 1.0, 0.0).astype(BF16)
        cols = slice(g * POOL_GROUP, (g + 1) * POOL_GROUP)
        s = _dot(a, k_hi[:, cols]) + _dot(a, k_lo[:, cols])
        lo = jnp.maximum(qpos - w // 2, 0)
        hi = jnp.minimum(qpos + w // 2 - 1, seq_len - 1)
        cnt = (hi - lo + 1).astype(F32)
        d = s / cnt - cur[:, cols]
        y = _dot(d.astype(BF16), pw_ref[g]) * ps_ref[:, cols]
        o_ref[:, cols] = y.astype(BF16)


def _pool(hp, pw, ps):
    return pl.pallas_call(
        _pool_kernel,
        grid=(N_TILES,),
        in_specs=[pl.BlockSpec((TM, POOL_WIDTH), lambda i: (jnp.maximum(i - 1, 0), 0)),
                  pl.BlockSpec((TM, POOL_WIDTH), lambda i: (i, 0)),
                  pl.BlockSpec((TM, POOL_WIDTH), lambda i: (jnp.minimum(i + 1, N_TILES - 1), 0)),
                  _const_spec((len(POOL_WINDOWS), POOL_GROUP, POOL_GROUP)),
                  _const_spec((1, POOL_WIDTH))],
        out_specs=pl.BlockSpec((TM, POOL_WIDTH), lambda i: (i, 0)),
        out_shape=jax.ShapeDtypeStruct((T_ALL, POOL_WIDTH), BF16),
        compiler_params=_cparams(1),
        name="pool_mix",
    )(hp, hp, hp, pw, ps)


def _merge_kernel(x_ref, mod_ref, yp_ref, ymc_ref, yml_ref, ygc_ref, ygl_ref, sg_ref,
                  wbp_ref, wbm_ref, wbg_ref, wo_ref, lg_ref, lb_ref, rw_ref, rb_ref,
                  x1_ref, u2_ref, ri_ref, rwt_ref, cnt_ref, carry):
    i = pl.program_id(0)

    @pl.when(i == 0)
    def _():
        carry[...] = jnp.zeros_like(carry)

    is_ctx = i < CTX_TILES
    ym = jnp.where(is_ctx, ymc_ref[...], yml_ref[...])
    yg = jnp.where(is_ctx, ygc_ref[...], ygl_ref[...])
    m = (sg_ref[:, 0:D_MODEL].astype(F32) * _dot(yp_ref[...], wbp_ref[...])
         + sg_ref[:, D_MODEL:2 * D_MODEL].astype(F32) * _dot(ym, wbm_ref[...])
         + sg_ref[:, 2 * D_MODEL:3 * D_MODEL].astype(F32) * _dot(yg, wbg_ref[...]))
    y = _dot(m.astype(BF16), wo_ref[...])
    gate1 = mod_ref[0, :, 2 * D_MODEL:3 * D_MODEL]
    x1 = _layer_norm(ALPHA_DN * x_ref[...] + gate1 * y, lg_ref[...], lb_ref[...])
    x1_ref[...] = x1
    sh2 = mod_ref[0, :, 3 * D_MODEL:4 * D_MODEL]
    sc2 = mod_ref[0, :, 4 * D_MODEL:5 * D_MODEL]
    u2 = x1 * (1.0 + sc2) + sh2
    u2_ref[...] = u2

    logits = _dot3(u2, rw_ref[...]) + rb_ref[...]
    lane = lax.broadcasted_iota(I32, (TM, LANES), 1)
    lane_f = lane.astype(F32)
    vals, idxs = [], []
    rest = logits
    for _ in range(TOP_K):
        mx = jnp.max(rest, axis=-1, keepdims=True)
        ix = jnp.min(jnp.where(rest == mx, lane_f, float(LANES)), axis=-1, keepdims=True).astype(I32)
        vals.append(mx)
        idxs.append(ix)
        rest = jnp.where(lane == ix, -jnp.inf, rest)
    es = [jnp.exp(v - vals[0]) for v in vals]
    den = functools.reduce(jnp.add, es)

    sel = functools.reduce(jnp.add, [jnp.where(lane == ix, 1.0, 0.0) for ix in idxs])
    r_i = lax.broadcasted_iota(I32, (TM, TM), 0)
    c_i = lax.broadcasted_iota(I32, (TM, TM), 1)
    below = jnp.where(c_i < r_i, 1.0, 0.0).astype(BF16)
    rank = _dot(below, sel.astype(BF16)) + carry[...]
    carry[...] = carry[...] + jnp.sum(sel, axis=0, keepdims=True)
    cnt_ref[...] = carry[...]

    ri = jnp.zeros((TM, LANES), I32)
    rwt = jnp.zeros((TM, LANES), F32)
    for k in range(TOP_K):
        rk = jnp.sum(jnp.where(lane == idxs[k], rank, 0.0), axis=-1, keepdims=True).astype(I32)
        ri = jnp.where(lane == k, idxs[k], ri)
        ri = jnp.where(lane == TOP_K + k, rk, ri)
        rwt = jnp.where(lane == k, es[k] / den, rwt)
    ri_ref[...] = ri
    rwt_ref[...] = rwt


def _merge(x, mod, yp, ymc, yml, ygc, ygl, sg, wbp, wbm, wbg, wo, lg, lb, rw, rb):
    tok = lambda w: pl.BlockSpec((TM, w), lambda i: (i, 0))
    ctx = lambda w: pl.BlockSpec((TM, w), lambda i: (jnp.minimum(i, CTX_TILES - 1), 0))
    lat = lambda w: pl.BlockSpec((TM, w), lambda i: (jnp.maximum(i - CTX_TILES, 0), 0))
    return pl.pallas_call(
        _merge_kernel,
        grid=(N_TILES,),
        in_specs=[tok(D_MODEL),
                  pl.BlockSpec((1, 1, 6 * D_MODEL), lambda i: (_mod_row(i), 0, 0)),
                  tok(POOL_WIDTH), ctx(512), lat(512), ctx(512), lat(512), tok(3 * D_MODEL),
                  _const_spec((POOL_WIDTH, D_MODEL)), _const_spec((512, D_MODEL)), _const_spec((512, D_MODEL)),
                  _const_spec((D_MODEL, D_MODEL)), _const_spec((1, D_MODEL)), _const_spec((1, D_MODEL)),
                  _const_spec((D_MODEL, LANES)), _const_spec((1, LANES))],
        out_specs=[tok(D_MODEL), tok(D_MODEL), tok(LANES), tok(LANES), _const_spec((1, LANES))],
        out_shape=[jax.ShapeDtypeStruct((T_ALL, D_MODEL), F32), jax.ShapeDtypeStruct((T_ALL, D_MODEL), F32),
                   jax.ShapeDtypeStruct((T_ALL, LANES), I32), jax.ShapeDtypeStruct((T_ALL, LANES), F32),
                   jax.ShapeDtypeStruct((1, LANES), F32)],
        scratch_shapes=[pltpu.VMEM((1, LANES), F32)],
        compiler_params=_cparams(1),
        name="merge_route",
    )(x, mod, yp, ymc, yml, ygc, ygl, sg, wbp, wbm, wbg, wo, lg, lb, rw, rb)


def _row_copy(src_ref, src_row, dst_ref, dst_row, sem):
    return pltpu.make_async_copy(src_ref.at[pl.ds(src_row, 1), :], dst_ref.at[pl.ds(dst_row, 1), :], sem)


ROWS_PER_ISSUE = 4
WAITS_PER_TRIP = 64


def _drain_rows(wait_one, n_rows):
    def trip(_, c):
        for _ in range(WAITS_PER_TRIP):
            wait_one()
        return c

    lax.fori_loop(0, n_rows // WAITS_PER_TRIP, trip, 0)


def _dispatch_kernel(dest_ref, zrow_ref, nu_ref, u_ref, xs_ref, zbuf, sem, zsem):
    i = pl.program_id(0)
    base = i * (TM * TOP_K)

    @pl.when(i == 0)
    def _():
        zbuf[...] = jnp.zeros_like(zbuf)

        def zero_tile(row):
            if not isinstance(row, int):
                row = pl.multiple_of(row, TME)
            return pltpu.make_async_copy(zbuf, xs_ref.at[pl.ds(row, TME), :], zsem)

        for e in range(N_EXPERTS):
            @pl.when(zrow_ref[e] >= 0)
            def _():
                zero_tile(zrow_ref[e]).start()

        def start_tail(j, c):
            zero_tile(j * TME).start()
            return c

        lax.fori_loop(nu_ref[0], NT_E, start_tail, 0)

        for e in range(N_EXPERTS):
            @pl.when(zrow_ref[e] >= 0)
            def _():
                zero_tile(0).wait()

        def wait_tail(j, c):
            zero_tile(0).wait()
            return c

        lax.fori_loop(nu_ref[0], NT_E, wait_tail, 0)

    def issue(g, c):
        for r in range(ROWS_PER_ISSUE):
            t = g * ROWS_PER_ISSUE + r
            for k in range(TOP_K):
                _row_copy(u_ref, t, xs_ref, dest_ref[base + t * TOP_K + k], sem).start()
        return c

    lax.fori_loop(0, TM // ROWS_PER_ISSUE, issue, 0)
    _drain_rows(lambda: _row_copy(u_ref, 0, xs_ref, 0, sem).wait(), TM * TOP_K)


def _dispatch(plan, u2):
    grid_spec = pltpu.PrefetchScalarGridSpec(
        num_scalar_prefetch=3,
        grid=(N_TILES,),
        in_specs=[pl.BlockSpec((TM, D_MODEL), lambda i, *_: (i, 0))],
        out_specs=pl.BlockSpec(memory_space=pl.ANY),
        scratch_shapes=[pltpu.VMEM((TME, D_MODEL), F32), pltpu.SemaphoreType.DMA(()), pltpu.SemaphoreType.DMA(())],
    )
    return pl.pallas_call(
        _dispatch_kernel,
        grid_spec=grid_spec,
        out_shape=jax.ShapeDtypeStruct((P_ROWS, D_MODEL), F32),
        compiler_params=_cparams(1),
        name="moe_dispatch",
    )(plan["dest"], plan["zero_row"], plan["n_used"], u2)


def _expert_kernel(layer, te_ref, nu_ref, slot_ref, first_ref, next_ref,
                   xs_ref, wgu_hbm, bgu_ref, wd_hbm, bd_ref, ys_ref, wgu_f, wd_f, wgu_s, wd_s, sems):
    i = pl.program_id(0)

    def weight_copies(e, s):
        return (pltpu.make_async_copy(wgu_hbm.at[layer, e], wgu_f.at[s], sems.at[0, s]),
                pltpu.make_async_copy(wd_hbm.at[layer, e], wd_f.at[s], sems.at[1, s]))

    @pl.when(first_ref[i] == 1)
    def _():
        s = slot_ref[i]

        @pl.when(i == 0)
        def _():
            for cp in weight_copies(te_ref[i], s):
                cp.start()

        for cp in weight_copies(te_ref[i], s):
            cp.wait()

        @pl.when(next_ref[i] >= 0)
        def _():
            for cp in weight_copies(next_ref[i], 1 - s):
                cp.start()

        wgu_s[...] = wgu_f[s].astype(BF16)
        wd_s[...] = wd_f[s].astype(BF16)

    @pl.when(i < nu_ref[0])
    def _():
        h = _dot(xs_ref[...].astype(BF16), wgu_s[...]) + bgu_ref[0, 0]
        glu = jnp.minimum(h[:, 0:D_FF], SWIGLU_LIMIT)
        lin = jnp.clip(h[:, D_FF:2 * D_FF], -SWIGLU_LIMIT, SWIGLU_LIMIT)
        a = glu * _sigmoid(SWIGLU_ALPHA * glu) * (lin + 1.0)
        ys_ref[...] = _dot(a.astype(BF16), wd_s[...]) + bd_ref[0, 0]

    @pl.when(i >= nu_ref[0])
    def _():
        ys_ref[...] = jnp.zeros_like(ys_ref)


def _experts(layer, plan, xs, w_gate_up, b_gate_up, w_down, b_down):
    row = lambda i, te, nu, *_: (jnp.minimum(i, nu[0] - 1), 0)
    bias = lambda i, te, *_: (layer, te[i], 0, 0)
    grid_spec = pltpu.PrefetchScalarGridSpec(
        num_scalar_prefetch=5,
        grid=(NT_E,),
        in_specs=[pl.BlockSpec((TME, D_MODEL), row),
                  pl.BlockSpec(memory_space=pl.ANY),
                  pl.BlockSpec((1, 1, 1, 2 * D_FF), bias),
                  pl.BlockSpec(memory_space=pl.ANY),
                  pl.BlockSpec((1, 1, 1, D_MODEL), bias)],
        out_specs=pl.BlockSpec((TME, D_MODEL), lambda i, *_: (i, 0)),
        scratch_shapes=[pltpu.VMEM((2, D_MODEL, 2 * D_FF), F32), pltpu.VMEM((2, D_FF, D_MODEL), F32),
                        pltpu.VMEM((D_MODEL, 2 * D_FF), BF16), pltpu.VMEM((D_FF, D_MODEL), BF16),
                        pltpu.SemaphoreType.DMA((2, 2))],
    )
    return pl.pallas_call(
        functools.partial(_expert_kernel, layer),
        grid_spec=grid_spec,
        out_shape=jax.ShapeDtypeStruct((P_ROWS, D_MODEL), F32),
        compiler_params=_cparams(1),
        name="moe_experts",
    )(plan["tile_expert"], plan["n_used"], plan["tile_slot"], plan["tile_first"], plan["tile_next"], xs, w_gate_up,
      b_gate_up.reshape(DEPTH, N_EXPERTS, 1, 2 * D_FF), w_down, b_down.reshape(DEPTH, N_EXPERTS, 1, D_MODEL))


def _combine_kernel(dest_ref, ys_ref, rwt_ref, x1_ref, mod_ref, lg_ref, lb_ref, o_ref, buf, sem):
    base = pl.program_id(0) * (TM * TOP_K)

    def issue(g, c):
        for r in range(ROWS_PER_ISSUE):
            t = g * ROWS_PER_ISSUE + r
            for k in range(TOP_K):
                _row_copy(ys_ref, dest_ref[base + t * TOP_K + k], buf.at[k], t, sem).start()
        return c

    lax.fori_loop(0, TM // ROWS_PER_ISSUE, issue, 0)
    _drain_rows(lambda: _row_copy(ys_ref, 0, buf.at[0], 0, sem).wait(), TM * TOP_K)

    moe = functools.reduce(jnp.add, [rwt_ref[:, k:k + 1] * buf[k] for k in range(TOP_K)])
    gate2 = mod_ref[0, :, 5 * D_MODEL:6 * D_MODEL]
    o_ref[...] = _layer_norm(ALPHA_DN * x1_ref[...] + gate2 * moe, lg_ref[...], lb_ref[...])


def _combine(dest, ys, rwt, x1, mod, lg, lb):
    grid_spec = pltpu.PrefetchScalarGridSpec(
        num_scalar_prefetch=1,
        grid=(N_TILES,),
        in_specs=[pl.BlockSpec(memory_space=pl.ANY),
                  pl.BlockSpec((TM, LANES), lambda i, d: (i, 0)),
                  pl.BlockSpec((TM, D_MODEL), lambda i, d: (i, 0)),
                  pl.BlockSpec((1, 1, 6 * D_MODEL), lambda i, d: (_mod_row(i), 0, 0)),
                  pl.BlockSpec((1, D_MODEL), lambda i, d: (0, 0)),
                  pl.BlockSpec((1, D_MODEL), lambda i, d: (0, 0))],
        out_specs=pl.BlockSpec((TM, D_MODEL), lambda i, d: (i, 0)),
        scratch_shapes=[pltpu.VMEM((TOP_K, TM, D_MODEL), F32), pltpu.SemaphoreType.DMA(())],
    )
    return pl.pallas_call(
        _combine_kernel,
        grid_spec=grid_spec,
        out_shape=jax.ShapeDtypeStruct((T_ALL, D_MODEL), F32),
        compiler_params=_cparams(1),
        name="moe_combine",
    )(dest, ys, rwt, x1, mod, lg, lb)


def _pad_heads(w, n_heads, width):
    rows = w.shape[0]
    w = w.reshape(rows, n_heads, width)
    return jnp.pad(w, ((0, 0), (0, 0), (0, LANES - width))).reshape(rows, n_heads * LANES)


def _dup_heads(w):
    rows = w.shape[0]
    w = w.reshape(rows, GQA_KV, 1, GQA_HD)
    return jnp.concatenate([w, w], axis=2).reshape(rows, GQA_KV * LANES)


def _pack_w_in(w):
    cuts = [0, 512, 896, 1152, 1184, 1696, 1824, 1952, 5024]
    hp, cq, ckv, kr, q, k, v, g = [w[:, a:b] for a, b in zip(cuts[:-1], cuts[1:])]
    packed = jnp.concatenate([hp, cq, ckv, _pad_heads(kr, 1, MLA_ROPE), q, _dup_heads(k), _dup_heads(v), g], axis=1)
    return packed.astype(BF16)


def _pack_mla(wuq, wukv):
    wuq_p = _pad_heads(wuq, MLA_HEADS, MLA_QK).astype(BF16)
    kv = wukv.reshape(MLA_KV_LORA, MLA_HEADS, MLA_NOPE + MLA_V)
    wk_top = _pad_heads(kv[:, :, :MLA_NOPE].reshape(MLA_KV_LORA, MLA_HEADS * MLA_NOPE), MLA_HEADS, MLA_NOPE)
    place = jnp.zeros((LANES, MLA_HEADS, LANES), F32)
    r = jnp.arange(MLA_ROPE)
    place = place.at[r, :, MLA_NOPE + r].set(1.0).reshape(LANES, MLA_HEADS * LANES)
    wk = jnp.concatenate([wk_top, place], axis=0).astype(BF16)
    wv = kv[:, :, MLA_NOPE:].reshape(MLA_KV_LORA, MLA_HEADS * MLA_V).astype(BF16)
    return wuq_p, wk, wv


def _rope_table(dim, lane0):
    quarter = dim // 4
    pos = jnp.arange(S_LAT)
    rows = (pos // GRID_W).astype(F32)
    cols = (pos % GRID_W).astype(F32)
    freqs = jnp.power(ROPE_THETA, -jnp.arange(quarter, dtype=F32) / quarter)
    ang_r = rows[:, None] * freqs[None, :]
    ang_c = cols[:, None] * freqs[None, :]
    zero = jnp.zeros((S_LAT, quarter), F32)
    cos = jnp.concatenate([jnp.cos(ang_r), jnp.cos(ang_r), jnp.cos(ang_c), jnp.cos(ang_c)], axis=1)
    s_up = jnp.concatenate([-jnp.sin(ang_r), zero, -jnp.sin(ang_c), zero], axis=1)
    s_dn = jnp.concatenate([zero, jnp.sin(ang_r), zero, jnp.sin(ang_c)], axis=1)

    def place(t, fill):
        if lane0 is None:
            return jnp.tile(t, (1, LANES // dim))
        full = jnp.full((S_LAT, LANES), fill, F32)
        return full.at[:, lane0:lane0 + dim].set(t)

    planes = jnp.stack([place(cos, 1.0), place(s_up, 0.0), place(s_dn, 0.0)])
    ident = jnp.stack([jnp.ones((TM, LANES), F32), jnp.zeros((TM, LANES), F32), jnp.zeros((TM, LANES), F32)])
    return jnp.concatenate([planes, ident], axis=1)


def _routing_plan(ri, counts):
    idx = ri[:, 0:TOP_K]
    rank = ri[:, TOP_K:2 * TOP_K]
    cnt = counts[0, :N_EXPERTS].astype(I32)
    padded = (cnt + TME - 1) // TME * TME
    ends = jnp.cumsum(padded)
    starts = ends - padded
    dest = (starts[idx] + rank).reshape(-1).astype(I32)
    n_used = (ends[-1] // TME).astype(I32)
    tile_row = jnp.arange(NT_E, dtype=I32) * TME
    tile_expert = jnp.sum((tile_row[:, None] >= ends[None, :]).astype(I32), axis=1)
    last = jnp.sum((jnp.maximum(ends[-1] - TME, 0) >= ends).astype(I32))
    tile_expert = jnp.minimum(tile_expert, last).astype(I32)

    e_ids = jnp.arange(N_EXPERTS, dtype=I32)
    has = padded > 0
    ordinal = jnp.cumsum(has.astype(I32)) - 1
    later = jnp.where(has[None, :] & (e_ids[None, :] > e_ids[:, None]), e_ids[None, :], N_EXPERTS)
    nxt = jnp.min(later, axis=1)
    nxt = jnp.where(nxt == N_EXPERTS, -1, nxt).astype(I32)
    in_use = tile_row < ends[-1]
    tile_first = (in_use & (tile_row == starts[tile_expert])).astype(I32)
    zero_row = jnp.where(has, ends - TME, -1).astype(I32)
    return {"dest": dest, "tile_expert": tile_expert, "n_used": n_used.reshape(1),
            "tile_slot": (ordinal[tile_expert] % 2).astype(I32), "tile_first": tile_first,
            "tile_next": nxt[tile_expert], "zero_row": zero_row}


def kernel(x_prompt, x_sample, c, cache_mla_ckv, cache_mla_krope, cache_gqa_k, cache_gqa_v, c_ctx, w_ada, b_ada, w_in, mla_q_norm, mla_kv_norm, w_mla_uq, w_mla_ukv, gqa_sink, pool_w, pool_scale, w_branch_pool, w_branch_mla, w_branch_gqa, w_out, ln1_g, ln1_b, ln2_g, ln2_b, router_w, router_b, w_gate_up, b_gate_up, w_down, b_down):
    x = jnp.concatenate([x_prompt.reshape(T_CTX, D_MODEL), x_sample.reshape(T_LAT, D_MODEL)], axis=0)
    cond8 = jnp.concatenate([c_ctx[None, :], c, jnp.zeros((SUBLANES - 1 - B_LAT, D_MODEL), F32)], axis=0)
    mod_all = _ada_params(cond8, w_ada, b_ada)

    tab_g = _rope_table(GQA_HD, None)
    tab_q = _rope_table(MLA_ROPE, MLA_NOPE)
    tab_k = _rope_table(MLA_ROPE, 0)

    outs = {"ckv": [], "kr": [], "gk": [], "gv": []}
    for l in range(DEPTH):
        mod = mod_all[l].reshape(SUBLANES, 1, 6 * D_MODEL)
        wp = _pack_w_in(w_in[l])
        wuq_p, wk, wv = _pack_mla(w_mla_uq[l], w_mla_ukv[l])

        (hp, qm, ckv, kr, km, vm, qg, kg, vg, nk, nv, sg) = _in_proj(
            x, mod, wp, mla_q_norm[l][None, :], mla_kv_norm[l][None, :], wuq_p, wk, wv, tab_g, tab_q, tab_k)
        outs["ckv"].append(ckv[:T_CTX].reshape(B_CTX, S_CTX, MLA_KV_LORA))
        outs["kr"].append(kr[:T_CTX, :MLA_ROPE].reshape(B_CTX, S_CTX, MLA_ROPE))
        outs["gk"].append(nk[:T_CTX].reshape(B_CTX, S_CTX, GQA_KV, GQA_HD))
        outs["gv"].append(nv[:T_CTX].reshape(B_CTX, S_CTX, GQA_KV, GQA_HD))

        ckv_c = cache_mla_ckv[:, l].reshape(B_LAT * PAST_LEN, MLA_KV_LORA)
        kr_c = jnp.pad(cache_mla_krope[:, l].reshape(B_LAT * PAST_LEN, MLA_ROPE), ((0, 0), (0, LANES - MLA_ROPE)))
        km_c, vm_c = _expand_cache(ckv_c, kr_c, wk, wv)
        gk_c = _dup_heads(cache_gqa_k[:, l].reshape(B_LAT * PAST_LEN, GQA_KV * GQA_HD)).astype(BF16)
        gv_c = _dup_heads(cache_gqa_v[:, l].reshape(B_LAT * PAST_LEN, GQA_KV * GQA_HD)).astype(BF16)

        yp = _pool(hp, pool_w[l].astype(BF16), pool_scale[l][None, :])
        ymc = _mla_ctx(qm, km, vm)
        yml = _mla_lat(qm, km, vm, km_c, vm_c)
        ygc = _gqa_ctx(gqa_sink[l], qg, kg, vg)
        ygl = _gqa_lat(gqa_sink[l], qg, kg, vg, gk_c, gv_c)

        rw = jnp.pad(router_w[l], ((0, 0), (0, LANES - N_EXPERTS)))
        rb = jnp.concatenate([router_b[l], jnp.full((LANES - N_EXPERTS,), -jnp.inf, F32)])[None, :]
        x1, u2, ri, rwt, counts = _merge(
            x, mod, yp, ymc, yml, ygc, ygl, sg,
            w_branch_pool[l].astype(BF16), w_branch_mla[l].astype(BF16), w_branch_gqa[l].astype(BF16),
            w_out[l].astype(BF16), ln1_g[l][None, :], ln1_b[l][None, :], rw, rb)

        plan = _routing_plan(ri, counts)
        xs = _dispatch(plan, u2)
        ys = _experts(l, plan, xs, w_gate_up, b_gate_up, w_down, b_down)
        x = _combine(plan["dest"], ys, rwt, x1, mod, ln2_g[l][None, :], ln2_b[l][None, :])

    y_prompt = x[:T_CTX].reshape(B_CTX, S_CTX, D_MODEL)
    y_sample = x[T_CTX:].reshape(B_LAT, S_LAT, D_MODEL)
    return (y_prompt, y_sample, jnp.stack(outs["ckv"], axis=1), jnp.stack(outs["kr"], axis=1),
            jnp.stack(outs["gk"], axis=1), jnp.stack(outs["gv"], axis=1))
```

```python
import functools
import math

import jax
import jax.numpy as jnp
from jax import lax
from jax.experimental import pallas as pl
from jax.experimental.pallas import tpu as pltpu

F32 = jnp.float32
BF16 = jnp.bfloat16
I32 = jnp.int32

D_MODEL = 1024
DEPTH = 2
B_CTX, S_CTX = 16, 256
B_LAT, S_LAT = 4, 1024
PAST_LEN = 512
GRID_W = 64
ROPE_THETA = 10000.0
ALPHA_DN = (2 * DEPTH) ** 0.25
LN_EPS = 1e-5
RMS_EPS = 1e-6
NEG_INF = -1e30
POOL_WINDOWS = (2, 4, 8, 16)
POOL_GROUP = 128
POOL_WIDTH = 512
MLA_HEADS = 8
MLA_NOPE = 64
MLA_ROPE = 32
MLA_V = 64
MLA_QK = MLA_NOPE + MLA_ROPE
MLA_Q_LORA = 384
MLA_KV_LORA = 256
MLA_SCALE = 1.0 / math.sqrt(MLA_QK)
GQA_HEADS = 8
GQA_KV = 2
GQA_HD = 64
GQA_GROUP = GQA_HEADS // GQA_KV
GQA_SCALE = 1.0 / math.sqrt(GQA_HD)
WBLK = 128
N_EXPERTS = 32
TOP_K = 4
D_FF = 1024
SWIGLU_LIMIT = 7.0
SWIGLU_ALPHA = 1.702

LANES = 128
SUBLANES = 8
VMEM_LIMIT = 56 * 1024 * 1024

T_CTX = B_CTX * S_CTX
T_LAT = B_LAT * S_LAT
T_ALL = T_CTX + T_LAT
TM = 256
N_TILES = T_ALL // TM
CTX_TILES = T_CTX // TM
LAT_TILES_PER_SEQ = S_LAT // TM
HALO = 16

C_HP = 0
C_CQ = C_HP + POOL_WIDTH
C_CKV = C_CQ + MLA_Q_LORA
C_KR = C_CKV + MLA_KV_LORA
C_GQ = C_KR + LANES
C_GK = C_GQ + GQA_HEADS * GQA_HD
C_GV = C_GK + GQA_KV * LANES
C_G = C_GV + GQA_KV * LANES
C_END = C_G + 3 * D_MODEL

TME = 256
N_PAIRS = T_ALL * TOP_K
NT_E = N_PAIRS // TME + N_EXPERTS
P_ROWS = NT_E * TME


def _cparams(n_axes):
    return pltpu.CompilerParams(dimension_semantics=("arbitrary",) * n_axes, vmem_limit_bytes=VMEM_LIMIT)


def _dot(a, b):
    return jnp.dot(a, b, preferred_element_type=F32)


def _dot_t(a, b):
    return lax.dot_general(a, b, (((1,), (1,)), ((), ())), preferred_element_type=F32)


def _split(x):
    hi = x.astype(BF16)
    lo = (x - hi.astype(F32)).astype(BF16)
    return hi, lo


def _dot3(a, b):
    a_hi, a_lo = _split(a)
    b_hi, b_lo = _split(b)
    return _dot(a_hi, b_hi) + _dot(a_hi, b_lo) + _dot(a_lo, b_hi)


def _sigmoid(x):
    return 1.0 / (1.0 + jnp.exp(-x))


def _layer_norm(z, g, b):
    mu = jnp.mean(z, axis=-1, keepdims=True)
    zc = z - mu
    var = jnp.mean(zc * zc, axis=-1, keepdims=True)
    return zc * lax.rsqrt(var + LN_EPS) * g + b


def _mod_row(i):
    return jnp.where(i < CTX_TILES, 0, 1 + (i - CTX_TILES) // LAT_TILES_PER_SEQ)


def _pos_block(i):
    return jnp.where(i < CTX_TILES, LAT_TILES_PER_SEQ, (i - CTX_TILES) % LAT_TILES_PER_SEQ)


def _const_spec(shape):
    nd = len(shape)
    return pl.BlockSpec(shape, lambda *_: (0,) * nd)


def _ada_kernel(c_ref, w_ref, b_ref, o_ref):
    c = c_ref[...]
    s = c * _sigmoid(c)
    o_ref[0] = _dot3(s, w_ref[0]) + b_ref[0]


def _ada_params(cond8, w_ada, b_ada):
    n_col = 6 * D_MODEL
    blk = 1024
    return pl.pallas_call(
        _ada_kernel,
        grid=(DEPTH, n_col // blk),
        in_specs=[
            pl.BlockSpec((SUBLANES, D_MODEL), lambda l, j: (0, 0)),
            pl.BlockSpec((1, D_MODEL, blk), lambda l, j: (l, 0, j)),
            pl.BlockSpec((1, 1, blk), lambda l, j: (l, 0, j)),
        ],
        out_specs=pl.BlockSpec((1, SUBLANES, blk), lambda l, j: (l, 0, j)),
        out_shape=jax.ShapeDtypeStruct((DEPTH, SUBLANES, n_col), F32),
        compiler_params=_cparams(2),
        name="ada_params",
    )(cond8, w_ada, b_ada.reshape(DEPTH, 1, n_col))


def _rope(x, tab_ref, shift):
    return (x * tab_ref[0]
            + pltpu.roll(x, LANES - shift, 1) * tab_ref[1]
            + pltpu.roll(x, shift, 1) * tab_ref[2])


def _rms(h, g):
    return h * lax.rsqrt(jnp.mean(h * h, axis=-1, keepdims=True) + RMS_EPS) * g


def _in_kernel(x_ref, mod_ref, wp_ref, qn_ref, kvn_ref, wuq_ref, wk_ref, wv_ref, tg_ref, tq_ref, tk_ref,
               hp_ref, qm_ref, ckv_ref, kr_ref, km_ref, vm_ref, qg_ref, kg_ref, vg_ref, nk_ref, nv_ref, sg_ref):
    x = x_ref[...]
    sh = mod_ref[0, :, 0:D_MODEL]
    sc = mod_ref[0, :, D_MODEL:2 * D_MODEL]
    u = (x * (1.0 + sc) + sh).astype(BF16)

    def proj(lo, hi):
        return _dot(u, wp_ref[:, lo:hi])

    hp_ref[...] = proj(C_HP, C_CQ)

    qn = _rms(proj(C_CQ, C_CKV), qn_ref[...]).astype(BF16)
    q = _dot(qn, wuq_ref[...]) * MLA_SCALE
    for h in range(MLA_HEADS):
        qm_ref[:, h * LANES:(h + 1) * LANES] = _rope(q[:, h * LANES:(h + 1) * LANES], tq_ref, MLA_ROPE // 4).astype(BF16)

    ckv = _rms(proj(C_CKV, C_KR), kvn_ref[...])
    ckv_ref[...] = ckv
    kr = _rope(proj(C_KR, C_GQ), tk_ref, MLA_ROPE // 4)
    kr_ref[...] = kr
    ckv_b = ckv.astype(BF16)
    km_ref[...] = (_dot(ckv_b, wk_ref[0:MLA_KV_LORA, :]) + _dot(kr.astype(BF16), wk_ref[MLA_KV_LORA:, :])).astype(BF16)
    vm_ref[...] = _dot(ckv_b, wv_ref[...]).astype(BF16)

    hq = proj(C_GQ, C_GK) * GQA_SCALE
    for j in range(GQA_HEADS * GQA_HD // LANES):
        qg_ref[:, j * LANES:(j + 1) * LANES] = _rope(hq[:, j * LANES:(j + 1) * LANES], tg_ref, GQA_HD // 4).astype(BF16)
    hk = proj(C_GK, C_GV)
    hv = proj(C_GV, C_G)
    for g in range(GQA_KV):
        kg_ref[:, g * LANES:(g + 1) * LANES] = _rope(hk[:, g * LANES:(g + 1) * LANES], tg_ref, GQA_HD // 4).astype(BF16)
    vg_ref[...] = hv.astype(BF16)
    low_half = lax.broadcasted_iota(I32, (TM, LANES), 1) < GQA_HD
    nk_ref[...] = jnp.where(low_half, hk[:, 0:LANES], hk[:, LANES:2 * LANES])
    nv_ref[...] = jnp.where(low_half, hv[:, 0:LANES], hv[:, LANES:2 * LANES])

    for j in range(3):
        sg_ref[:, j * D_MODEL:(j + 1) * D_MODEL] = _sigmoid(proj(C_G + j * D_MODEL, C_G + (j + 1) * D_MODEL)).astype(BF16)


def _in_proj(x, mod, wp, qn, kvn, wuq, wk, wv, tab_g, tab_q, tab_k):
    tok = lambda w: pl.BlockSpec((TM, w), lambda i: (i, 0))
    tab = pl.BlockSpec((3, TM, LANES), lambda i: (0, _pos_block(i), 0))
    out_widths = [(POOL_WIDTH, F32), (MLA_HEADS * LANES, BF16), (MLA_KV_LORA, F32), (LANES, F32),
                  (MLA_HEADS * LANES, BF16), (MLA_HEADS * MLA_V, BF16), (GQA_HEADS * GQA_HD, BF16),
                  (GQA_KV * LANES, BF16), (GQA_KV * LANES, BF16), (GQA_KV * GQA_HD, F32), (GQA_KV * GQA_HD, F32),
                  (3 * D_MODEL, BF16)]
    return pl.pallas_call(
        _in_kernel,
        grid=(N_TILES,),
        in_specs=[
            tok(D_MODEL),
            pl.BlockSpec((1, 1, 6 * D_MODEL), lambda i: (_mod_row(i), 0, 0)),
            _const_spec((D_MODEL, C_END)),
            _const_spec((1, MLA_Q_LORA)),
            _const_spec((1, MLA_KV_LORA)),
            _const_spec((MLA_Q_LORA, MLA_HEADS * LANES)),
            _const_spec((MLA_KV_LORA + LANES, MLA_HEADS * LANES)),
            _const_spec((MLA_KV_LORA, MLA_HEADS * MLA_V)),
            tab, tab, tab,
        ],
        out_specs=[tok(w) for w, _ in out_widths],
        out_shape=[jax.ShapeDtypeStruct((T_ALL, w), dt) for w, dt in out_widths],
        compiler_params=_cparams(1),
        name="in_proj",
    )(x, mod, wp, qn, kvn, wuq, wk, wv, tab_g, tab_q, tab_k)


def _expand_kernel(ckv_ref, kr_ref, wk_ref, wv_ref, km_ref, vm_ref):
    ckv_b = ckv_ref[...].astype(BF16)
    km_ref[...] = (_dot(ckv_b, wk_ref[0:MLA_KV_LORA, :]) + _dot(kr_ref[...].astype(BF16), wk_ref[MLA_KV_LORA:, :])).astype(BF16)
    vm_ref[...] = _dot(ckv_b, wv_ref[...]).astype(BF16)


def _expand_cache(ckv, kr_pad, wk, wv):
    rows = ckv.shape[0]
    blk = PAST_LEN
    return pl.pallas_call(
        _expand_kernel,
        grid=(rows // blk,),
        in_specs=[
            pl.BlockSpec((blk, MLA_KV_LORA), lambda i: (i, 0)),
            pl.BlockSpec((blk, LANES), lambda i: (i, 0)),
            _const_spec((MLA_KV_LORA + LANES, MLA_HEADS * LANES)),
            _const_spec((MLA_KV_LORA, MLA_HEADS * MLA_V)),
        ],
        out_specs=[pl.BlockSpec((blk, MLA_HEADS * LANES), lambda i: (i, 0)),
                   pl.BlockSpec((blk, MLA_HEADS * MLA_V), lambda i: (i, 0))],
        out_shape=[jax.ShapeDtypeStruct((rows, MLA_HEADS * LANES), BF16),
                   jax.ShapeDtypeStruct((rows, MLA_HEADS * MLA_V), BF16)],
        compiler_params=_cparams(1),
        name="mla_expand_cache",
    )(ckv, kr_pad, wk, wv)


def _lane_chunks(x):
    return [x[:, c:c + LANES] for c in range(0, x.shape[1], LANES)]


def _softmax_sets(ss, extra=None):
    m = jnp.max(functools.reduce(jnp.maximum, [c for s in ss for c in _lane_chunks(s)]), axis=-1, keepdims=True)
    if extra is not None:
        m = jnp.maximum(m, extra)
    ps = [jnp.exp(s - m) for s in ss]
    den = jnp.sum(functools.reduce(jnp.add, [c for p in ps for c in _lane_chunks(p)]), axis=-1, keepdims=True)
    if extra is not None:
        den = den + jnp.exp(extra - m)
    return ps, den


def _mla_kernel(n_sets, q_ref, *refs):
    k_refs = refs[:n_sets]
    v_refs = refs[n_sets:2 * n_sets]
    o_ref = refs[2 * n_sets]
    rows = q_ref.shape[0]
    low_half = lax.broadcasted_iota(I32, (rows, LANES), 1) < MLA_V
    for j in range(MLA_HEADS // 2):
        outs = []
        for h in (2 * j, 2 * j + 1):
            qh = q_ref[:, h * LANES:(h + 1) * LANES]
            ps, den = _softmax_sets([_dot_t(qh, k[:, h * LANES:(h + 1) * LANES]) for k in k_refs])
            o = functools.reduce(jnp.add, [_dot(p.astype(BF16), v[:, j * LANES:(j + 1) * LANES]) for p, v in zip(ps, v_refs)])
            outs.append(o / den)
        o_ref[:, j * LANES:(j + 1) * LANES] = jnp.where(low_half, outs[0], outs[1]).astype(BF16)


def _mla_ctx(qm, km, vm):
    blk = S_CTX
    return pl.pallas_call(
        functools.partial(_mla_kernel, 1),
        grid=(B_CTX,),
        in_specs=[pl.BlockSpec((blk, MLA_HEADS * LANES), lambda b: (b, 0)),
                  pl.BlockSpec((blk, MLA_HEADS * LANES), lambda b: (b, 0)),
                  pl.BlockSpec((blk, MLA_HEADS * MLA_V), lambda b: (b, 0))],
        out_specs=pl.BlockSpec((blk, MLA_HEADS * MLA_V), lambda b: (b, 0)),
        out_shape=jax.ShapeDtypeStruct((T_CTX, MLA_HEADS * MLA_V), BF16),
        compiler_params=_cparams(1),
        name="mla_attn_ctx",
    )(qm, km, vm)


def _mla_lat(qm, km, vm, km_c, vm_c):
    lat0 = T_CTX // S_LAT
    return pl.pallas_call(
        functools.partial(_mla_kernel, 2),
        grid=(B_LAT, LAT_TILES_PER_SEQ),
        in_specs=[pl.BlockSpec((TM, MLA_HEADS * LANES), lambda b, t: (CTX_TILES + b * LAT_TILES_PER_SEQ + t, 0)),
                  pl.BlockSpec((PAST_LEN, MLA_HEADS * LANES), lambda b, t: (b, 0)),
                  pl.BlockSpec((S_LAT, MLA_HEADS * LANES), lambda b, t: (lat0 + b, 0)),
                  pl.BlockSpec((PAST_LEN, MLA_HEADS * MLA_V), lambda b, t: (b, 0)),
                  pl.BlockSpec((S_LAT, MLA_HEADS * MLA_V), lambda b, t: (lat0 + b, 0))],
        out_specs=pl.BlockSpec((TM, MLA_HEADS * MLA_V), lambda b, t: (b * LAT_TILES_PER_SEQ + t, 0)),
        out_shape=jax.ShapeDtypeStruct((T_LAT, MLA_HEADS * MLA_V), BF16),
        compiler_params=_cparams(2),
        name="mla_attn_lat",
    )(qm, km_c, km, vm_c, vm)


def _gqa_kernel(band, sink_ref, q_ref, *refs):
    n_sets = 4 if band else 1
    k_refs = refs[:n_sets]
    v_refs = refs[n_sets:2 * n_sets]
    o_ref = refs[2 * n_sets]
    rows = q_ref.shape[0]
    low_half = lax.broadcasted_iota(I32, (rows, LANES), 1) < GQA_HD
    masks = [None] * n_sets
    if band:
        n = pl.program_id(1)
        n_blk = pl.num_programs(1)
        qi = lax.broadcasted_iota(I32, (rows, WBLK), 0)
        kj = lax.broadcasted_iota(I32, (rows, WBLK), 1)
        masks[0] = kj >= qi + jnp.where(n > 0, 0, WBLK)
        masks[2] = kj <= qi - jnp.where(n < n_blk - 1, 0, WBLK)
    zero = jnp.zeros((rows, LANES), BF16)
    for j in range(GQA_HEADS // 2):
        g = j // (GQA_GROUP // 2)
        blk = q_ref[:, j * LANES:(j + 1) * LANES]
        outs = []
        for half, qh in enumerate((jnp.where(low_half, blk, zero), jnp.where(low_half, zero, blk))):
            sink = jnp.full((rows, 1), sink_ref[2 * j + half], F32)
            ss = []
            for k, msk in zip(k_refs, masks):
                s = _dot_t(qh, k[:, g * LANES:(g + 1) * LANES])
                ss.append(s if msk is None else jnp.where(msk, s, NEG_INF))
            ps, den = _softmax_sets(ss, sink)
            o = functools.reduce(jnp.add, [_dot(p.astype(BF16), v[:, g * LANES:(g + 1) * LANES]) for p, v in zip(ps, v_refs)])
            outs.append(o / den)
        o_ref[:, j * LANES:(j + 1) * LANES] = jnp.where(low_half, outs[0], outs[1]).astype(BF16)


def _gqa_ctx(sink, qg, kg, vg):
    blk = S_CTX
    grid_spec = pltpu.PrefetchScalarGridSpec(
        num_scalar_prefetch=1,
        grid=(B_CTX,),
        in_specs=[pl.BlockSpec((blk, GQA_HEADS * GQA_HD), lambda b, s: (b, 0)),
                  pl.BlockSpec((blk, GQA_KV * LANES), lambda b, s: (b, 0)),
                  pl.BlockSpec((blk, GQA_KV * LANES), lambda b, s: (b, 0))],
        out_specs=pl.BlockSpec((blk, GQA_HEADS * GQA_HD), lambda b, s: (b, 0)),
    )
    return pl.pallas_call(
        functools.partial(_gqa_kernel, False),
        grid_spec=grid_spec,
        out_shape=jax.ShapeDtypeStruct((T_CTX, GQA_HEADS * GQA_HD), BF16),
        compiler_params=_cparams(1),
        name="gqa_attn_ctx",
    )(sink, qg, kg, vg)


def _gqa_lat(sink, qg, kg, vg, kg_c, vg_c):
    nb = S_LAT // WBLK
    first = T_CTX // WBLK

    def blk(off):
        def index_map(b, n, s):
            return (first + b * nb + jnp.clip(n + off, 0, nb - 1), 0)
        return index_map

    kv_w = GQA_KV * LANES
    grid_spec = pltpu.PrefetchScalarGridSpec(
        num_scalar_prefetch=1,
        grid=(B_LAT, nb),
        in_specs=[pl.BlockSpec((WBLK, GQA_HEADS * GQA_HD), blk(0)),
                  pl.BlockSpec((WBLK, kv_w), blk(-1)), pl.BlockSpec((WBLK, kv_w), blk(0)), pl.BlockSpec((WBLK, kv_w), blk(1)),
                  pl.BlockSpec((PAST_LEN, kv_w), lambda b, n, s: (b, 0)),
                  pl.BlockSpec((WBLK, kv_w), blk(-1)), pl.BlockSpec((WBLK, kv_w), blk(0)), pl.BlockSpec((WBLK, kv_w), blk(1)),
                  pl.BlockSpec((PAST_LEN, kv_w), lambda b, n, s: (b, 0))],
        out_specs=pl.BlockSpec((WBLK, GQA_HEADS * GQA_HD), lambda b, n, s: (b * nb + n, 0)),
    )
    return pl.pallas_call(
        functools.partial(_gqa_kernel, True),
        grid_spec=grid_spec,
        out_shape=jax.ShapeDtypeStruct((T_LAT, GQA_HEADS * GQA_HD), BF16),
        compiler_params=_cparams(2),
        name="gqa_attn_lat",
    )(sink, qg, kg, kg, kg, kg_c, vg, vg, vg, vg_c)


def _pool_kernel(hp_p, hp_c, hp_n, pw_ref, ps_ref, o_ref):
    i = pl.program_id(0)
    is_ctx = i < CTX_TILES
    t4 = (i - CTX_TILES) % LAT_TILES_PER_SEQ
    seq_len = jnp.where(is_ctx, S_CTX, S_LAT)
    base = jnp.where(is_ctx, 0, t4 * TM)

    n_keys = TM + 2 * HALO
    first_key = jnp.where(jnp.logical_or(is_ctx, t4 == 0), HALO, 0)
    end_key = jnp.where(jnp.logical_or(is_ctx, t4 == LAT_TILES_PER_SEQ - 1), HALO + TM, n_keys)
    qi = lax.broadcasted_iota(I32, (TM, n_keys), 0)
    kj = lax.broadcasted_iota(I32, (TM, n_keys), 1)
    rel = kj - HALO - qi
    key_ok = (kj >= first_key) & (kj < end_key)

    cur = hp_c[...]
    keys = jnp.concatenate([hp_p[TM - HALO:TM, :], cur, hp_n[0:HALO, :]], axis=0)
    k_hi, k_lo = _split(keys)
    qpos = base + lax.broadcasted_iota(I32, (TM, 1), 0)
    for g, w in enumerate(POOL_WINDOWS):
        a = jnp.where((rel >= -(w // 2)) & (rel <= w // 2 - 1) & key_ok, 1.0, 0.0).astype(BF16)
        cols = slice(g * POOL_GROUP, (g + 1) * POOL_GROUP)
        s = _dot(a, k_hi[:, cols]) + _dot(a, k_lo[:, cols])
        lo = jnp.maximum(qpos - w // 2, 0)
        hi = jnp.minimum(qpos + w // 2 - 1, seq_len - 1)
        cnt = (hi - lo + 1).astype(F32)
        d = s / cnt - cur[:, cols]
        y = _dot(d.astype(BF16), pw_ref[g]) * ps_ref[:, cols]
        o_ref[:, cols] = y.astype(BF16)


def _pool(hp, pw, ps):
    return pl.pallas_call(
        _pool_kernel,
        grid=(N_TILES,),
        in_specs=[pl.BlockSpec((TM, POOL_WIDTH), lambda i: (jnp.maximum(i - 1, 0), 0)),
                  pl.BlockSpec((TM, POOL_WIDTH), lambda i: (i, 0)),
                  pl.BlockSpec((TM, POOL_WIDTH), lambda i: (jnp.minimum(i + 1, N_TILES - 1), 0)),
                  _const_spec((len(POOL_WINDOWS), POOL_GROUP, POOL_GROUP)),
                  _const_spec((1, POOL_WIDTH))],
        out_specs=pl.BlockSpec((TM, POOL_WIDTH), lambda i: (i, 0)),
        out_shape=jax.ShapeDtypeStruct((T_ALL, POOL_WIDTH), BF16),
        compiler_params=_cparams(1),
        name="pool_mix",
    )(hp, hp, hp, pw, ps)


def _merge_kernel(x_ref, mod_ref, yp_ref, ymc_ref, yml_ref, ygc_ref, ygl_ref, sg_ref,
                  wbp_ref, wbm_ref, wbg_ref, wo_ref, lg_ref, lb_ref, rw_ref, rb_ref,
                  x1_ref, u2_ref, ri_ref, rwt_ref, cnt_ref, carry):
    i = pl.program_id(0)

    @pl.when(i == 0)
    def _():
        carry[...] = jnp.zeros_like(carry)

    is_ctx = i < CTX_TILES
    ym = jnp.where(is_ctx, ymc_ref[...], yml_ref[...])
    yg = jnp.where(is_ctx, ygc_ref[...], ygl_ref[...])
    m = (sg_ref[:, 0:D_MODEL].astype(F32) * _dot(yp_ref[...], wbp_ref[...])
         + sg_ref[:, D_MODEL:2 * D_MODEL].astype(F32) * _dot(ym, wbm_ref[...])
         + sg_ref[:, 2 * D_MODEL:3 * D_MODEL].astype(F32) * _dot(yg, wbg_ref[...]))
    y = _dot(m.astype(BF16), wo_ref[...])
    gate1 = mod_ref[0, :, 2 * D_MODEL:3 * D_MODEL]
    x1 = _layer_norm(ALPHA_DN * x_ref[...] + gate1 * y, lg_ref[...], lb_ref[...])
    x1_ref[...] = x1
    sh2 = mod_ref[0, :, 3 * D_MODEL:4 * D_MODEL]
    sc2 = mod_ref[0, :, 4 * D_MODEL:5 * D_MODEL]
    u2 = x1 * (1.0 + sc2) + sh2
    u2_ref[...] = u2

    logits = _dot3(u2, rw_ref[...]) + rb_ref[...]
    lane = lax.broadcasted_iota(I32, (TM, LANES), 1)
    lane_f = lane.astype(F32)
    vals, idxs = [], []
    rest = logits
    for _ in range(TOP_K):
        mx = jnp.max(rest, axis=-1, keepdims=True)
        ix = jnp.min(jnp.where(rest == mx, lane_f, float(LANES)), axis=-1, keepdims=True).astype(I32)
        vals.append(mx)
        idxs.append(ix)
        rest = jnp.where(lane == ix, -jnp.inf, rest)
    es = [jnp.exp(v - vals[0]) for v in vals]
    den = functools.reduce(jnp.add, es)

    sel = functools.reduce(jnp.add, [jnp.where(lane == ix, 1.0, 0.0) for ix in idxs])
    r_i = lax.broadcasted_iota(I32, (TM, TM), 0)
    c_i = lax.broadcasted_iota(I32, (TM, TM), 1)
    below = jnp.where(c_i < r_i, 1.0, 0.0).astype(BF16)
    rank = _dot(below, sel.astype(BF16)) + carry[...]
    carry[...] = carry[...] + jnp.sum(sel, axis=0, keepdims=True)
    cnt_ref[...] = carry[...]

    ri = jnp.zeros((TM, LANES), I32)
    rwt = jnp.zeros((TM, LANES), F32)
    for k in range(TOP_K):
        rk = jnp.sum(jnp.where(lane == idxs[k], rank, 0.0), axis=-1, keepdims=True).astype(I32)
        ri = jnp.where(lane == k, idxs[k], ri)
        ri = jnp.where(lane == TOP_K + k, rk, ri)
        rwt = jnp.where(lane == k, es[k] / den, rwt)
    ri_ref[...] = ri
    rwt_ref[...] = rwt


def _merge(x, mod, yp, ymc, yml, ygc, ygl, sg, wbp, wbm, wbg, wo, lg, lb, rw, rb):
    tok = lambda w: pl.BlockSpec((TM, w), lambda i: (i, 0))
    ctx = lambda w: pl.BlockSpec((TM, w), lambda i: (jnp.minimum(i, CTX_TILES - 1), 0))
    lat = lambda w: pl.BlockSpec((TM, w), lambda i: (jnp.maximum(i - CTX_TILES, 0), 0))
    return pl.pallas_call(
        _merge_kernel,
        grid=(N_TILES,),
        in_specs=[tok(D_MODEL),
                  pl.BlockSpec((1, 1, 6 * D_MODEL), lambda i: (_mod_row(i), 0, 0)),
                  tok(POOL_WIDTH), ctx(512), lat(512), ctx(512), lat(512), tok(3 * D_MODEL),
                  _const_spec((POOL_WIDTH, D_MODEL)), _const_spec((512, D_MODEL)), _const_spec((512, D_MODEL)),
                  _const_spec((D_MODEL, D_MODEL)), _const_spec((1, D_MODEL)), _const_spec((1, D_MODEL)),
                  _const_spec((D_MODEL, LANES)), _const_spec((1, LANES))],
        out_specs=[tok(D_MODEL), tok(D_MODEL), tok(LANES), tok(LANES), _const_spec((1, LANES))],
        out_shape=[jax.ShapeDtypeStruct((T_ALL, D_MODEL), F32), jax.ShapeDtypeStruct((T_ALL, D_MODEL), F32),
                   jax.ShapeDtypeStruct((T_ALL, LANES), I32), jax.ShapeDtypeStruct((T_ALL, LANES), F32),
                   jax.ShapeDtypeStruct((1, LANES), F32)],
        scratch_shapes=[pltpu.VMEM((1, LANES), F32)],
        compiler_params=_cparams(1),
        name="merge_route",
    )(x, mod, yp, ymc, yml, ygc, ygl, sg, wbp, wbm, wbg, wo, lg, lb, rw, rb)


def _row_copy(src_ref, src_row, dst_ref, dst_row, sem):
    return pltpu.make_async_copy(src_ref.at[pl.ds(src_row, 1), :], dst_ref.at[pl.ds(dst_row, 1), :], sem)


ROWS_PER_ISSUE = 4
WAITS_PER_TRIP = 64


def _drain_rows(wait_one, n_rows):
    def trip(_, c):
        for _ in range(WAITS_PER_TRIP):
            wait_one()
        return c

    lax.fori_loop(0, n_rows // WAITS_PER_TRIP, trip, 0)


def _dispatch_kernel(dest_ref, zrow_ref, nu_ref, u_ref, xs_ref, zbuf, sem, zsem):
    i = pl.program_id(0)
    base = i * (TM * TOP_K)

    @pl.when(i == 0)
    def _():
        zbuf[...] = jnp.zeros_like(zbuf)

        def zero_tile(row):
            if not isinstance(row, int):
                row = pl.multiple_of(row, TME)
            return pltpu.make_async_copy(zbuf, xs_ref.at[pl.ds(row, TME), :], zsem)

        for e in range(N_EXPERTS):
            @pl.when(zrow_ref[e] >= 0)
            def _():
                zero_tile(zrow_ref[e]).start()

        def start_tail(j, c):
            zero_tile(j * TME).start()
            return c

        lax.fori_loop(nu_ref[0], NT_E, start_tail, 0)

        for e in range(N_EXPERTS):
            @pl.when(zrow_ref[e] >= 0)
            def _():
                zero_tile(0).wait()

        def wait_tail(j, c):
            zero_tile(0).wait()
            return c

        lax.fori_loop(nu_ref[0], NT_E, wait_tail, 0)

    def issue(g, c):
        for r in range(ROWS_PER_ISSUE):
            t = g * ROWS_PER_ISSUE + r
            for k in range(TOP_K):
                _row_copy(u_ref, t, xs_ref, dest_ref[base + t * TOP_K + k], sem).start(priority=k % 2)
        return c

    lax.fori_loop(0, TM // ROWS_PER_ISSUE, issue, 0)
    _drain_rows(lambda: _row_copy(u_ref, 0, xs_ref, 0, sem).wait(), TM * TOP_K)


def _dispatch(plan, u2):
    grid_spec = pltpu.PrefetchScalarGridSpec(
        num_scalar_prefetch=3,
        grid=(N_TILES,),
        in_specs=[pl.BlockSpec((TM, D_MODEL), lambda i, *_: (i, 0))],
        out_specs=pl.BlockSpec(memory_space=pl.ANY),
        scratch_shapes=[pltpu.VMEM((TME, D_MODEL), F32), pltpu.SemaphoreType.DMA(()), pltpu.SemaphoreType.DMA(())],
    )
    return pl.pallas_call(
        _dispatch_kernel,
        grid_spec=grid_spec,
        out_shape=jax.ShapeDtypeStruct((P_ROWS, D_MODEL), F32),
        compiler_params=_cparams(1),
        name="moe_dispatch",
    )(plan["dest"], plan["zero_row"], plan["n_used"], u2)


def _expert_kernel(layer, te_ref, nu_ref, slot_ref, first_ref, next_ref,
                   xs_ref, wgu_hbm, bgu_ref, wd_hbm, bd_ref, ys_ref, wgu_f, wd_f, wgu_s, wd_s, sems):
    i = pl.program_id(0)

    def weight_copies(e, s):
        return (pltpu.make_async_copy(wgu_hbm.at[layer, e], wgu_f.at[s], sems.at[0, s]),
                pltpu.make_async_copy(wd_hbm.at[layer, e], wd_f.at[s], sems.at[1, s]))

    @pl.when(first_ref[i] == 1)
    def _():
        s = slot_ref[i]

        @pl.when(i == 0)
        def _():
            for cp in weight_copies(te_ref[i], s):
                cp.start()

        for cp in weight_copies(te_ref[i], s):
            cp.wait()

        @pl.when(next_ref[i] >= 0)
        def _():
            for cp in weight_copies(next_ref[i], 1 - s):
                cp.start()

        wgu_s[...] = wgu_f[s].astype(BF16)
        wd_s[...] = wd_f[s].astype(BF16)

    @pl.when(i < nu_ref[0])
    def _():
        h = _dot(xs_ref[...].astype(BF16), wgu_s[...]) + bgu_ref[0, 0]
        glu = jnp.minimum(h[:, 0:D_FF], SWIGLU_LIMIT)
        lin = jnp.clip(h[:, D_FF:2 * D_FF], -SWIGLU_LIMIT, SWIGLU_LIMIT)
        a = glu * _sigmoid(SWIGLU_ALPHA * glu) * (lin + 1.0)
        ys_ref[...] = _dot(a.astype(BF16), wd_s[...]) + bd_ref[0, 0]

    @pl.when(i >= nu_ref[0])
    def _():
        ys_ref[...] = jnp.zeros_like(ys_ref)


def _experts(layer, plan, xs, w_gate_up, b_gate_up, w_down, b_down):
    row = lambda i, te, nu, *_: (jnp.minimum(i, nu[0] - 1), 0)
    bias = lambda i, te, *_: (layer, te[i], 0, 0)
    grid_spec = pltpu.PrefetchScalarGridSpec(
        num_scalar_prefetch=5,
        grid=(NT_E,),
        in_specs=[pl.BlockSpec((TME, D_MODEL), row),
                  pl.BlockSpec(memory_space=pl.ANY),
                  pl.BlockSpec((1, 1, 1, 2 * D_FF), bias),
                  pl.BlockSpec(memory_space=pl.ANY),
                  pl.BlockSpec((1, 1, 1, D_MODEL), bias)],
        out_specs=pl.BlockSpec((TME, D_MODEL), lambda i, *_: (i, 0)),
        scratch_shapes=[pltpu.VMEM((2, D_MODEL, 2 * D_FF), F32), pltpu.VMEM((2, D_FF, D_MODEL), F32),
                        pltpu.VMEM((D_MODEL, 2 * D_FF), BF16), pltpu.VMEM((D_FF, D_MODEL), BF16),
                        pltpu.SemaphoreType.DMA((2, 2))],
    )
    return pl.pallas_call(
        functools.partial(_expert_kernel, layer),
        grid_spec=grid_spec,
        out_shape=jax.ShapeDtypeStruct((P_ROWS, D_MODEL), F32),
        compiler_params=_cparams(1),
        name="moe_experts",
    )(plan["tile_expert"], plan["n_used"], plan["tile_slot"], plan["tile_first"], plan["tile_next"], xs, w_gate_up,
      b_gate_up.reshape(DEPTH, N_EXPERTS, 1, 2 * D_FF), w_down, b_down.reshape(DEPTH, N_EXPERTS, 1, D_MODEL))


def _combine_kernel(dest_ref, ys_ref, rwt_ref, x1_ref, mod_ref, lg_ref, lb_ref, o_ref, buf, sem):
    base = pl.program_id(0) * (TM * TOP_K)

    def issue(g, c):
        for r in range(ROWS_PER_ISSUE):
            t = g * ROWS_PER_ISSUE + r
            for k in range(TOP_K):
                _row_copy(ys_ref, dest_ref[base + t * TOP_K + k], buf.at[k], t, sem).start(priority=k % 2)
        return c

    lax.fori_loop(0, TM // ROWS_PER_ISSUE, issue, 0)
    _drain_rows(lambda: _row_copy(ys_ref, 0, buf.at[0], 0, sem).wait(), TM * TOP_K)

    moe = functools.reduce(jnp.add, [rwt_ref[:, k:k + 1] * buf[k] for k in range(TOP_K)])
    gate2 = mod_ref[0, :, 5 * D_MODEL:6 * D_MODEL]
    o_ref[...] = _layer_norm(ALPHA_DN * x1_ref[...] + gate2 * moe, lg_ref[...], lb_ref[...])


def _combine(dest, ys, rwt, x1, mod, lg, lb):
    grid_spec = pltpu.PrefetchScalarGridSpec(
        num_scalar_prefetch=1,
        grid=(N_TILES,),
        in_specs=[pl.BlockSpec(memory_space=pl.ANY),
                  pl.BlockSpec((TM, LANES), lambda i, d: (i, 0)),
                  pl.BlockSpec((TM, D_MODEL), lambda i, d: (i, 0)),
                  pl.BlockSpec((1, 1, 6 * D_MODEL), lambda i, d: (_mod_row(i), 0, 0)),
                  pl.BlockSpec((1, D_MODEL), lambda i, d: (0, 0)),
                  pl.BlockSpec((1, D_MODEL), lambda i, d: (0, 0))],
        out_specs=pl.BlockSpec((TM, D_MODEL), lambda i, d: (i, 0)),
        scratch_shapes=[pltpu.VMEM((TOP_K, TM, D_MODEL), F32), pltpu.SemaphoreType.DMA(())],
    )
    return pl.pallas_call(
        _combine_kernel,
        grid_spec=grid_spec,
        out_shape=jax.ShapeDtypeStruct((T_ALL, D_MODEL), F32),
        compiler_params=_cparams(1),
        name="moe_combine",
    )(dest, ys, rwt, x1, mod, lg, lb)


def _pad_heads(w, n_heads, width):
    rows = w.shape[0]
    w = w.reshape(rows, n_heads, width)
    return jnp.pad(w, ((0, 0), (0, 0), (0, LANES - width))).reshape(rows, n_heads * LANES)


def _dup_heads(w):
    rows = w.shape[0]
    w = w.reshape(rows, GQA_KV, 1, GQA_HD)
    return jnp.concatenate([w, w], axis=2).reshape(rows, GQA_KV * LANES)


def _pack_w_in(w):
    cuts = [0, 512, 896, 1152, 1184, 1696, 1824, 1952, 5024]
    hp, cq, ckv, kr, q, k, v, g = [w[:, a:b] for a, b in zip(cuts[:-1], cuts[1:])]
    packed = jnp.concatenate([hp, cq, ckv, _pad_heads(kr, 1, MLA_ROPE), q, _dup_heads(k), _dup_heads(v), g], axis=1)
    return packed.astype(BF16)


def _pack_mla(wuq, wukv):
    wuq_p = _pad_heads(wuq, MLA_HEADS, MLA_QK).astype(BF16)
    kv = wukv.reshape(MLA_KV_LORA, MLA_HEADS, MLA_NOPE + MLA_V)
    wk_top = _pad_heads(kv[:, :, :MLA_NOPE].reshape(MLA_KV_LORA, MLA_HEADS * MLA_NOPE), MLA_HEADS, MLA_NOPE)
    place = jnp.zeros((LANES, MLA_HEADS, LANES), F32)
    r = jnp.arange(MLA_ROPE)
    place = place.at[r, :, MLA_NOPE + r].set(1.0).reshape(LANES, MLA_HEADS * LANES)
    wk = jnp.concatenate([wk_top, place], axis=0).astype(BF16)
    wv = kv[:, :, MLA_NOPE:].reshape(MLA_KV_LORA, MLA_HEADS * MLA_V).astype(BF16)
    return wuq_p, wk, wv


def _rope_table(dim, lane0):
    quarter = dim // 4
    pos = jnp.arange(S_LAT)
    rows = (pos // GRID_W).astype(F32)
    cols = (pos % GRID_W).astype(F32)
    freqs = jnp.power(ROPE_THETA, -jnp.arange(quarter, dtype=F32) / quarter)
    ang_r = rows[:, None] * freqs[None, :]
    ang_c = cols[:, None] * freqs[None, :]
    zero = jnp.zeros((S_LAT, quarter), F32)
    cos = jnp.concatenate([jnp.cos(ang_r), jnp.cos(ang_r), jnp.cos(ang_c), jnp.cos(ang_c)], axis=1)
    s_up = jnp.concatenate([-jnp.sin(ang_r), zero, -jnp.sin(ang_c), zero], axis=1)
    s_dn = jnp.concatenate([zero, jnp.sin(ang_r), zero, jnp.sin(ang_c)], axis=1)

    def place(t, fill):
        if lane0 is None:
            return jnp.tile(t, (1, LANES // dim))
        full = jnp.full((S_LAT, LANES), fill, F32)
        return full.at[:, lane0:lane0 + dim].set(t)

    planes = jnp.stack([place(cos, 1.0), place(s_up, 0.0), place(s_dn, 0.0)])
    ident = jnp.stack([jnp.ones((TM, LANES), F32), jnp.zeros((TM, LANES), F32), jnp.zeros((TM, LANES), F32)])
    return jnp.concatenate([planes, ident], axis=1)


TILE_LANES = 2 * LANES
ROW_EXPERT, ROW_SLOT, ROW_FIRST, ROW_NEXT, ROW_ZERO, ROW_USED = range(6)


def _plan_kernel(ri_ref, cnt_ref, dest_ref, tiles_ref):
    lane = lax.broadcasted_iota(I32, (1, LANES), 1)
    r_i = lax.broadcasted_iota(I32, (LANES, LANES), 0)
    c_i = lax.broadcasted_iota(I32, (LANES, LANES), 1)
    upto = jnp.where(r_i <= c_i, 1.0, 0.0).astype(BF16)

    def prefix(row):
        return _dot(jnp.broadcast_to(row, (SUBLANES, LANES)).astype(BF16), upto)[0:1, :]

    cnt = cnt_ref[...]
    t_cnt = jnp.floor((cnt + (TME - 1)) * (1.0 / TME))
    t_end = prefix(t_cnt)
    t_start = t_end - t_cnt
    n_used = jnp.max(t_end, axis=-1, keepdims=True)

    tok_lane = lax.broadcasted_iota(I32, (TM, LANES), 1)
    ri = ri_ref[...]
    dest = jnp.zeros((TM, LANES), I32)
    for k in range(TOP_K):
        start_k = jnp.sum(jnp.where(tok_lane == ri[:, k:k + 1], t_start, 0.0), axis=-1, keepdims=True)
        dest_k = start_k.astype(I32) * TME + ri[:, TOP_K + k:TOP_K + k + 1]
        dest = jnp.where(tok_lane == k, dest_k, dest)
    dest_ref[...] = dest

    @pl.when(pl.program_id(0) == 0)
    def _():
        eye = jnp.where(r_i == c_i, 1.0, 0.0).astype(BF16)

        def to_col(row):
            return _dot_t(eye, jnp.broadcast_to(row, (SUBLANES, LANES)).astype(BF16))[:, 0:1]

        e_col = lax.broadcasted_iota(I32, (LANES, 1), 0)
        real_e = e_col < N_EXPERTS
        tile = lax.broadcasted_iota(I32, (LANES, TILE_LANES), 1).astype(F32)
        end_col = to_col(t_end)
        expert = jnp.sum(jnp.where(real_e & (end_col <= tile), 1.0, 0.0), axis=0, keepdims=True)
        last = jnp.sum(jnp.where(real_e & (end_col <= n_used - 1.0), 1.0, 0.0), axis=0, keepdims=True)
        expert = jnp.minimum(expert, last)
        mine = e_col.astype(F32) == expert

        def per_tile(col):
            return jnp.sum(jnp.where(mine, col, 0.0), axis=0, keepdims=True)

        has = t_cnt > 0.0
        ordinal = prefix(jnp.where(has, 1.0, 0.0)) - 1.0
        ord_col = to_col(ordinal)
        slot = per_tile(ord_col - 2.0 * jnp.floor(ord_col * 0.5))
        tile_row = tile[0:1, :]
        first = jnp.where((tile_row < n_used) & (tile_row == per_tile(to_col(t_start))), 1.0, 0.0)
        later = jnp.where(has & (lane > e_col), lane.astype(F32), float(LANES))
        nxt_col = jnp.min(later, axis=-1, keepdims=True)
        nxt_col = jnp.where(nxt_col == float(LANES), -1.0, nxt_col)
        nxt = per_tile(nxt_col)
        zero_row = jnp.where(has, (t_end - 1.0) * TME, -1.0)
        zero_row = jnp.concatenate([zero_row, jnp.full((1, TILE_LANES - LANES), -1.0, F32)], axis=1)
        used = jnp.broadcast_to(n_used, (1, TILE_LANES))
        pad = jnp.zeros((SUBLANES - 6, TILE_LANES), F32)
        tiles_ref[...] = jnp.concatenate([expert, slot, first, nxt, zero_row, used, pad], axis=0).astype(I32)


def _routing_plan(ri, counts):
    dest, tiles = pl.pallas_call(
        _plan_kernel,
        grid=(N_TILES,),
        in_specs=[pl.BlockSpec((TM, LANES), lambda i: (i, 0)), _const_spec((1, LANES))],
        out_specs=[pl.BlockSpec((TM, LANES), lambda i: (i, 0)), _const_spec((SUBLANES, TILE_LANES))],
        out_shape=[jax.ShapeDtypeStruct((T_ALL, LANES), I32), jax.ShapeDtypeStruct((SUBLANES, TILE_LANES), I32)],
        compiler_params=_cparams(1),
        name="moe_plan",
    )(ri, counts)
    return {"dest": dest[:, 0:TOP_K].reshape(-1), "tile_expert": tiles[ROW_EXPERT, :NT_E],
            "n_used": tiles[ROW_USED, 0:1], "tile_slot": tiles[ROW_SLOT, :NT_E], "tile_first": tiles[ROW_FIRST, :NT_E],
            "tile_next": tiles[ROW_NEXT, :NT_E], "zero_row": tiles[ROW_ZERO, :N_EXPERTS]}


def kernel(x_prompt, x_sample, c, cache_mla_ckv, cache_mla_krope, cache_gqa_k, cache_gqa_v, c_ctx, w_ada, b_ada, w_in, mla_q_norm, mla_kv_norm, w_mla_uq, w_mla_ukv, gqa_sink, pool_w, pool_scale, w_branch_pool, w_branch_mla, w_branch_gqa, w_out, ln1_g, ln1_b, ln2_g, ln2_b, router_w, router_b, w_gate_up, b_gate_up, w_down, b_down):
    x = jnp.concatenate([x_prompt.reshape(T_CTX, D_MODEL), x_sample.reshape(T_LAT, D_MODEL)], axis=0)
    cond8 = jnp.concatenate([c_ctx[None, :], c, jnp.zeros((SUBLANES - 1 - B_LAT, D_MODEL), F32)], axis=0)
    mod_all = _ada_params(cond8, w_ada, b_ada)

    tab_g = _rope_table(GQA_HD, None)
    tab_q = _rope_table(MLA_ROPE, MLA_NOPE)
    tab_k = _rope_table(MLA_ROPE, 0)

    outs = {"ckv": [], "kr": [], "gk": [], "gv": []}
    for l in range(DEPTH):
        mod = mod_all[l].reshape(SUBLANES, 1, 6 * D_MODEL)
        wp = _pack_w_in(w_in[l])
        wuq_p, wk, wv = _pack_mla(w_mla_uq[l], w_mla_ukv[l])

        (hp, qm, ckv, kr, km, vm, qg, kg, vg, nk, nv, sg) = _in_proj(
            x, mod, wp, mla_q_norm[l][None, :], mla_kv_norm[l][None, :], wuq_p, wk, wv, tab_g, tab_q, tab_k)
        outs["ckv"].append(ckv[:T_CTX].reshape(B_CTX, S_CTX, MLA_KV_LORA))
        outs["kr"].append(kr[:T_CTX, :MLA_ROPE].reshape(B_CTX, S_CTX, MLA_ROPE))
        outs["gk"].append(nk[:T_CTX].reshape(B_CTX, S_CTX, GQA_KV, GQA_HD))
        outs["gv"].append(nv[:T_CTX].reshape(B_CTX, S_CTX, GQA_KV, GQA_HD))

        ckv_c = cache_mla_ckv[:, l].reshape(B_LAT * PAST_LEN, MLA_KV_LORA)
        kr_c = jnp.pad(cache_mla_krope[:, l].reshape(B_LAT * PAST_LEN, MLA_ROPE), ((0, 0), (0, LANES - MLA_ROPE)))
        km_c, vm_c = _expand_cache(ckv_c, kr_c, wk, wv)
        gk_c = _dup_heads(cache_gqa_k[:, l].reshape(B_LAT * PAST_LEN, GQA_KV * GQA_HD)).astype(BF16)
        gv_c = _dup_heads(cache_gqa_v[:, l].reshape(B_LAT * PAST_LEN, GQA_KV * GQA_HD)).astype(BF16)

        yp = _pool(hp, pool_w[l].astype(BF16), pool_scale[l][None, :])
        ymc = _mla_ctx(qm, km, vm)
        yml = _mla_lat(qm, km, vm, km_c, vm_c)
        ygc = _gqa_ctx(gqa_sink[l], qg, kg, vg)
        ygl = _gqa_lat(gqa_sink[l], qg, kg, vg, gk_c, gv_c)

        rw = jnp.pad(router_w[l], ((0, 0), (0, LANES - N_EXPERTS)))
        rb = jnp.concatenate([router_b[l], jnp.full((LANES - N_EXPERTS,), -jnp.inf, F32)])[None, :]
        x1, u2, ri, rwt, counts = _merge(
            x, mod, yp, ymc, yml, ygc, ygl, sg,
            w_branch_pool[l].astype(BF16), w_branch_mla[l].astype(BF16), w_branch_gqa[l].astype(BF16),
            w_out[l].astype(BF16), ln1_g[l][None, :], ln1_b[l][None, :], rw, rb)

        plan = _routing_plan(ri, counts)
        xs = _dispatch(plan, u2)
        ys = _experts(l, plan, xs, w_gate_up, b_gate_up, w_down, b_down)
        x = _combine(plan["dest"], ys, rwt, x1, mod, ln2_g[l][None, :], ln2_b[l][None, :])

    y_prompt = x[:T_CTX].reshape(B_CTX, S_CTX, D_MODEL)
    y_sample = x[T_CTX:].reshape(B_LAT, S_LAT, D_MODEL)
    return (y_prompt, y_sample, jnp.stack(outs["ckv"], axis=1), jnp.stack(outs["kr"], axis=1),
            jnp.stack(outs["gk"], axis=1), jnp.stack(outs["gv"], axis=1))
```

```python
import functools
import math

import jax
import jax.numpy as jnp
from jax import lax
from jax.experimental import pallas as pl
from jax.experimental.pallas import tpu as pltpu

F32 = jnp.float32
BF16 = jnp.bfloat16
I32 = jnp.int32

D_MODEL = 1024
DEPTH = 2
B_CTX, S_CTX = 16, 256
B_LAT, S_LAT = 4, 1024
PAST_LEN = 512
GRID_W = 64
ROPE_THETA = 10000.0
ALPHA_DN = (2 * DEPTH) ** 0.25
LN_EPS = 1e-5
RMS_EPS = 1e-6
NEG_INF = -1e30
POOL_WINDOWS = (2, 4, 8, 16)
POOL_GROUP = 128
POOL_WIDTH = 512
MLA_HEADS = 8
MLA_NOPE = 64
MLA_ROPE = 32
MLA_V = 64
MLA_QK = MLA_NOPE + MLA_ROPE
MLA_Q_LORA = 384
MLA_KV_LORA = 256
MLA_SCALE = 1.0 / math.sqrt(MLA_QK)
GQA_HEADS = 8
GQA_KV = 2
GQA_HD = 64
GQA_GROUP = GQA_HEADS // GQA_KV
GQA_SCALE = 1.0 / math.sqrt(GQA_HD)
WBLK = 128
N_EXPERTS = 32
TOP_K = 4
D_FF = 1024
SWIGLU_LIMIT = 7.0
SWIGLU_ALPHA = 1.702

LANES = 128
SUBLANES = 8
VMEM_LIMIT = 56 * 1024 * 1024

T_CTX = B_CTX * S_CTX
T_LAT = B_LAT * S_LAT
T_ALL = T_CTX + T_LAT
TM = 256
N_TILES = T_ALL // TM
CTX_TILES = T_CTX // TM
LAT_TILES_PER_SEQ = S_LAT // TM
HALO = 16
MLA_LAT_TQ = 512

C_HP = 0
C_CQ = C_HP + POOL_WIDTH
C_CKV = C_CQ + MLA_Q_LORA
C_KR = C_CKV + MLA_KV_LORA
C_GQ = C_KR + LANES
C_GK = C_GQ + GQA_HEADS * GQA_HD
C_GV = C_GK + GQA_KV * LANES
C_G = C_GV + GQA_KV * LANES
C_END = C_G + 3 * D_MODEL

TME = 256
N_PAIRS = T_ALL * TOP_K
NT_E = N_PAIRS // TME + N_EXPERTS
P_ROWS = NT_E * TME


def _cparams(n_axes):
    return pltpu.CompilerParams(dimension_semantics=("arbitrary",) * n_axes, vmem_limit_bytes=VMEM_LIMIT)


def _dot(a, b):
    return jnp.dot(a, b, preferred_element_type=F32)


def _dot_t(a, b):
    return lax.dot_general(a, b, (((1,), (1,)), ((), ())), preferred_element_type=F32)


def _split(x):
    hi = x.astype(BF16)
    lo = (x - hi.astype(F32)).astype(BF16)
    return hi, lo


def _dot3(a, b):
    a_hi, a_lo = _split(a)
    b_hi, b_lo = _split(b)
    return _dot(a_hi, b_hi) + _dot(a_hi, b_lo) + _dot(a_lo, b_hi)


def _sigmoid(x):
    return 1.0 / (1.0 + jnp.exp(-x))


def _layer_norm(z, g, b):
    mu = jnp.mean(z, axis=-1, keepdims=True)
    zc = z - mu
    var = jnp.mean(zc * zc, axis=-1, keepdims=True)
    return zc * lax.rsqrt(var + LN_EPS) * g + b


def _mod_row(i):
    return jnp.where(i < CTX_TILES, 0, 1 + (i - CTX_TILES) // LAT_TILES_PER_SEQ)


def _pos_block(i):
    return jnp.where(i < CTX_TILES, LAT_TILES_PER_SEQ, (i - CTX_TILES) % LAT_TILES_PER_SEQ)


def _const_spec(shape):
    nd = len(shape)
    return pl.BlockSpec(shape, lambda *_: (0,) * nd)


def _ada_kernel(c_ref, w_ref, b_ref, o_ref):
    c = c_ref[...]
    s = c * _sigmoid(c)
    o_ref[0] = _dot3(s, w_ref[0]) + b_ref[0]


def _ada_params(cond8, w_ada, b_ada):
    n_col = 6 * D_MODEL
    blk = 1024
    return pl.pallas_call(
        _ada_kernel,
        grid=(DEPTH, n_col // blk),
        in_specs=[
            pl.BlockSpec((SUBLANES, D_MODEL), lambda l, j: (0, 0)),
            pl.BlockSpec((1, D_MODEL, blk), lambda l, j: (l, 0, j)),
            pl.BlockSpec((1, 1, blk), lambda l, j: (l, 0, j)),
        ],
        out_specs=pl.BlockSpec((1, SUBLANES, blk), lambda l, j: (l, 0, j)),
        out_shape=jax.ShapeDtypeStruct((DEPTH, SUBLANES, n_col), F32),
        compiler_params=_cparams(2),
        name="ada_params",
    )(cond8, w_ada, b_ada.reshape(DEPTH, 1, n_col))


def _rope(x, tab_ref, shift):
    return (x * tab_ref[0]
            + pltpu.roll(x, LANES - shift, 1) * tab_ref[1]
            + pltpu.roll(x, shift, 1) * tab_ref[2])


def _rms(h, g):
    return h * lax.rsqrt(jnp.mean(h * h, axis=-1, keepdims=True) + RMS_EPS) * g


def _in_kernel(x_ref, mod_ref, wp_ref, qn_ref, kvn_ref, wuq_ref, wk_ref, wv_ref, tg_ref, tq_ref, tk_ref,
               hp_ref, qm_ref, ckv_ref, kr_ref, km_ref, vm_ref, qg_ref, kg_ref, vg_ref, nk_ref, nv_ref, sg_ref):
    x = x_ref[...]
    sh = mod_ref[0, :, 0:D_MODEL]
    sc = mod_ref[0, :, D_MODEL:2 * D_MODEL]
    u = (x * (1.0 + sc) + sh).astype(BF16)

    def proj(lo, hi):
        return _dot(u, wp_ref[:, lo:hi])

    hp_ref[...] = proj(C_HP, C_CQ)

    qn = _rms(proj(C_CQ, C_CKV), qn_ref[...]).astype(BF16)
    q = _dot(qn, wuq_ref[...]) * MLA_SCALE
    for h in range(MLA_HEADS):
        qm_ref[:, h * LANES:(h + 1) * LANES] = _rope(q[:, h * LANES:(h + 1) * LANES], tq_ref, MLA_ROPE // 4).astype(BF16)

    ckv = _rms(proj(C_CKV, C_KR), kvn_ref[...])
    ckv_ref[...] = ckv
    kr = _rope(proj(C_KR, C_GQ), tk_ref, MLA_ROPE // 4)
    kr_ref[...] = kr
    ckv_b = ckv.astype(BF16)
    km_ref[...] = (_dot(ckv_b, wk_ref[0:MLA_KV_LORA, :]) + _dot(kr.astype(BF16), wk_ref[MLA_KV_LORA:, :])).astype(BF16)
    vm_ref[...] = _dot(ckv_b, wv_ref[...]).astype(BF16)

    hq = proj(C_GQ, C_GK) * GQA_SCALE
    for j in range(GQA_HEADS * GQA_HD // LANES):
        qg_ref[:, j * LANES:(j + 1) * LANES] = _rope(hq[:, j * LANES:(j + 1) * LANES], tg_ref, GQA_HD // 4).astype(BF16)
    hk = proj(C_GK, C_GV)
    hv = proj(C_GV, C_G)
    for g in range(GQA_KV):
        kg_ref[:, g * LANES:(g + 1) * LANES] = _rope(hk[:, g * LANES:(g + 1) * LANES], tg_ref, GQA_HD // 4).astype(BF16)
    vg_ref[...] = hv.astype(BF16)
    low_half = lax.broadcasted_iota(I32, (TM, LANES), 1) < GQA_HD
    nk_ref[...] = jnp.where(low_half, hk[:, 0:LANES], hk[:, LANES:2 * LANES])
    nv_ref[...] = jnp.where(low_half, hv[:, 0:LANES], hv[:, LANES:2 * LANES])

    for j in range(3):
        sg_ref[:, j * D_MODEL:(j + 1) * D_MODEL] = _sigmoid(proj(C_G + j * D_MODEL, C_G + (j + 1) * D_MODEL)).astype(BF16)


def _in_proj(x, mod, wp, qn, kvn, wuq, wk, wv, tab_g, tab_q, tab_k):
    tok = lambda w: pl.BlockSpec((TM, w), lambda i: (i, 0))
    tab = pl.BlockSpec((3, TM, LANES), lambda i: (0, _pos_block(i), 0))
    out_widths = [(POOL_WIDTH, F32), (MLA_HEADS * LANES, BF16), (MLA_KV_LORA, F32), (LANES, F32),
                  (MLA_HEADS * LANES, BF16), (MLA_HEADS * MLA_V, BF16), (GQA_HEADS * GQA_HD, BF16),
                  (GQA_KV * LANES, BF16), (GQA_KV * LANES, BF16), (GQA_KV * GQA_HD, F32), (GQA_KV * GQA_HD, F32),
                  (3 * D_MODEL, BF16)]
    return pl.pallas_call(
        _in_kernel,
        grid=(N_TILES,),
        in_specs=[
            tok(D_MODEL),
            pl.BlockSpec((1, 1, 6 * D_MODEL), lambda i: (_mod_row(i), 0, 0)),
            _const_spec((D_MODEL, C_END)),
            _const_spec((1, MLA_Q_LORA)),
            _const_spec((1, MLA_KV_LORA)),
            _const_spec((MLA_Q_LORA, MLA_HEADS * LANES)),
            _const_spec((MLA_KV_LORA + LANES, MLA_HEADS * LANES)),
            _const_spec((MLA_KV_LORA, MLA_HEADS * MLA_V)),
            tab, tab, tab,
        ],
        out_specs=[tok(w) for w, _ in out_widths],
        out_shape=[jax.ShapeDtypeStruct((T_ALL, w), dt) for w, dt in out_widths],
        compiler_params=_cparams(1),
        name="in_proj",
    )(x, mod, wp, qn, kvn, wuq, wk, wv, tab_g, tab_q, tab_k)


def _expand_kernel(ckv_ref, kr_ref, wk_ref, wv_ref, km_ref, vm_ref):
    ckv_b = ckv_ref[...].astype(BF16)
    km_ref[...] = (_dot(ckv_b, wk_ref[0:MLA_KV_LORA, :]) + _dot(kr_ref[...].astype(BF16), wk_ref[MLA_KV_LORA:, :])).astype(BF16)
    vm_ref[...] = _dot(ckv_b, wv_ref[...]).astype(BF16)


def _expand_cache(ckv, kr_pad, wk, wv):
    rows = ckv.shape[0]
    blk = PAST_LEN
    return pl.pallas_call(
        _expand_kernel,
        grid=(rows // blk,),
        in_specs=[
            pl.BlockSpec((blk, MLA_KV_LORA), lambda i: (i, 0)),
            pl.BlockSpec((blk, LANES), lambda i: (i, 0)),
            _const_spec((MLA_KV_LORA + LANES, MLA_HEADS * LANES)),
            _const_spec((MLA_KV_LORA, MLA_HEADS * MLA_V)),
        ],
        out_specs=[pl.BlockSpec((blk, MLA_HEADS * LANES), lambda i: (i, 0)),
                   pl.BlockSpec((blk, MLA_HEADS * MLA_V), lambda i: (i, 0))],
        out_shape=[jax.ShapeDtypeStruct((rows, MLA_HEADS * LANES), BF16),
                   jax.ShapeDtypeStruct((rows, MLA_HEADS * MLA_V), BF16)],
        compiler_params=_cparams(1),
        name="mla_expand_cache",
    )(ckv, kr_pad, wk, wv)


def _lane_chunks(x):
    return [x[:, c:c + LANES] for c in range(0, x.shape[1], LANES)]


def _softmax_sets(ss, extra=None):
    m = jnp.max(functools.reduce(jnp.maximum, [c for s in ss for c in _lane_chunks(s)]), axis=-1, keepdims=True)
    if extra is not None:
        m = jnp.maximum(m, extra)
    ps = [jnp.exp(s - m) for s in ss]
    den = jnp.sum(functools.reduce(jnp.add, [c for p in ps for c in _lane_chunks(p)]), axis=-1, keepdims=True)
    if extra is not None:
        den = den + jnp.exp(extra - m)
    return ps, den


def _mla_kernel(n_sets, q_ref, *refs):
    k_refs = refs[:n_sets]
    v_refs = refs[n_sets:2 * n_sets]
    o_ref = refs[2 * n_sets]
    rows = q_ref.shape[0]
    low_half = lax.broadcasted_iota(I32, (rows, LANES), 1) < MLA_V
    for j in range(MLA_HEADS // 2):
        outs = []
        for h in (2 * j, 2 * j + 1):
            qh = q_ref[:, h * LANES:(h + 1) * LANES]
            ps, den = _softmax_sets([_dot_t(qh, k[:, h * LANES:(h + 1) * LANES]) for k in k_refs])
            o = functools.reduce(jnp.add, [_dot(p.astype(BF16), v[:, j * LANES:(j + 1) * LANES]) for p, v in zip(ps, v_refs)])
            outs.append(o / den)
        o_ref[:, j * LANES:(j + 1) * LANES] = jnp.where(low_half, outs[0], outs[1]).astype(BF16)


def _mla_ctx(qm, km, vm):
    blk = S_CTX
    return pl.pallas_call(
        functools.partial(_mla_kernel, 1),
        grid=(B_CTX,),
        in_specs=[pl.BlockSpec((blk, MLA_HEADS * LANES), lambda b: (b, 0)),
                  pl.BlockSpec((blk, MLA_HEADS * LANES), lambda b: (b, 0)),
                  pl.BlockSpec((blk, MLA_HEADS * MLA_V), lambda b: (b, 0))],
        out_specs=pl.BlockSpec((blk, MLA_HEADS * MLA_V), lambda b: (b, 0)),
        out_shape=jax.ShapeDtypeStruct((T_CTX, MLA_HEADS * MLA_V), BF16),
        compiler_params=_cparams(1),
        name="mla_attn_ctx",
    )(qm, km, vm)


def _mla_lat(qm, km, vm, km_c, vm_c):
    lat0 = T_CTX // S_LAT
    tq = MLA_LAT_TQ
    n_q = S_LAT // tq
    return pl.pallas_call(
        functools.partial(_mla_kernel, 2),
        grid=(B_LAT, n_q),
        in_specs=[pl.BlockSpec((tq, MLA_HEADS * LANES), lambda b, t: (T_CTX // tq + b * n_q + t, 0)),
                  pl.BlockSpec((PAST_LEN, MLA_HEADS * LANES), lambda b, t: (b, 0)),
                  pl.BlockSpec((S_LAT, MLA_HEADS * LANES), lambda b, t: (lat0 + b, 0)),
                  pl.BlockSpec((PAST_LEN, MLA_HEADS * MLA_V), lambda b, t: (b, 0)),
                  pl.BlockSpec((S_LAT, MLA_HEADS * MLA_V), lambda b, t: (lat0 + b, 0))],
        out_specs=pl.BlockSpec((tq, MLA_HEADS * MLA_V), lambda b, t: (b * n_q + t, 0)),
        out_shape=jax.ShapeDtypeStruct((T_LAT, MLA_HEADS * MLA_V), BF16),
        compiler_params=_cparams(2),
        name="mla_attn_lat",
    )(qm, km_c, km, vm_c, vm)


def _gqa_kernel(band, sink_ref, q_ref, *refs):
    n_sets = 4 if band else 1
    k_refs = refs[:n_sets]
    v_refs = refs[n_sets:2 * n_sets]
    o_ref = refs[2 * n_sets]
    rows = q_ref.shape[0]
    stack = 2 if band else 1
    low_half = lax.broadcasted_iota(I32, (rows, LANES), 1) < GQA_HD
    masks = [None] * n_sets
    if band:
        n = pl.program_id(1)
        n_blk = pl.num_programs(1)
        qi = lax.broadcasted_iota(I32, (stack * rows, WBLK), 0) % rows
        kj = lax.broadcasted_iota(I32, (stack * rows, WBLK), 1)
        masks[0] = kj >= qi + jnp.where(n > 0, 0, WBLK)
        masks[2] = kj <= qi - jnp.where(n < n_blk - 1, 0, WBLK)
    zero = jnp.zeros((rows, LANES), BF16)
    for j in range(GQA_HEADS // 2):
        g = j // (GQA_GROUP // 2)
        blk = q_ref[:, j * LANES:(j + 1) * LANES]
        q_heads = [jnp.where(low_half, blk, zero), jnp.where(low_half, zero, blk)]
        outs = []
        for h0 in range(0, 2, stack):
            qs = jnp.concatenate(q_heads[h0:h0 + stack], axis=0)
            sink = jnp.concatenate([jnp.full((rows, 1), sink_ref[2 * j + h0 + i], F32) for i in range(stack)], axis=0)
            ss = []
            for k, msk in zip(k_refs, masks):
                s = _dot_t(qs, k[:, g * LANES:(g + 1) * LANES])
                ss.append(s if msk is None else jnp.where(msk, s, NEG_INF))
            ps, den = _softmax_sets(ss, sink)
            o = functools.reduce(jnp.add, [_dot(p.astype(BF16), v[:, g * LANES:(g + 1) * LANES]) for p, v in zip(ps, v_refs)])
            o = o / den
            outs += [o[i * rows:(i + 1) * rows] for i in range(stack)]
        o_ref[:, j * LANES:(j + 1) * LANES] = jnp.where(low_half, outs[0], outs[1]).astype(BF16)


def _gqa_ctx(sink, qg, kg, vg):
    blk = S_CTX
    grid_spec = pltpu.PrefetchScalarGridSpec(
        num_scalar_prefetch=1,
        grid=(B_CTX,),
        in_specs=[pl.BlockSpec((blk, GQA_HEADS * GQA_HD), lambda b, s: (b, 0)),
                  pl.BlockSpec((blk, GQA_KV * LANES), lambda b, s: (b, 0)),
                  pl.BlockSpec((blk, GQA_KV * LANES), lambda b, s: (b, 0))],
        out_specs=pl.BlockSpec((blk, GQA_HEADS * GQA_HD), lambda b, s: (b, 0)),
    )
    return pl.pallas_call(
        functools.partial(_gqa_kernel, False),
        grid_spec=grid_spec,
        out_shape=jax.ShapeDtypeStruct((T_CTX, GQA_HEADS * GQA_HD), BF16),
        compiler_params=_cparams(1),
        name="gqa_attn_ctx",
    )(sink, qg, kg, vg)


def _gqa_lat(sink, qg, kg, vg, kg_c, vg_c):
    nb = S_LAT // WBLK
    first = T_CTX // WBLK

    def blk(off):
        def index_map(b, n, s):
            return (first + b * nb + jnp.clip(n + off, 0, nb - 1), 0)
        return index_map

    kv_w = GQA_KV * LANES
    grid_spec = pltpu.PrefetchScalarGridSpec(
        num_scalar_prefetch=1,
        grid=(B_LAT, nb),
        in_specs=[pl.BlockSpec((WBLK, GQA_HEADS * GQA_HD), blk(0)),
                  pl.BlockSpec((WBLK, kv_w), blk(-1)), pl.BlockSpec((WBLK, kv_w), blk(0)), pl.BlockSpec((WBLK, kv_w), blk(1)),
                  pl.BlockSpec((PAST_LEN, kv_w), lambda b, n, s: (b, 0)),
                  pl.BlockSpec((WBLK, kv_w), blk(-1)), pl.BlockSpec((WBLK, kv_w), blk(0)), pl.BlockSpec((WBLK, kv_w), blk(1)),
                  pl.BlockSpec((PAST_LEN, kv_w), lambda b, n, s: (b, 0))],
        out_specs=pl.BlockSpec((WBLK, GQA_HEADS * GQA_HD), lambda b, n, s: (b * nb + n, 0)),
    )
    return pl.pallas_call(
        functools.partial(_gqa_kernel, True),
        grid_spec=grid_spec,
        out_shape=jax.ShapeDtypeStruct((T_LAT, GQA_HEADS * GQA_HD), BF16),
        compiler_params=_cparams(2),
        name="gqa_attn_lat",
    )(sink, qg, kg, kg, kg, kg_c, vg, vg, vg, vg_c)


def _pool_kernel(hp_p, hp_c, hp_n, pw_ref, ps_ref, o_ref):
    i = pl.program_id(0)
    is_ctx = i < CTX_TILES
    t4 = (i - CTX_TILES) % LAT_TILES_PER_SEQ
    seq_len = jnp.where(is_ctx, S_CTX, S_LAT)
    base = jnp.where(is_ctx, 0, t4 * TM)

    n_keys = TM + 2 * HALO
    first_key = jnp.where(jnp.logical_or(is_ctx, t4 == 0), HALO, 0)
    end_key = jnp.where(jnp.logical_or(is_ctx, t4 == LAT_TILES_PER_SEQ - 1), HALO + TM, n_keys)
    qi = lax.broadcasted_iota(I32, (TM, n_keys), 0)
    kj = lax.broadcasted_iota(I32, (TM, n_keys), 1)
    rel = kj - HALO - qi
    key_ok = (kj >= first_key) & (kj < end_key)

    cur = hp_c[...]
    keys = jnp.concatenate([hp_p[TM - HALO:TM, :], cur, hp_n[0:HALO, :]], axis=0)
    k_hi, k_lo = _split(keys)
    qpos = base + lax.broadcasted_iota(I32, (TM, 1), 0)
    for g, w in enumerate(POOL_WINDOWS):
        a = jnp.where((rel >= -(w // 2)) & (rel <= w // 2 - 1) & key_ok, 1.0, 0.0).astype(BF16)
        cols = slice(g * POOL_GROUP, (g + 1) * POOL_GROUP)
        s = _dot(a, k_hi[:, cols]) + _dot(a, k_lo[:, cols])
        lo = jnp.maximum(qpos - w // 2, 0)
        hi = jnp.minimum(qpos + w // 2 - 1, seq_len - 1)
        cnt = (hi - lo + 1).astype(F32)
        d = s / cnt - cur[:, cols]
        y = _dot(d.astype(BF16), pw_ref[g]) * ps_ref[:, cols]
        o_ref[:, cols] = y.astype(BF16)


def _pool(hp, pw, ps):
    return pl.pallas_call(
        _pool_kernel,
        grid=(N_TILES,),
        in_specs=[pl.BlockSpec((TM, POOL_WIDTH), lambda i: (jnp.maximum(i - 1, 0), 0)),
                  pl.BlockSpec((TM, POOL_WIDTH), lambda i: (i, 0)),
                  pl.BlockSpec((TM, POOL_WIDTH), lambda i: (jnp.minimum(i + 1, N_TILES - 1), 0)),
                  _const_spec((len(POOL_WINDOWS), POOL_GROUP, POOL_GROUP)),
                  _const_spec((1, POOL_WIDTH))],
        out_specs=pl.BlockSpec((TM, POOL_WIDTH), lambda i: (i, 0)),
        out_shape=jax.ShapeDtypeStruct((T_ALL, POOL_WIDTH), BF16),
        compiler_params=_cparams(1),
        name="pool_mix",
    )(hp, hp, hp, pw, ps)


def _merge_kernel(x_ref, mod_ref, yp_ref, ymc_ref, yml_ref, ygc_ref, ygl_ref, sg_ref,
                  wbp_ref, wbm_ref, wbg_ref, wo_ref, lg_ref, lb_ref, rw_ref, rb_ref,
                  x1_ref, u2_ref, ri_ref, rwt_ref, cnt_ref, carry):
    i = pl.program_id(0)

    @pl.when(i == 0)
    def _():
        carry[...] = jnp.zeros_like(carry)

    is_ctx = i < CTX_TILES
    ym = jnp.where(is_ctx, ymc_ref[...], yml_ref[...])
    yg = jnp.where(is_ctx, ygc_ref[...], ygl_ref[...])
    m = (sg_ref[:, 0:D_MODEL].astype(F32) * _dot(yp_ref[...], wbp_ref[...])
         + sg_ref[:, D_MODEL:2 * D_MODEL].astype(F32) * _dot(ym, wbm_ref[...])
         + sg_ref[:, 2 * D_MODEL:3 * D_MODEL].astype(F32) * _dot(yg, wbg_ref[...]))
    y = _dot(m.astype(BF16), wo_ref[...])
    gate1 = mod_ref[0, :, 2 * D_MODEL:3 * D_MODEL]
    x1 = _layer_norm(ALPHA_DN * x_ref[...] + gate1 * y, lg_ref[...], lb_ref[...])
    x1_ref[...] = x1
    sh2 = mod_ref[0, :, 3 * D_MODEL:4 * D_MODEL]
    sc2 = mod_ref[0, :, 4 * D_MODEL:5 * D_MODEL]
    u2 = x1 * (1.0 + sc2) + sh2
    u2_ref[...] = u2

    u_hi, u_lo = _split(u2)
    rw_hi, rw_lo = _split(rw_ref[...])
    both = _dot(u_hi, jnp.concatenate([rw_hi, rw_lo], axis=1))
    logits = both[:, 0:LANES] + both[:, LANES:2 * LANES] + _dot(u_lo, rw_hi) + rb_ref[...]
    lane = lax.broadcasted_iota(I32, (TM, LANES), 1)
    lane_f = lane.astype(F32)
    vals, idxs = [], []
    rest = logits
    for _ in range(TOP_K):
        mx = jnp.max(rest, axis=-1, keepdims=True)
        ix = jnp.min(jnp.where(rest == mx, lane_f, float(LANES)), axis=-1, keepdims=True).astype(I32)
        vals.append(mx)
        idxs.append(ix)
        rest = jnp.where(lane == ix, -jnp.inf, rest)
    es = [jnp.exp(v - vals[0]) for v in vals]
    den = functools.reduce(jnp.add, es)

    sel = functools.reduce(jnp.add, [jnp.where(lane == ix, 1.0, 0.0) for ix in idxs])
    r_i = lax.broadcasted_iota(I32, (TM, TM), 0)
    c_i = lax.broadcasted_iota(I32, (TM, TM), 1)
    below = jnp.where(c_i < r_i, 1.0, 0.0).astype(BF16)
    rank = _dot(below, sel.astype(BF16)) + carry[...]
    carry[...] = carry[...] + jnp.sum(sel, axis=0, keepdims=True)
    cnt_ref[...] = carry[...]

    ri = jnp.zeros((TM, LANES), I32)
    rwt = jnp.zeros((TM, LANES), F32)
    for k in range(TOP_K):
        rk = jnp.sum(jnp.where(lane == idxs[k], rank, 0.0), axis=-1, keepdims=True).astype(I32)
        ri = jnp.where(lane == k, idxs[k], ri)
        ri = jnp.where(lane == TOP_K + k, rk, ri)
        rwt = jnp.where(lane == k, es[k] / den, rwt)
    ri_ref[...] = ri
    rwt_ref[...] = rwt


def _merge(x, mod, yp, ymc, yml, ygc, ygl, sg, wbp, wbm, wbg, wo, lg, lb, rw, rb):
    tok = lambda w: pl.BlockSpec((TM, w), lambda i: (i, 0))
    ctx = lambda w: pl.BlockSpec((TM, w), lambda i: (jnp.minimum(i, CTX_TILES - 1), 0))
    lat = lambda w: pl.BlockSpec((TM, w), lambda i: (jnp.maximum(i - CTX_TILES, 0), 0))
    return pl.pallas_call(
        _merge_kernel,
        grid=(N_TILES,),
        in_specs=[tok(D_MODEL),
                  pl.BlockSpec((1, 1, 6 * D_MODEL), lambda i: (_mod_row(i), 0, 0)),
                  tok(POOL_WIDTH), ctx(512), lat(512), ctx(512), lat(512), tok(3 * D_MODEL),
                  _const_spec((POOL_WIDTH, D_MODEL)), _const_spec((512, D_MODEL)), _const_spec((512, D_MODEL)),
                  _const_spec((D_MODEL, D_MODEL)), _const_spec((1, D_MODEL)), _const_spec((1, D_MODEL)),
                  _const_spec((D_MODEL, LANES)), _const_spec((1, LANES))],
        out_specs=[tok(D_MODEL), tok(D_MODEL), tok(LANES), tok(LANES), _const_spec((1, LANES))],
        out_shape=[jax.ShapeDtypeStruct((T_ALL, D_MODEL), F32), jax.ShapeDtypeStruct((T_ALL, D_MODEL), F32),
                   jax.ShapeDtypeStruct((T_ALL, LANES), I32), jax.ShapeDtypeStruct((T_ALL, LANES), F32),
                   jax.ShapeDtypeStruct((1, LANES), F32)],
        scratch_shapes=[pltpu.VMEM((1, LANES), F32)],
        compiler_params=_cparams(1),
        name="merge_route",
    )(x, mod, yp, ymc, yml, ygc, ygl, sg, wbp, wbm, wbg, wo, lg, lb, rw, rb)


def _row_copy(src_ref, src_row, dst_ref, dst_row, sem):
    return pltpu.make_async_copy(src_ref.at[pl.ds(src_row, 1), :], dst_ref.at[pl.ds(dst_row, 1), :], sem)


ROWS_PER_ISSUE = 4
WAITS_PER_TRIP = 64


def _drain_rows(wait_one, n_rows):
    def trip(_, c):
        for _ in range(WAITS_PER_TRIP):
            wait_one()
        return c

    lax.fori_loop(0, n_rows // WAITS_PER_TRIP, trip, 0)


def _dispatch_kernel(dest_ref, zrow_ref, nu_ref, u_ref, xs_ref, zbuf, sem, zsem):
    i = pl.program_id(0)
    base = i * (TM * TOP_K)

    @pl.when(i == 0)
    def _():
        zbuf[...] = jnp.zeros_like(zbuf)

        def zero_tile(row):
            if not isinstance(row, int):
                row = pl.multiple_of(row, TME)
            return pltpu.make_async_copy(zbuf, xs_ref.at[pl.ds(row, TME), :], zsem)

        for e in range(N_EXPERTS):
            @pl.when(zrow_ref[e] >= 0)
            def _():
                zero_tile(zrow_ref[e]).start()

        def start_tail(j, c):
            zero_tile(j * TME).start()
            return c

        lax.fori_loop(nu_ref[0], NT_E, start_tail, 0)

        for e in range(N_EXPERTS):
            @pl.when(zrow_ref[e] >= 0)
            def _():
                zero_tile(0).wait()

        def wait_tail(j, c):
            zero_tile(0).wait()
            return c

        lax.fori_loop(nu_ref[0], NT_E, wait_tail, 0)

    def issue(g, c):
        for r in range(ROWS_PER_ISSUE):
            t = g * ROWS_PER_ISSUE + r
            for k in range(TOP_K):
                _row_copy(u_ref, t, xs_ref, dest_ref[base + t * TOP_K + k], sem).start(priority=k % 2)
        return c

    lax.fori_loop(0, TM // ROWS_PER_ISSUE, issue, 0)
    _drain_rows(lambda: _row_copy(u_ref, 0, xs_ref, 0, sem).wait(), TM * TOP_K)


def _dispatch(plan, u2):
    grid_spec = pltpu.PrefetchScalarGridSpec(
        num_scalar_prefetch=3,
        grid=(N_TILES,),
        in_specs=[pl.BlockSpec((TM, D_MODEL), lambda i, *_: (i, 0))],
        out_specs=pl.BlockSpec(memory_space=pl.ANY),
        scratch_shapes=[pltpu.VMEM((TME, D_MODEL), F32), pltpu.SemaphoreType.DMA(()), pltpu.SemaphoreType.DMA(())],
    )
    return pl.pallas_call(
        _dispatch_kernel,
        grid_spec=grid_spec,
        out_shape=jax.ShapeDtypeStruct((P_ROWS, D_MODEL), F32),
        compiler_params=_cparams(1),
        name="moe_dispatch",
    )(plan["dest"], plan["zero_row"], plan["n_used"], u2)


def _expert_kernel(layer, te_ref, nu_ref, slot_ref, first_ref, next_ref,
                   xs_ref, wgu_hbm, bgu_ref, wd_hbm, bd_ref, ys_ref, wgu_f, wd_f, wgu_s, wd_s, sems):
    i = pl.program_id(0)

    def weight_copies(e, s):
        return (pltpu.make_async_copy(wgu_hbm.at[layer, e], wgu_f.at[s], sems.at[0, s]),
                pltpu.make_async_copy(wd_hbm.at[layer, e], wd_f.at[s], sems.at[1, s]))

    @pl.when(first_ref[i] == 1)
    def _():
        s = slot_ref[i]

        @pl.when(i == 0)
        def _():
            for cp in weight_copies(te_ref[i], s):
                cp.start()

        for cp in weight_copies(te_ref[i], s):
            cp.wait()

        @pl.when(next_ref[i] >= 0)
        def _():
            for cp in weight_copies(next_ref[i], 1 - s):
                cp.start()

        wgu_s[...] = wgu_f[s].astype(BF16)
        wd_s[...] = wd_f[s].astype(BF16)

    @pl.when(i < nu_ref[0])
    def _():
        h = _dot(xs_ref[...].astype(BF16), wgu_s[...]) + bgu_ref[0, 0]
        glu = jnp.minimum(h[:, 0:D_FF], SWIGLU_LIMIT)
        lin = jnp.clip(h[:, D_FF:2 * D_FF], -SWIGLU_LIMIT, SWIGLU_LIMIT)
        a = glu * _sigmoid(SWIGLU_ALPHA * glu) * (lin + 1.0)
        ys_ref[...] = _dot(a.astype(BF16), wd_s[...]) + bd_ref[0, 0]

    @pl.when(i >= nu_ref[0])
    def _():
        ys_ref[...] = jnp.zeros_like(ys_ref)


def _experts(layer, plan, xs, w_gate_up, b_gate_up, w_down, b_down):
    row = lambda i, te, nu, *_: (jnp.minimum(i, nu[0] - 1), 0)
    bias = lambda i, te, *_: (layer, te[i], 0, 0)
    grid_spec = pltpu.PrefetchScalarGridSpec(
        num_scalar_prefetch=5,
        grid=(NT_E,),
        in_specs=[pl.BlockSpec((TME, D_MODEL), row),
                  pl.BlockSpec(memory_space=pl.ANY),
                  pl.BlockSpec((1, 1, 1, 2 * D_FF), bias),
                  pl.BlockSpec(memory_space=pl.ANY),
                  pl.BlockSpec((1, 1, 1, D_MODEL), bias)],
        out_specs=pl.BlockSpec((TME, D_MODEL), lambda i, *_: (i, 0)),
        scratch_shapes=[pltpu.VMEM((2, D_MODEL, 2 * D_FF), F32), pltpu.VMEM((2, D_FF, D_MODEL), F32),
                        pltpu.VMEM((D_MODEL, 2 * D_FF), BF16), pltpu.VMEM((D_FF, D_MODEL), BF16),
                        pltpu.SemaphoreType.DMA((2, 2))],
    )
    return pl.pallas_call(
        functools.partial(_expert_kernel, layer),
        grid_spec=grid_spec,
        out_shape=jax.ShapeDtypeStruct((P_ROWS, D_MODEL), F32),
        compiler_params=_cparams(1),
        name="moe_experts",
    )(plan["tile_expert"], plan["n_used"], plan["tile_slot"], plan["tile_first"], plan["tile_next"], xs, w_gate_up,
      b_gate_up.reshape(DEPTH, N_EXPERTS, 1, 2 * D_FF), w_down, b_down.reshape(DEPTH, N_EXPERTS, 1, D_MODEL))


def _combine_kernel(dest_ref, ys_ref, rwt_ref, x1_ref, mod_ref, lg_ref, lb_ref, o_ref, buf, sems):
    i = pl.program_id(0)
    slot = jnp.bitwise_and(i, 1)

    def gather(tile, s):
        base = tile * (TM * TOP_K)

        def issue(g, c):
            for r in range(ROWS_PER_ISSUE):
                t = g * ROWS_PER_ISSUE + r
                for k in range(TOP_K):
                    _row_copy(ys_ref, dest_ref[base + t * TOP_K + k], buf.at[s, k], t, sems.at[s]).start(priority=k % 2)
            return c

        lax.fori_loop(0, TM // ROWS_PER_ISSUE, issue, 0)

    @pl.when(i == 0)
    def _():
        gather(0, 0)

    @pl.when(i + 1 < pl.num_programs(0))
    def _():
        gather(i + 1, 1 - slot)

    _drain_rows(lambda: _row_copy(ys_ref, 0, buf.at[slot, 0], 0, sems.at[slot]).wait(), TM * TOP_K)

    moe = functools.reduce(jnp.add, [rwt_ref[:, k:k + 1] * buf[slot, k] for k in range(TOP_K)])
    gate2 = mod_ref[0, :, 5 * D_MODEL:6 * D_MODEL]
    o_ref[...] = _layer_norm(ALPHA_DN * x1_ref[...] + gate2 * moe, lg_ref[...], lb_ref[...])


def _combine(dest, ys, rwt, x1, mod, lg, lb):
    grid_spec = pltpu.PrefetchScalarGridSpec(
        num_scalar_prefetch=1,
        grid=(N_TILES,),
        in_specs=[pl.BlockSpec(memory_space=pl.ANY),
                  pl.BlockSpec((TM, LANES), lambda i, d: (i, 0)),
                  pl.BlockSpec((TM, D_MODEL), lambda i, d: (i, 0)),
                  pl.BlockSpec((1, 1, 6 * D_MODEL), lambda i, d: (_mod_row(i), 0, 0)),
                  pl.BlockSpec((1, D_MODEL), lambda i, d: (0, 0)),
                  pl.BlockSpec((1, D_MODEL), lambda i, d: (0, 0))],
        out_specs=pl.BlockSpec((TM, D_MODEL), lambda i, d: (i, 0)),
        scratch_shapes=[pltpu.VMEM((2, TOP_K, TM, D_MODEL), F32), pltpu.SemaphoreType.DMA((2,))],
    )
    return pl.pallas_call(
        _combine_kernel,
        grid_spec=grid_spec,
        out_shape=jax.ShapeDtypeStruct((T_ALL, D_MODEL), F32),
        compiler_params=_cparams(1),
        name="moe_combine",
    )(dest, ys, rwt, x1, mod, lg, lb)


def _pad_heads(w, n_heads, width):
    rows = w.shape[0]
    w = w.reshape(rows, n_heads, width)
    return jnp.pad(w, ((0, 0), (0, 0), (0, LANES - width))).reshape(rows, n_heads * LANES)


def _dup_heads(w):
    rows = w.shape[0]
    w = w.reshape(rows, GQA_KV, 1, GQA_HD)
    return jnp.concatenate([w, w], axis=2).reshape(rows, GQA_KV * LANES)


def _pack_w_in(w):
    cuts = [0, 512, 896, 1152, 1184, 1696, 1824, 1952, 5024]
    hp, cq, ckv, kr, q, k, v, g = [w[:, a:b] for a, b in zip(cuts[:-1], cuts[1:])]
    packed = jnp.concatenate([hp, cq, ckv, _pad_heads(kr, 1, MLA_ROPE), q, _dup_heads(k), _dup_heads(v), g], axis=1)
    return packed.astype(BF16)


def _pack_mla(wuq, wukv):
    wuq_p = _pad_heads(wuq, MLA_HEADS, MLA_QK).astype(BF16)
    kv = wukv.reshape(MLA_KV_LORA, MLA_HEADS, MLA_NOPE + MLA_V)
    wk_top = _pad_heads(kv[:, :, :MLA_NOPE].reshape(MLA_KV_LORA, MLA_HEADS * MLA_NOPE), MLA_HEADS, MLA_NOPE)
    place = jnp.zeros((LANES, MLA_HEADS, LANES), F32)
    r = jnp.arange(MLA_ROPE)
    place = place.at[r, :, MLA_NOPE + r].set(1.0).reshape(LANES, MLA_HEADS * LANES)
    wk = jnp.concatenate([wk_top, place], axis=0).astype(BF16)
    wv = kv[:, :, MLA_NOPE:].reshape(MLA_KV_LORA, MLA_HEADS * MLA_V).astype(BF16)
    return wuq_p, wk, wv


def _rope_table(dim, lane0):
    quarter = dim // 4
    pos = jnp.arange(S_LAT)
    rows = (pos // GRID_W).astype(F32)
    cols = (pos % GRID_W).astype(F32)
    freqs = jnp.power(ROPE_THETA, -jnp.arange(quarter, dtype=F32) / quarter)
    ang_r = rows[:, None] * freqs[None, :]
    ang_c = cols[:, None] * freqs[None, :]
    zero = jnp.zeros((S_LAT, quarter), F32)
    cos = jnp.concatenate([jnp.cos(ang_r), jnp.cos(ang_r), jnp.cos(ang_c), jnp.cos(ang_c)], axis=1)
    s_up = jnp.concatenate([-jnp.sin(ang_r), zero, -jnp.sin(ang_c), zero], axis=1)
    s_dn = jnp.concatenate([zero, jnp.sin(ang_r), zero, jnp.sin(ang_c)], axis=1)

    def place(t, fill):
        if lane0 is None:
            return jnp.tile(t, (1, LANES // dim))
        full = jnp.full((S_LAT, LANES), fill, F32)
        return full.at[:, lane0:lane0 + dim].set(t)

    planes = jnp.stack([place(cos, 1.0), place(s_up, 0.0), place(s_dn, 0.0)])
    ident = jnp.stack([jnp.ones((TM, LANES), F32), jnp.zeros((TM, LANES), F32), jnp.zeros((TM, LANES), F32)])
    return jnp.concatenate([planes, ident], axis=1)


TILE_LANES = 2 * LANES
ROW_EXPERT, ROW_SLOT, ROW_FIRST, ROW_NEXT, ROW_ZERO, ROW_USED = range(6)


def _plan_kernel(ri_ref, cnt_ref, dest_ref, tiles_ref):
    lane = lax.broadcasted_iota(I32, (1, LANES), 1)
    r_i = lax.broadcasted_iota(I32, (LANES, LANES), 0)
    c_i = lax.broadcasted_iota(I32, (LANES, LANES), 1)
    upto = jnp.where(r_i <= c_i, 1.0, 0.0).astype(BF16)

    def prefix(row):
        return _dot(jnp.broadcast_to(row, (SUBLANES, LANES)).astype(BF16), upto)[0:1, :]

    cnt = cnt_ref[...]
    t_cnt = jnp.floor((cnt + (TME - 1)) * (1.0 / TME))
    t_end = prefix(t_cnt)
    t_start = t_end - t_cnt
    n_used = jnp.max(t_end, axis=-1, keepdims=True)

    tok_lane = lax.broadcasted_iota(I32, ri_ref.shape, 1)
    ri = ri_ref[...]
    dest = jnp.zeros(ri_ref.shape, I32)
    for k in range(TOP_K):
        start_k = jnp.sum(jnp.where(tok_lane == ri[:, k:k + 1], t_start, 0.0), axis=-1, keepdims=True)
        dest_k = start_k.astype(I32) * TME + ri[:, TOP_K + k:TOP_K + k + 1]
        dest = jnp.where(tok_lane == k, dest_k, dest)
    dest_ref[...] = dest

    @pl.when(pl.program_id(0) == 0)
    def _():
        eye = jnp.where(r_i == c_i, 1.0, 0.0).astype(BF16)

        def to_col(row):
            return _dot_t(eye, jnp.broadcast_to(row, (SUBLANES, LANES)).astype(BF16))[:, 0:1]

        e_col = lax.broadcasted_iota(I32, (LANES, 1), 0)
        real_e = e_col < N_EXPERTS
        tile = lax.broadcasted_iota(I32, (LANES, TILE_LANES), 1).astype(F32)
        end_col = to_col(t_end)
        expert = jnp.sum(jnp.where(real_e & (end_col <= tile), 1.0, 0.0), axis=0, keepdims=True)
        last = jnp.sum(jnp.where(real_e & (end_col <= n_used - 1.0), 1.0, 0.0), axis=0, keepdims=True)
        expert = jnp.minimum(expert, last)
        mine = e_col.astype(F32) == expert

        def per_tile(col):
            return jnp.sum(jnp.where(mine, col, 0.0), axis=0, keepdims=True)

        has = t_cnt > 0.0
        ordinal = prefix(jnp.where(has, 1.0, 0.0)) - 1.0
        ord_col = to_col(ordinal)
        slot = per_tile(ord_col - 2.0 * jnp.floor(ord_col * 0.5))
        tile_row = tile[0:1, :]
        first = jnp.where((tile_row < n_used) & (tile_row == per_tile(to_col(t_start))), 1.0, 0.0)
        later = jnp.where(has & (lane > e_col), lane.astype(F32), float(LANES))
        nxt_col = jnp.min(later, axis=-1, keepdims=True)
        nxt_col = jnp.where(nxt_col == float(LANES), -1.0, nxt_col)
        nxt = per_tile(nxt_col)
        zero_row = jnp.where(has, (t_end - 1.0) * TME, -1.0)
        zero_row = jnp.concatenate([zero_row, jnp.full((1, TILE_LANES - LANES), -1.0, F32)], axis=1)
        used = jnp.broadcast_to(n_used, (1, TILE_LANES))
        pad = jnp.zeros((SUBLANES - 6, TILE_LANES), F32)
        tiles_ref[...] = jnp.concatenate([expert, slot, first, nxt, zero_row, used, pad], axis=0).astype(I32)


def _routing_plan(ri, counts):
    rows = 4 * TM
    dest, tiles = pl.pallas_call(
        _plan_kernel,
        grid=(T_ALL // rows,),
        in_specs=[pl.BlockSpec((rows, LANES), lambda i: (i, 0)), _const_spec((1, LANES))],
        out_specs=[pl.BlockSpec((rows, LANES), lambda i: (i, 0)), _const_spec((SUBLANES, TILE_LANES))],
        out_shape=[jax.ShapeDtypeStruct((T_ALL, LANES), I32), jax.ShapeDtypeStruct((SUBLANES, TILE_LANES), I32)],
        compiler_params=_cparams(1),
        name="moe_plan",
    )(ri, counts)
    return {"dest": dest[:, 0:TOP_K].reshape(-1), "tile_expert": tiles[ROW_EXPERT, :NT_E],
            "n_used": tiles[ROW_USED, 0:1], "tile_slot": tiles[ROW_SLOT, :NT_E], "tile_first": tiles[ROW_FIRST, :NT_E],
            "tile_next": tiles[ROW_NEXT, :NT_E], "zero_row": tiles[ROW_ZERO, :N_EXPERTS]}


def kernel(x_prompt, x_sample, c, cache_mla_ckv, cache_mla_krope, cache_gqa_k, cache_gqa_v, c_ctx, w_ada, b_ada, w_in, mla_q_norm, mla_kv_norm, w_mla_uq, w_mla_ukv, gqa_sink, pool_w, pool_scale, w_branch_pool, w_branch_mla, w_branch_gqa, w_out, ln1_g, ln1_b, ln2_g, ln2_b, router_w, router_b, w_gate_up, b_gate_up, w_down, b_down):
    x = jnp.concatenate([x_prompt.reshape(T_CTX, D_MODEL), x_sample.reshape(T_LAT, D_MODEL)], axis=0)
    cond8 = jnp.concatenate([c_ctx[None, :], c, jnp.zeros((SUBLANES - 1 - B_LAT, D_MODEL), F32)], axis=0)
    mod_all = _ada_params(cond8, w_ada, b_ada)

    tab_g = _rope_table(GQA_HD, None)
    tab_q = _rope_table(MLA_ROPE, MLA_NOPE)
    tab_k = _rope_table(MLA_ROPE, 0)

    outs = {"ckv": [], "kr": [], "gk": [], "gv": []}
    for l in range(DEPTH):
        mod = mod_all[l].reshape(SUBLANES, 1, 6 * D_MODEL)
        wp = _pack_w_in(w_in[l])
        wuq_p, wk, wv = _pack_mla(w_mla_uq[l], w_mla_ukv[l])

        (hp, qm, ckv, kr, km, vm, qg, kg, vg, nk, nv, sg) = _in_proj(
            x, mod, wp, mla_q_norm[l][None, :], mla_kv_norm[l][None, :], wuq_p, wk, wv, tab_g, tab_q, tab_k)
        outs["ckv"].append(ckv[:T_CTX].reshape(B_CTX, S_CTX, MLA_KV_LORA))
        outs["kr"].append(kr[:T_CTX, :MLA_ROPE].reshape(B_CTX, S_CTX, MLA_ROPE))
        outs["gk"].append(nk[:T_CTX].reshape(B_CTX, S_CTX, GQA_KV, GQA_HD))
        outs["gv"].append(nv[:T_CTX].reshape(B_CTX, S_CTX, GQA_KV, GQA_HD))

        ckv_c = cache_mla_ckv[:, l].reshape(B_LAT * PAST_LEN, MLA_KV_LORA)
        kr_c = jnp.pad(cache_mla_krope[:, l].reshape(B_LAT * PAST_LEN, MLA_ROPE), ((0, 0), (0, LANES - MLA_ROPE)))
        km_c, vm_c = _expand_cache(ckv_c, kr_c, wk, wv)
        gk_c = _dup_heads(cache_gqa_k[:, l].reshape(B_LAT * PAST_LEN, GQA_KV * GQA_HD)).astype(BF16)
        gv_c = _dup_heads(cache_gqa_v[:, l].reshape(B_LAT * PAST_LEN, GQA_KV * GQA_HD)).astype(BF16)

        yp = _pool(hp, pool_w[l].astype(BF16), pool_scale[l][None, :])
        ymc = _mla_ctx(qm, km, vm)
        yml = _mla_lat(qm, km, vm, km_c, vm_c)
        ygc = _gqa_ctx(gqa_sink[l], qg, kg, vg)
        ygl = _gqa_lat(gqa_sink[l], qg, kg, vg, gk_c, gv_c)

        rw = jnp.pad(router_w[l], ((0, 0), (0, LANES - N_EXPERTS)))
        rb = jnp.concatenate([router_b[l], jnp.full((LANES - N_EXPERTS,), -jnp.inf, F32)])[None, :]
        x1, u2, ri, rwt, counts = _merge(
            x, mod, yp, ymc, yml, ygc, ygl, sg,
            w_branch_pool[l].astype(BF16), w_branch_mla[l].astype(BF16), w_branch_gqa[l].astype(BF16),
            w_out[l].astype(BF16), ln1_g[l][None, :], ln1_b[l][None, :], rw, rb)

        plan = _routing_plan(ri, counts)
        xs = _dispatch(plan, u2)
        ys = _experts(l, plan, xs, w_gate_up, b_gate_up, w_down, b_down)
        x = _combine(plan["dest"], ys, rwt, x1, mod, ln2_g[l][None, :], ln2_b[l][None, :])

    y_prompt = x[:T_CTX].reshape(B_CTX, S_CTX, D_MODEL)
    y_sample = x[T_CTX:].reshape(B_LAT, S_LAT, D_MODEL)
    return (y_prompt, y_sample, jnp.stack(outs["ckv"], axis=1), jnp.stack(outs["kr"], axis=1),
            jnp.stack(outs["gk"], axis=1), jnp.stack(outs["gv"], axis=1))
```

```python
import functools
import math

import jax
import jax.numpy as jnp
from jax import lax
from jax.experimental import pallas as pl
from jax.experimental.pallas import tpu as pltpu

F32 = jnp.float32
BF16 = jnp.bfloat16
I32 = jnp.int32

D_MODEL = 1024
DEPTH = 2
B_CTX, S_CTX = 16, 256
B_LAT, S_LAT = 4, 1024
PAST_LEN = 512
GRID_W = 64
ROPE_THETA = 10000.0
ALPHA_DN = (2 * DEPTH) ** 0.25
LN_EPS = 1e-5
RMS_EPS = 1e-6
NEG_INF = -1e30
POOL_WINDOWS = (2, 4, 8, 16)
POOL_GROUP = 128
POOL_WIDTH = 512
MLA_HEADS = 8
MLA_NOPE = 64
MLA_ROPE = 32
MLA_V = 64
MLA_QK = MLA_NOPE + MLA_ROPE
MLA_Q_LORA = 384
MLA_KV_LORA = 256
MLA_SCALE = 1.0 / math.sqrt(MLA_QK)
GQA_HEADS = 8
GQA_KV = 2
GQA_HD = 64
GQA_GROUP = GQA_HEADS // GQA_KV
GQA_SCALE = 1.0 / math.sqrt(GQA_HD)
WBLK = 128
N_EXPERTS = 32
TOP_K = 4
D_FF = 1024
SWIGLU_LIMIT = 7.0
SWIGLU_ALPHA = 1.702

LANES = 128
SUBLANES = 8
VMEM_LIMIT = 56 * 1024 * 1024

T_CTX = B_CTX * S_CTX
T_LAT = B_LAT * S_LAT
T_ALL = T_CTX + T_LAT
TM = 256
N_TILES = T_ALL // TM
CTX_TILES = T_CTX // TM
LAT_TILES_PER_SEQ = S_LAT // TM
HALO = 16
MLA_LAT_TQ = 512

C_HP = 0
C_CQ = C_HP + POOL_WIDTH
C_CKV = C_CQ + MLA_Q_LORA
C_KR = C_CKV + MLA_KV_LORA
C_GQ = C_KR + LANES
C_GK = C_GQ + GQA_HEADS * GQA_HD
C_GV = C_GK + GQA_KV * LANES
C_G = C_GV + GQA_KV * LANES
C_END = C_G + 3 * D_MODEL

TME = 256
N_PAIRS = T_ALL * TOP_K
NT_E = N_PAIRS // TME + N_EXPERTS
P_ROWS = NT_E * TME


def _cparams(n_axes):
    return pltpu.CompilerParams(dimension_semantics=("arbitrary",) * n_axes, vmem_limit_bytes=VMEM_LIMIT)


def _dot(a, b):
    return jnp.dot(a, b, preferred_element_type=F32)


def _dot_t(a, b):
    return lax.dot_general(a, b, (((1,), (1,)), ((), ())), preferred_element_type=F32)


def _split(x):
    hi = x.astype(BF16)
    lo = (x - hi.astype(F32)).astype(BF16)
    return hi, lo


def _dot3(a, b):
    a_hi, a_lo = _split(a)
    b_hi, b_lo = _split(b)
    return _dot(a_hi, b_hi) + _dot(a_hi, b_lo) + _dot(a_lo, b_hi)


def _sigmoid(x):
    return 1.0 / (1.0 + jnp.exp(-x))


def _layer_norm(z, g, b):
    mu = jnp.mean(z, axis=-1, keepdims=True)
    zc = z - mu
    var = jnp.mean(zc * zc, axis=-1, keepdims=True)
    return zc * lax.rsqrt(var + LN_EPS) * g + b


def _mod_row(i):
    return jnp.where(i < CTX_TILES, 0, 1 + (i - CTX_TILES) // LAT_TILES_PER_SEQ)


def _pos_block(i):
    return jnp.where(i < CTX_TILES, LAT_TILES_PER_SEQ, (i - CTX_TILES) % LAT_TILES_PER_SEQ)


def _const_spec(shape):
    nd = len(shape)
    return pl.BlockSpec(shape, lambda *_: (0,) * nd)


def _layer_spec(layer, shape):
    nd = len(shape)
    return pl.BlockSpec((None,) + tuple(shape), lambda *_: (layer,) + (0,) * nd)


def _ada_kernel(c_ref, w_ref, b_ref, o_ref):
    c = c_ref[...]
    s = c * _sigmoid(c)
    o_ref[0] = _dot3(s, w_ref[0]) + b_ref[0]


def _ada_params(cond8, w_ada, b_ada):
    n_col = 6 * D_MODEL
    blk = 1024
    return pl.pallas_call(
        _ada_kernel,
        grid=(DEPTH, n_col // blk),
        in_specs=[
            pl.BlockSpec((SUBLANES, D_MODEL), lambda l, j: (0, 0)),
            pl.BlockSpec((1, D_MODEL, blk), lambda l, j: (l, 0, j)),
            pl.BlockSpec((1, 1, blk), lambda l, j: (l, 0, j)),
        ],
        out_specs=pl.BlockSpec((1, SUBLANES, blk), lambda l, j: (l, 0, j)),
        out_shape=jax.ShapeDtypeStruct((DEPTH, SUBLANES, n_col), F32),
        compiler_params=_cparams(2),
        name="ada_params",
    )(cond8, w_ada, b_ada.reshape(DEPTH, 1, n_col))


def _rope(x, tab_ref, shift):
    return (x * tab_ref[0]
            + pltpu.roll(x, LANES - shift, 1) * tab_ref[1]
            + pltpu.roll(x, shift, 1) * tab_ref[2])


def _rms(h, g):
    return h * lax.rsqrt(jnp.mean(h * h, axis=-1, keepdims=True) + RMS_EPS) * g


def _in_kernel(xc_ref, xl_ref, mod_ref, wp_ref, qn_ref, kvn_ref, wuq_ref, wk_ref, wv_ref, tg_ref, tq_ref, tk_ref,
               hp_ref, qm_ref, ckv_ref, kr_ref, km_ref, vm_ref, qg_ref, kg_ref, vg_ref, nk_ref, nv_ref, sg_ref):
    x = jnp.where(pl.program_id(0) < CTX_TILES, xc_ref[...], xl_ref[...])
    sh = mod_ref[0, :, 0:D_MODEL]
    sc = mod_ref[0, :, D_MODEL:2 * D_MODEL]
    u = (x * (1.0 + sc) + sh).astype(BF16)

    def proj(lo, hi):
        return _dot(u, wp_ref[:, lo:hi])

    hp_ref[...] = proj(C_HP, C_CQ)

    qn = _rms(proj(C_CQ, C_CKV), qn_ref[...]).astype(BF16)
    q = _dot(qn, wuq_ref[...]) * MLA_SCALE
    for h in range(MLA_HEADS):
        qm_ref[:, h * LANES:(h + 1) * LANES] = _rope(q[:, h * LANES:(h + 1) * LANES], tq_ref, MLA_ROPE // 4).astype(BF16)

    ckv = _rms(proj(C_CKV, C_KR), kvn_ref[...])
    ckv_ref[...] = ckv
    kr = _rope(proj(C_KR, C_GQ), tk_ref, MLA_ROPE // 4)
    kr_ref[...] = kr
    ckv_b = ckv.astype(BF16)
    km_ref[...] = (_dot(ckv_b, wk_ref[0:MLA_KV_LORA, :]) + _dot(kr.astype(BF16), wk_ref[MLA_KV_LORA:, :])).astype(BF16)
    vm_ref[...] = _dot(ckv_b, wv_ref[...]).astype(BF16)

    hq = proj(C_GQ, C_GK) * GQA_SCALE
    for j in range(GQA_HEADS * GQA_HD // LANES):
        qg_ref[:, j * LANES:(j + 1) * LANES] = _rope(hq[:, j * LANES:(j + 1) * LANES], tg_ref, GQA_HD // 4).astype(BF16)
    hk = proj(C_GK, C_GV)
    hv = proj(C_GV, C_G)
    for g in range(GQA_KV):
        kg_ref[:, g * LANES:(g + 1) * LANES] = _rope(hk[:, g * LANES:(g + 1) * LANES], tg_ref, GQA_HD // 4).astype(BF16)
    vg_ref[...] = hv.astype(BF16)
    low_half = lax.broadcasted_iota(I32, (TM, LANES), 1) < GQA_HD
    nk_ref[...] = jnp.where(low_half, hk[:, 0:LANES], hk[:, LANES:2 * LANES])
    nv_ref[...] = jnp.where(low_half, hv[:, 0:LANES], hv[:, LANES:2 * LANES])

    for j in range(3):
        sg_ref[:, j * D_MODEL:(j + 1) * D_MODEL] = _sigmoid(proj(C_G + j * D_MODEL, C_G + (j + 1) * D_MODEL)).astype(BF16)


def _in_proj(layer, xc, xl, mod, wp, qn, kvn, wuq, wk, wv, tab_g, tab_q, tab_k):
    tok = lambda w: pl.BlockSpec((TM, w), lambda i: (i, 0))
    per_layer = functools.partial(_layer_spec, layer)
    x_specs = [pl.BlockSpec((TM, D_MODEL), lambda i: (jnp.minimum(i, CTX_TILES - 1), 0)),
               pl.BlockSpec((TM, D_MODEL), lambda i: (jnp.maximum(i - CTX_TILES, 0), 0))]
    tab = pl.BlockSpec((3, TM, LANES), lambda i: (0, _pos_block(i), 0))
    out_widths = [(POOL_WIDTH, F32), (MLA_HEADS * LANES, BF16), (MLA_KV_LORA, F32), (LANES, F32),
                  (MLA_HEADS * LANES, BF16), (MLA_HEADS * MLA_V, BF16), (GQA_HEADS * GQA_HD, BF16),
                  (GQA_KV * LANES, BF16), (GQA_KV * LANES, BF16), (GQA_KV * GQA_HD, F32), (GQA_KV * GQA_HD, F32),
                  (3 * D_MODEL, BF16)]
    return pl.pallas_call(
        _in_kernel,
        grid=(N_TILES,),
        in_specs=x_specs + [
            pl.BlockSpec((None, 1, 1, 6 * D_MODEL), lambda i: (layer, _mod_row(i), 0, 0)),
            per_layer((D_MODEL, C_END)),
            per_layer((1, MLA_Q_LORA)),
            per_layer((1, MLA_KV_LORA)),
            per_layer((MLA_Q_LORA, MLA_HEADS * LANES)),
            per_layer((MLA_KV_LORA + LANES, MLA_HEADS * LANES)),
            per_layer((MLA_KV_LORA, MLA_HEADS * MLA_V)),
            tab, tab, tab,
        ],
        out_specs=[tok(w) for w, _ in out_widths],
        out_shape=[jax.ShapeDtypeStruct((T_ALL, w), dt) for w, dt in out_widths],
        compiler_params=_cparams(1),
        name="in_proj",
    )(xc, xl, mod, wp, qn, kvn, wuq, wk, wv, tab_g, tab_q, tab_k)


def _expand_kernel(ckv_ref, kr_ref, wk_ref, wv_ref, km_ref, vm_ref):
    ckv_b = ckv_ref[...].astype(BF16)
    km_ref[...] = (_dot(ckv_b, wk_ref[0:MLA_KV_LORA, :]) + _dot(kr_ref[...].astype(BF16), wk_ref[MLA_KV_LORA:, :])).astype(BF16)
    vm_ref[...] = _dot(ckv_b, wv_ref[...]).astype(BF16)


def _cache_block(layer):
    return lambda b, *_: (b * DEPTH + layer, 0)


def _expand_cache(layer, ckv, kr_pad, wk, wv):
    rows = B_LAT * PAST_LEN
    blk = PAST_LEN
    return pl.pallas_call(
        _expand_kernel,
        grid=(B_LAT,),
        in_specs=[
            pl.BlockSpec((blk, MLA_KV_LORA), _cache_block(layer)),
            pl.BlockSpec((blk, LANES), _cache_block(layer)),
            _layer_spec(layer, (MLA_KV_LORA + LANES, MLA_HEADS * LANES)),
            _layer_spec(layer, (MLA_KV_LORA, MLA_HEADS * MLA_V)),
        ],
        out_specs=[pl.BlockSpec((blk, MLA_HEADS * LANES), lambda i: (i, 0)),
                   pl.BlockSpec((blk, MLA_HEADS * MLA_V), lambda i: (i, 0))],
        out_shape=[jax.ShapeDtypeStruct((rows, MLA_HEADS * LANES), BF16),
                   jax.ShapeDtypeStruct((rows, MLA_HEADS * MLA_V), BF16)],
        compiler_params=_cparams(1),
        name="mla_expand_cache",
    )(ckv, kr_pad, wk, wv)


def _lane_chunks(x):
    return [x[:, c:c + LANES] for c in range(0, x.shape[1], LANES)]


def _softmax_sets(ss, extra=None):
    m = jnp.max(functools.reduce(jnp.maximum, [c for s in ss for c in _lane_chunks(s)]), axis=-1, keepdims=True)
    if extra is not None:
        m = jnp.maximum(m, extra)
    ps = [jnp.exp(s - m) for s in ss]
    den = jnp.sum(functools.reduce(jnp.add, [c for p in ps for c in _lane_chunks(p)]), axis=-1, keepdims=True)
    if extra is not None:
        den = den + jnp.exp(extra - m)
    return ps, den


def _mla_kernel(n_sets, q_ref, *refs):
    k_refs = refs[:n_sets]
    v_refs = refs[n_sets:2 * n_sets]
    o_ref = refs[2 * n_sets]
    rows = q_ref.shape[0]
    low_half = lax.broadcasted_iota(I32, (rows, LANES), 1) < MLA_V
    for j in range(MLA_HEADS // 2):
        outs = []
        for h in (2 * j, 2 * j + 1):
            qh = q_ref[:, h * LANES:(h + 1) * LANES]
            ps, den = _softmax_sets([_dot_t(qh, k[:, h * LANES:(h + 1) * LANES]) for k in k_refs])
            o = functools.reduce(jnp.add, [_dot(p.astype(BF16), v[:, j * LANES:(j + 1) * LANES]) for p, v in zip(ps, v_refs)])
            outs.append(o / den)
        o_ref[:, j * LANES:(j + 1) * LANES] = jnp.where(low_half, outs[0], outs[1]).astype(BF16)


def _mla_ctx(qm, km, vm):
    blk = S_CTX
    return pl.pallas_call(
        functools.partial(_mla_kernel, 1),
        grid=(B_CTX,),
        in_specs=[pl.BlockSpec((blk, MLA_HEADS * LANES), lambda b: (b, 0)),
                  pl.BlockSpec((blk, MLA_HEADS * LANES), lambda b: (b, 0)),
                  pl.BlockSpec((blk, MLA_HEADS * MLA_V), lambda b: (b, 0))],
        out_specs=pl.BlockSpec((blk, MLA_HEADS * MLA_V), lambda b: (b, 0)),
        out_shape=jax.ShapeDtypeStruct((T_CTX, MLA_HEADS * MLA_V), BF16),
        compiler_params=_cparams(1),
        name="mla_attn_ctx",
    )(qm, km, vm)


def _mla_lat(qm, km, vm, km_c, vm_c):
    lat0 = T_CTX // S_LAT
    tq = MLA_LAT_TQ
    n_q = S_LAT // tq
    return pl.pallas_call(
        functools.partial(_mla_kernel, 2),
        grid=(B_LAT, n_q),
        in_specs=[pl.BlockSpec((tq, MLA_HEADS * LANES), lambda b, t: (T_CTX // tq + b * n_q + t, 0)),
                  pl.BlockSpec((PAST_LEN, MLA_HEADS * LANES), lambda b, t: (b, 0)),
                  pl.BlockSpec((S_LAT, MLA_HEADS * LANES), lambda b, t: (lat0 + b, 0)),
                  pl.BlockSpec((PAST_LEN, MLA_HEADS * MLA_V), lambda b, t: (b, 0)),
                  pl.BlockSpec((S_LAT, MLA_HEADS * MLA_V), lambda b, t: (lat0 + b, 0))],
        out_specs=pl.BlockSpec((tq, MLA_HEADS * MLA_V), lambda b, t: (b * n_q + t, 0)),
        out_shape=jax.ShapeDtypeStruct((T_LAT, MLA_HEADS * MLA_V), BF16),
        compiler_params=_cparams(2),
        name="mla_attn_lat",
    )(qm, km_c, km, vm_c, vm)


def _gqa_kernel(band, layer, sink_ref, q_ref, *refs):
    n_sets = 4 if band else 1
    k_refs = refs[:n_sets]
    v_refs = refs[n_sets:2 * n_sets]
    o_ref = refs[2 * n_sets]
    rows = q_ref.shape[0]
    stack = 2 if band else 1
    low_half = lax.broadcasted_iota(I32, (rows, LANES), 1) < GQA_HD
    masks = [None] * n_sets
    if band:
        n = pl.program_id(1)
        n_blk = pl.num_programs(1)
        qi = lax.broadcasted_iota(I32, (stack * rows, WBLK), 0) % rows
        kj = lax.broadcasted_iota(I32, (stack * rows, WBLK), 1)
        masks[0] = kj >= qi + jnp.where(n > 0, 0, WBLK)
        masks[2] = kj <= qi - jnp.where(n < n_blk - 1, 0, WBLK)
    zero = jnp.zeros((rows, LANES), BF16)
    for j in range(GQA_HEADS // 2):
        g = j // (GQA_GROUP // 2)
        blk = q_ref[:, j * LANES:(j + 1) * LANES]
        q_heads = [jnp.where(low_half, blk, zero), jnp.where(low_half, zero, blk)]
        outs = []
        for h0 in range(0, 2, stack):
            qs = jnp.concatenate(q_heads[h0:h0 + stack], axis=0)
            sink = jnp.concatenate([jnp.full((rows, 1), sink_ref[layer * GQA_HEADS + 2 * j + h0 + i], F32)
                                    for i in range(stack)], axis=0)
            ss = []
            for k, msk in zip(k_refs, masks):
                s = _dot_t(qs, k[:, g * LANES:(g + 1) * LANES])
                ss.append(s if msk is None else jnp.where(msk, s, NEG_INF))
            ps, den = _softmax_sets(ss, sink)
            o = functools.reduce(jnp.add, [_dot(p.astype(BF16), v[:, g * LANES:(g + 1) * LANES]) for p, v in zip(ps, v_refs)])
            o = o / den
            outs += [o[i * rows:(i + 1) * rows] for i in range(stack)]
        o_ref[:, j * LANES:(j + 1) * LANES] = jnp.where(low_half, outs[0], outs[1]).astype(BF16)


def _gqa_ctx(layer, sink, qg, kg, vg):
    blk = S_CTX
    grid_spec = pltpu.PrefetchScalarGridSpec(
        num_scalar_prefetch=1,
        grid=(B_CTX,),
        in_specs=[pl.BlockSpec((blk, GQA_HEADS * GQA_HD), lambda b, s: (b, 0)),
                  pl.BlockSpec((blk, GQA_KV * LANES), lambda b, s: (b, 0)),
                  pl.BlockSpec((blk, GQA_KV * LANES), lambda b, s: (b, 0))],
        out_specs=pl.BlockSpec((blk, GQA_HEADS * GQA_HD), lambda b, s: (b, 0)),
    )
    return pl.pallas_call(
        functools.partial(_gqa_kernel, False, layer),
        grid_spec=grid_spec,
        out_shape=jax.ShapeDtypeStruct((T_CTX, GQA_HEADS * GQA_HD), BF16),
        compiler_params=_cparams(1),
        name="gqa_attn_ctx",
    )(sink, qg, kg, vg)


def _gqa_lat(layer, sink, qg, kg, vg, kg_c, vg_c):
    nb = S_LAT // WBLK
    first = T_CTX // WBLK

    def blk(off):
        def index_map(b, n, s):
            return (first + b * nb + jnp.clip(n + off, 0, nb - 1), 0)
        return index_map

    kv_w = GQA_KV * LANES
    grid_spec = pltpu.PrefetchScalarGridSpec(
        num_scalar_prefetch=1,
        grid=(B_LAT, nb),
        in_specs=[pl.BlockSpec((WBLK, GQA_HEADS * GQA_HD), blk(0)),
                  pl.BlockSpec((WBLK, kv_w), blk(-1)), pl.BlockSpec((WBLK, kv_w), blk(0)), pl.BlockSpec((WBLK, kv_w), blk(1)),
                  pl.BlockSpec((PAST_LEN, kv_w), _cache_block(layer)),
                  pl.BlockSpec((WBLK, kv_w), blk(-1)), pl.BlockSpec((WBLK, kv_w), blk(0)), pl.BlockSpec((WBLK, kv_w), blk(1)),
                  pl.BlockSpec((PAST_LEN, kv_w), _cache_block(layer))],
        out_specs=pl.BlockSpec((WBLK, GQA_HEADS * GQA_HD), lambda b, n, s: (b * nb + n, 0)),
    )
    return pl.pallas_call(
        functools.partial(_gqa_kernel, True, layer),
        grid_spec=grid_spec,
        out_shape=jax.ShapeDtypeStruct((T_LAT, GQA_HEADS * GQA_HD), BF16),
        compiler_params=_cparams(2),
        name="gqa_attn_lat",
    )(sink, qg, kg, kg, kg, kg_c, vg, vg, vg, vg_c)


def _pool_tile(i, hp_p, hp_c, hp_n, pw_ref, ps_ref):
    is_ctx = i < CTX_TILES
    t4 = (i - CTX_TILES) % LAT_TILES_PER_SEQ
    seq_len = jnp.where(is_ctx, S_CTX, S_LAT)
    base = jnp.where(is_ctx, 0, t4 * TM)

    n_keys = TM + 2 * HALO
    first_key = jnp.where(jnp.logical_or(is_ctx, t4 == 0), HALO, 0)
    end_key = jnp.where(jnp.logical_or(is_ctx, t4 == LAT_TILES_PER_SEQ - 1), HALO + TM, n_keys)
    qi = lax.broadcasted_iota(I32, (TM, n_keys), 0)
    kj = lax.broadcasted_iota(I32, (TM, n_keys), 1)
    rel = kj - HALO - qi
    key_ok = (kj >= first_key) & (kj < end_key)

    cur = hp_c[...]
    keys = jnp.concatenate([hp_p[TM - HALO:TM, :], cur, hp_n[0:HALO, :]], axis=0)
    k_hi, k_lo = _split(keys)
    qpos = base + lax.broadcasted_iota(I32, (TM, 1), 0)
    outs = []
    for g, w in enumerate(POOL_WINDOWS):
        a = jnp.where((rel >= -(w // 2)) & (rel <= w // 2 - 1) & key_ok, 1.0, 0.0).astype(BF16)
        cols = slice(g * POOL_GROUP, (g + 1) * POOL_GROUP)
        s = _dot(a, k_hi[:, cols]) + _dot(a, k_lo[:, cols])
        lo = jnp.maximum(qpos - w // 2, 0)
        hi = jnp.minimum(qpos + w // 2 - 1, seq_len - 1)
        cnt = (hi - lo + 1).astype(F32)
        d = s / cnt - cur[:, cols]
        y = _dot(d.astype(BF16), pw_ref[g]) * ps_ref[:, cols]
        outs.append(y.astype(BF16))
    return jnp.concatenate(outs, axis=1)


def _merge_kernel(xc_ref, xl_ref, mod_ref, hp_p, hp_c, hp_n, pw_ref, ps_ref, ymc_ref, yml_ref, ygc_ref, ygl_ref, sg_ref,
                  wbp_ref, wbm_ref, wbg_ref, wo_ref, lg_ref, lb_ref, rw_ref, rb_ref,
                  x1_ref, u2_ref, ri_ref, rwt_ref, cnt_ref, carry):
    i = pl.program_id(0)

    @pl.when(i == 0)
    def _():
        carry[...] = jnp.zeros_like(carry)

    is_ctx = i < CTX_TILES
    x = jnp.where(is_ctx, xc_ref[...], xl_ref[...])
    yp = _pool_tile(i, hp_p, hp_c, hp_n, pw_ref, ps_ref)
    ym = jnp.where(is_ctx, ymc_ref[...], yml_ref[...])
    yg = jnp.where(is_ctx, ygc_ref[...], ygl_ref[...])
    m = (sg_ref[:, 0:D_MODEL].astype(F32) * _dot(yp, wbp_ref[...])
         + sg_ref[:, D_MODEL:2 * D_MODEL].astype(F32) * _dot(ym, wbm_ref[...])
         + sg_ref[:, 2 * D_MODEL:3 * D_MODEL].astype(F32) * _dot(yg, wbg_ref[...]))
    y = _dot(m.astype(BF16), wo_ref[...])
    gate1 = mod_ref[0, :, 2 * D_MODEL:3 * D_MODEL]
    x1 = _layer_norm(ALPHA_DN * x + gate1 * y, lg_ref[...], lb_ref[...])
    x1_ref[...] = x1
    sh2 = mod_ref[0, :, 3 * D_MODEL:4 * D_MODEL]
    sc2 = mod_ref[0, :, 4 * D_MODEL:5 * D_MODEL]
    u2 = x1 * (1.0 + sc2) + sh2
    u2_ref[...] = u2

    u_hi, u_lo = _split(u2)
    rw_hi, rw_lo = _split(rw_ref[...])
    both = _dot(u_hi, jnp.concatenate([rw_hi, rw_lo], axis=1))
    logits = both[:, 0:LANES] + both[:, LANES:2 * LANES] + _dot(u_lo, rw_hi) + rb_ref[...]
    lane = lax.broadcasted_iota(I32, (TM, LANES), 1)
    lane_f = lane.astype(F32)
    vals, idxs = [], []
    rest = logits
    for _ in range(TOP_K):
        mx = jnp.max(rest, axis=-1, keepdims=True)
        ix = jnp.min(jnp.where(rest == mx, lane_f, float(LANES)), axis=-1, keepdims=True).astype(I32)
        vals.append(mx)
        idxs.append(ix)
        rest = jnp.where(lane == ix, -jnp.inf, rest)
    es = [jnp.exp(v - vals[0]) for v in vals]
    den = functools.reduce(jnp.add, es)

    sel = functools.reduce(jnp.add, [jnp.where(lane == ix, 1.0, 0.0) for ix in idxs])
    r_i = lax.broadcasted_iota(I32, (TM, TM), 0)
    c_i = lax.broadcasted_iota(I32, (TM, TM), 1)
    below = jnp.where(c_i < r_i, 1.0, 0.0).astype(BF16)
    rank = _dot(below, sel.astype(BF16)) + carry[...]
    carry[...] = carry[...] + jnp.sum(sel, axis=0, keepdims=True)
    cnt_ref[...] = carry[...]

    ri = jnp.zeros((TM, LANES), I32)
    rwt = jnp.zeros((TM, LANES), F32)
    for k in range(TOP_K):
        rk = jnp.sum(jnp.where(lane == idxs[k], rank, 0.0), axis=-1, keepdims=True).astype(I32)
        ri = jnp.where(lane == k, idxs[k], ri)
        ri = jnp.where(lane == TOP_K + k, rk, ri)
        rwt = jnp.where(lane == k, es[k] / den, rwt)
    ri_ref[...] = ri
    rwt_ref[...] = rwt


def _tok_spec(width):
    return pl.BlockSpec((TM, width), lambda i, *_: (i, 0))


def _ctx_spec(width):
    return pl.BlockSpec((TM, width), lambda i, *_: (jnp.minimum(i, CTX_TILES - 1), 0))


def _lat_spec(width):
    return pl.BlockSpec((TM, width), lambda i, *_: (jnp.maximum(i - CTX_TILES, 0), 0))


def _mod_spec(layer):
    return pl.BlockSpec((None, 1, 1, 6 * D_MODEL), lambda i, *_: (layer, _mod_row(i), 0, 0))


def _merge(layer, xc, xl, mod, hp, pw, ps, ymc, yml, ygc, ygl, sg, wbp, wbm, wbg, wo, lg, lb, rw, rb):
    tok, ctx, lat = _tok_spec, _ctx_spec, _lat_spec
    per_layer = functools.partial(_layer_spec, layer)
    return pl.pallas_call(
        _merge_kernel,
        grid=(N_TILES,),
        in_specs=[ctx(D_MODEL), lat(D_MODEL),
                  _mod_spec(layer),
                  pl.BlockSpec((TM, POOL_WIDTH), lambda i: (jnp.maximum(i - 1, 0), 0)),
                  tok(POOL_WIDTH),
                  pl.BlockSpec((TM, POOL_WIDTH), lambda i: (jnp.minimum(i + 1, N_TILES - 1), 0)),
                  per_layer((len(POOL_WINDOWS), POOL_GROUP, POOL_GROUP)),
                  per_layer((1, POOL_WIDTH)),
                  ctx(512), lat(512), ctx(512), lat(512), tok(3 * D_MODEL),
                  per_layer((POOL_WIDTH, D_MODEL)), per_layer((512, D_MODEL)), per_layer((512, D_MODEL)),
                  per_layer((D_MODEL, D_MODEL)), per_layer((1, D_MODEL)), per_layer((1, D_MODEL)),
                  per_layer((D_MODEL, LANES)), per_layer((1, LANES))],
        out_specs=[tok(D_MODEL), tok(D_MODEL), tok(LANES), tok(LANES), _const_spec((1, LANES))],
        out_shape=[jax.ShapeDtypeStruct((T_ALL, D_MODEL), F32), jax.ShapeDtypeStruct((T_ALL, D_MODEL), F32),
                   jax.ShapeDtypeStruct((T_ALL, LANES), I32), jax.ShapeDtypeStruct((T_ALL, LANES), F32),
                   jax.ShapeDtypeStruct((1, LANES), F32)],
        scratch_shapes=[pltpu.VMEM((1, LANES), F32)],
        compiler_params=_cparams(1),
        name="merge_route",
    )(xc, xl, mod, hp, hp, hp, pw, ps, ymc, yml, ygc, ygl, sg, wbp, wbm, wbg, wo, lg, lb, rw, rb)


def _row_copy(src_ref, src_row, dst_ref, dst_row, sem):
    return pltpu.make_async_copy(src_ref.at[pl.ds(src_row, 1), :], dst_ref.at[pl.ds(dst_row, 1), :], sem)


ROWS_PER_ISSUE = 4
WAITS_PER_TRIP = 64


def _drain_rows(wait_one, n_rows):
    def trip(_, c):
        for _ in range(WAITS_PER_TRIP):
            wait_one()
        return c

    lax.fori_loop(0, n_rows // WAITS_PER_TRIP, trip, 0)


def _dispatch_kernel(dest_ref, zrow_ref, nu_ref, u_ref, xs_ref, zbuf, sem, zsem):
    i = pl.program_id(0)
    base = i * (TM * TOP_K)

    @pl.when(i == 0)
    def _():
        zbuf[...] = jnp.zeros_like(zbuf)

        def zero_tile(row):
            if not isinstance(row, int):
                row = pl.multiple_of(row, TME)
            return pltpu.make_async_copy(zbuf, xs_ref.at[pl.ds(row, TME), :], zsem)

        for e in range(N_EXPERTS):
            @pl.when(zrow_ref[e] >= 0)
            def _():
                zero_tile(zrow_ref[e]).start()

        def start_tail(j, c):
            zero_tile(j * TME).start()
            return c

        lax.fori_loop(nu_ref[0], NT_E, start_tail, 0)

        for e in range(N_EXPERTS):
            @pl.when(zrow_ref[e] >= 0)
            def _():
                zero_tile(0).wait()

        def wait_tail(j, c):
            zero_tile(0).wait()
            return c

        lax.fori_loop(nu_ref[0], NT_E, wait_tail, 0)

    def issue(g, c):
        for r in range(ROWS_PER_ISSUE):
            t = g * ROWS_PER_ISSUE + r
            for k in range(TOP_K):
                _row_copy(u_ref, t, xs_ref, dest_ref[base + t * TOP_K + k], sem).start(priority=k % 2)
        return c

    lax.fori_loop(0, TM // ROWS_PER_ISSUE, issue, 0)
    _drain_rows(lambda: _row_copy(u_ref, 0, xs_ref, 0, sem).wait(), TM * TOP_K)


def _dispatch(plan, u2):
    grid_spec = pltpu.PrefetchScalarGridSpec(
        num_scalar_prefetch=3,
        grid=(N_TILES,),
        in_specs=[pl.BlockSpec((TM, D_MODEL), lambda i, *_: (i, 0))],
        out_specs=pl.BlockSpec(memory_space=pl.ANY),
        scratch_shapes=[pltpu.VMEM((TME, D_MODEL), F32), pltpu.SemaphoreType.DMA(()), pltpu.SemaphoreType.DMA(())],
    )
    return pl.pallas_call(
        _dispatch_kernel,
        grid_spec=grid_spec,
        out_shape=jax.ShapeDtypeStruct((P_ROWS, D_MODEL), F32),
        compiler_params=_cparams(1),
        name="moe_dispatch",
    )(plan["dest"], plan["zero_row"], plan["n_used"], u2)


def _expert_kernel(layer, te_ref, nu_ref, slot_ref, first_ref, next_ref,
                   xs_ref, wgu_hbm, bgu_ref, wd_hbm, bd_ref, ys_ref, wgu_f, wd_f, wgu_s, wd_s, sems):
    i = pl.program_id(0)

    def weight_copies(e, s):
        return (pltpu.make_async_copy(wgu_hbm.at[layer, e], wgu_f.at[s], sems.at[0, s]),
                pltpu.make_async_copy(wd_hbm.at[layer, e], wd_f.at[s], sems.at[1, s]))

    @pl.when(first_ref[i] == 1)
    def _():
        s = slot_ref[i]

        @pl.when(i == 0)
        def _():
            for cp in weight_copies(te_ref[i], s):
                cp.start()

        for cp in weight_copies(te_ref[i], s):
            cp.wait()

        @pl.when(next_ref[i] >= 0)
        def _():
            for cp in weight_copies(next_ref[i], 1 - s):
                cp.start()

        wgu_s[...] = wgu_f[s].astype(BF16)
        wd_s[...] = wd_f[s].astype(BF16)

    @pl.when(i < nu_ref[0])
    def _():
        h = _dot(xs_ref[...].astype(BF16), wgu_s[...]) + bgu_ref[0, 0]
        glu = jnp.minimum(h[:, 0:D_FF], SWIGLU_LIMIT)
        lin = jnp.clip(h[:, D_FF:2 * D_FF], -SWIGLU_LIMIT, SWIGLU_LIMIT)
        a = glu * _sigmoid(SWIGLU_ALPHA * glu) * (lin + 1.0)
        ys_ref[...] = _dot(a.astype(BF16), wd_s[...]) + bd_ref[0, 0]

    @pl.when(i >= nu_ref[0])
    def _():
        ys_ref[...] = jnp.zeros_like(ys_ref)


def _experts(layer, plan, xs, w_gate_up, b_gate_up, w_down, b_down):
    row = lambda i, te, nu, *_: (jnp.minimum(i, nu[0] - 1), 0)
    bias = lambda i, te, *_: (layer, te[i], 0, 0)
    grid_spec = pltpu.PrefetchScalarGridSpec(
        num_scalar_prefetch=5,
        grid=(NT_E,),
        in_specs=[pl.BlockSpec((TME, D_MODEL), row),
                  pl.BlockSpec(memory_space=pl.ANY),
                  pl.BlockSpec((1, 1, 1, 2 * D_FF), bias),
                  pl.BlockSpec(memory_space=pl.ANY),
                  pl.BlockSpec((1, 1, 1, D_MODEL), bias)],
        out_specs=pl.BlockSpec((TME, D_MODEL), lambda i, *_: (i, 0)),
        scratch_shapes=[pltpu.VMEM((2, D_MODEL, 2 * D_FF), F32), pltpu.VMEM((2, D_FF, D_MODEL), F32),
                        pltpu.VMEM((D_MODEL, 2 * D_FF), BF16), pltpu.VMEM((D_FF, D_MODEL), BF16),
                        pltpu.SemaphoreType.DMA((2, 2))],
    )
    return pl.pallas_call(
        functools.partial(_expert_kernel, layer),
        grid_spec=grid_spec,
        out_shape=jax.ShapeDtypeStruct((P_ROWS, D_MODEL), F32),
        compiler_params=_cparams(1),
        name="moe_experts",
    )(plan["tile_expert"], plan["n_used"], plan["tile_slot"], plan["tile_first"], plan["tile_next"], xs, w_gate_up,
      b_gate_up.reshape(DEPTH, N_EXPERTS, 1, 2 * D_FF), w_down, b_down.reshape(DEPTH, N_EXPERTS, 1, D_MODEL))


def _combine_kernel(dest_ref, ys_ref, rwt_ref, x1_ref, mod_ref, lg_ref, lb_ref, oc_ref, ol_ref, buf, sems):
    i = pl.program_id(0)
    slot = jnp.bitwise_and(i, 1)

    def gather(tile, s):
        base = tile * (TM * TOP_K)

        def issue(g, c):
            for r in range(ROWS_PER_ISSUE):
                t = g * ROWS_PER_ISSUE + r
                for k in range(TOP_K):
                    _row_copy(ys_ref, dest_ref[base + t * TOP_K + k], buf.at[s, k], t, sems.at[s]).start(priority=k % 2)
            return c

        lax.fori_loop(0, TM // ROWS_PER_ISSUE, issue, 0)

    @pl.when(i == 0)
    def _():
        gather(0, 0)

    @pl.when(i + 1 < pl.num_programs(0))
    def _():
        gather(i + 1, 1 - slot)

    _drain_rows(lambda: _row_copy(ys_ref, 0, buf.at[slot, 0], 0, sems.at[slot]).wait(), TM * TOP_K)

    moe = functools.reduce(jnp.add, [rwt_ref[:, k:k + 1] * buf[slot, k] for k in range(TOP_K)])
    gate2 = mod_ref[0, :, 5 * D_MODEL:6 * D_MODEL]
    out = _layer_norm(ALPHA_DN * x1_ref[...] + gate2 * moe, lg_ref[...], lb_ref[...])

    @pl.when(i < CTX_TILES)
    def _():
        oc_ref[...] = out

    @pl.when(i >= CTX_TILES)
    def _():
        ol_ref[...] = out


def _combine(layer, dest, ys, rwt, x1, mod, lg, lb):
    grid_spec = pltpu.PrefetchScalarGridSpec(
        num_scalar_prefetch=1,
        grid=(N_TILES,),
        in_specs=[pl.BlockSpec(memory_space=pl.ANY),
                  pl.BlockSpec((TM, LANES), lambda i, d: (i, 0)),
                  pl.BlockSpec((TM, D_MODEL), lambda i, d: (i, 0)),
                  _mod_spec(layer),
                  _layer_spec(layer, (1, D_MODEL)),
                  _layer_spec(layer, (1, D_MODEL))],
        out_specs=[_ctx_spec(D_MODEL), _lat_spec(D_MODEL)],
        scratch_shapes=[pltpu.VMEM((2, TOP_K, TM, D_MODEL), F32), pltpu.SemaphoreType.DMA((2,))],
    )
    return pl.pallas_call(
        _combine_kernel,
        grid_spec=grid_spec,
        out_shape=[jax.ShapeDtypeStruct((T_CTX, D_MODEL), F32), jax.ShapeDtypeStruct((T_LAT, D_MODEL), F32)],
        compiler_params=_cparams(1),
        name="moe_combine",
    )(dest, ys, rwt, x1, mod, lg, lb)


def _pad_heads(w, n_heads, width):
    lead = w.shape[:-1]
    w = w.reshape(lead + (n_heads, width))
    pad = ((0, 0),) * (len(lead) + 1) + ((0, LANES - width),)
    return jnp.pad(w, pad).reshape(lead + (n_heads * LANES,))


def _dup_heads(w):
    lead = w.shape[:-1]
    w = w.reshape(lead + (GQA_KV, 1, GQA_HD))
    return jnp.concatenate([w, w], axis=-2).reshape(lead + (GQA_KV * LANES,))


def _pack_w_in(w):
    cuts = [0, 512, 896, 1152, 1184, 1696, 1824, 1952, 5024]
    hp, cq, ckv, kr, q, k, v, g = [w[..., a:b] for a, b in zip(cuts[:-1], cuts[1:])]
    packed = jnp.concatenate([hp, cq, ckv, _pad_heads(kr, 1, MLA_ROPE), q, _dup_heads(k), _dup_heads(v), g], axis=-1)
    return packed.astype(BF16)


def _pack_mla(wuq, wukv):
    wuq_p = _pad_heads(wuq, MLA_HEADS, MLA_QK).astype(BF16)
    kv = wukv.reshape(DEPTH, MLA_KV_LORA, MLA_HEADS, MLA_NOPE + MLA_V)
    wk_top = _pad_heads(kv[..., :MLA_NOPE].reshape(DEPTH, MLA_KV_LORA, MLA_HEADS * MLA_NOPE), MLA_HEADS, MLA_NOPE)
    place = jnp.zeros((LANES, MLA_HEADS, LANES), F32)
    r = jnp.arange(MLA_ROPE)
    place = place.at[r, :, MLA_NOPE + r].set(1.0).reshape(1, LANES, MLA_HEADS * LANES)
    wk = jnp.concatenate([wk_top, jnp.broadcast_to(place, (DEPTH,) + place.shape[1:])], axis=1).astype(BF16)
    wv = kv[..., MLA_NOPE:].reshape(DEPTH, MLA_KV_LORA, MLA_HEADS * MLA_V).astype(BF16)
    return wuq_p, wk, wv


def _rope_table(dim, lane0):
    quarter = dim // 4
    pos = jnp.arange(S_LAT)
    rows = (pos // GRID_W).astype(F32)
    cols = (pos % GRID_W).astype(F32)
    freqs = jnp.power(ROPE_THETA, -jnp.arange(quarter, dtype=F32) / quarter)
    ang_r = rows[:, None] * freqs[None, :]
    ang_c = cols[:, None] * freqs[None, :]
    zero = jnp.zeros((S_LAT, quarter), F32)
    cos = jnp.concatenate([jnp.cos(ang_r), jnp.cos(ang_r), jnp.cos(ang_c), jnp.cos(ang_c)], axis=1)
    s_up = jnp.concatenate([-jnp.sin(ang_r), zero, -jnp.sin(ang_c), zero], axis=1)
    s_dn = jnp.concatenate([zero, jnp.sin(ang_r), zero, jnp.sin(ang_c)], axis=1)

    def place(t, fill):
        if lane0 is None:
            return jnp.tile(t, (1, LANES // dim))
        full = jnp.full((S_LAT, LANES), fill, F32)
        return full.at[:, lane0:lane0 + dim].set(t)

    planes = jnp.stack([place(cos, 1.0), place(s_up, 0.0), place(s_dn, 0.0)])
    ident = jnp.stack([jnp.ones((TM, LANES), F32), jnp.zeros((TM, LANES), F32), jnp.zeros((TM, LANES), F32)])
    return jnp.concatenate([planes, ident], axis=1)


TILE_LANES = 2 * LANES
ROW_EXPERT, ROW_SLOT, ROW_FIRST, ROW_NEXT, ROW_ZERO, ROW_USED = range(6)


def _plan_kernel(ri_ref, cnt_ref, dest_ref, tiles_ref):
    lane = lax.broadcasted_iota(I32, (1, LANES), 1)
    r_i = lax.broadcasted_iota(I32, (LANES, LANES), 0)
    c_i = lax.broadcasted_iota(I32, (LANES, LANES), 1)
    upto = jnp.where(r_i <= c_i, 1.0, 0.0).astype(BF16)

    def prefix(row):
        return _dot(jnp.broadcast_to(row, (SUBLANES, LANES)).astype(BF16), upto)[0:1, :]

    cnt = cnt_ref[...]
    t_cnt = jnp.floor((cnt + (TME - 1)) * (1.0 / TME))
    t_end = prefix(t_cnt)
    t_start = t_end - t_cnt
    n_used = jnp.max(t_end, axis=-1, keepdims=True)

    tok_lane = lax.broadcasted_iota(I32, ri_ref.shape, 1)
    ri = ri_ref[...]
    dest = jnp.zeros(ri_ref.shape, I32)
    for k in range(TOP_K):
        start_k = jnp.sum(jnp.where(tok_lane == ri[:, k:k + 1], t_start, 0.0), axis=-1, keepdims=True)
        dest_k = start_k.astype(I32) * TME + ri[:, TOP_K + k:TOP_K + k + 1]
        dest = jnp.where(tok_lane == k, dest_k, dest)
    dest_ref[...] = dest

    @pl.when(pl.program_id(0) == 0)
    def _():
        eye = jnp.where(r_i == c_i, 1.0, 0.0).astype(BF16)

        def to_col(row):
            return _dot_t(eye, jnp.broadcast_to(row, (SUBLANES, LANES)).astype(BF16))[:, 0:1]

        e_col = lax.broadcasted_iota(I32, (LANES, 1), 0)
        real_e = e_col < N_EXPERTS
        tile = lax.broadcasted_iota(I32, (LANES, TILE_LANES), 1).astype(F32)
        end_col = to_col(t_end)
        expert = jnp.sum(jnp.where(real_e & (end_col <= tile), 1.0, 0.0), axis=0, keepdims=True)
        last = jnp.sum(jnp.where(real_e & (end_col <= n_used - 1.0), 1.0, 0.0), axis=0, keepdims=True)
        expert = jnp.minimum(expert, last)
        mine = e_col.astype(F32) == expert

        def per_tile(col):
            return jnp.sum(jnp.where(mine, col, 0.0), axis=0, keepdims=True)

        has = t_cnt > 0.0
        ordinal = prefix(jnp.where(has, 1.0, 0.0)) - 1.0
        ord_col = to_col(ordinal)
        slot = per_tile(ord_col - 2.0 * jnp.floor(ord_col * 0.5))
        tile_row = tile[0:1, :]
        first = jnp.where((tile_row < n_used) & (tile_row == per_tile(to_col(t_start))), 1.0, 0.0)
        later = jnp.where(has & (lane > e_col), lane.astype(F32), float(LANES))
        nxt_col = jnp.min(later, axis=-1, keepdims=True)
        nxt_col = jnp.where(nxt_col == float(LANES), -1.0, nxt_col)
        nxt = per_tile(nxt_col)
        zero_row = jnp.where(has, (t_end - 1.0) * TME, -1.0)
        zero_row = jnp.concatenate([zero_row, jnp.full((1, TILE_LANES - LANES), -1.0, F32)], axis=1)
        used = jnp.broadcast_to(n_used, (1, TILE_LANES))
        pad = jnp.zeros((SUBLANES - 6, TILE_LANES), F32)
        tiles_ref[...] = jnp.concatenate([expert, slot, first, nxt, zero_row, used, pad], axis=0).astype(I32)


def _routing_plan(ri, counts):
    rows = 4 * TM
    dest, tiles = pl.pallas_call(
        _plan_kernel,
        grid=(T_ALL // rows,),
        in_specs=[pl.BlockSpec((rows, LANES), lambda i: (i, 0)), _const_spec((1, LANES))],
        out_specs=[pl.BlockSpec((rows, LANES), lambda i: (i, 0)), _const_spec((SUBLANES, TILE_LANES))],
        out_shape=[jax.ShapeDtypeStruct((T_ALL, LANES), I32), jax.ShapeDtypeStruct((SUBLANES, TILE_LANES), I32)],
        compiler_params=_cparams(1),
        name="moe_plan",
    )(ri, counts)
    return {"dest": dest[:, 0:TOP_K].reshape(-1), "tile_expert": tiles[ROW_EXPERT, :NT_E],
            "n_used": tiles[ROW_USED, 0:1], "tile_slot": tiles[ROW_SLOT, :NT_E], "tile_first": tiles[ROW_FIRST, :NT_E],
            "tile_next": tiles[ROW_NEXT, :NT_E], "zero_row": tiles[ROW_ZERO, :N_EXPERTS]}


def kernel(x_prompt, x_sample, c, cache_mla_ckv, cache_mla_krope, cache_gqa_k, cache_gqa_v, c_ctx, w_ada, b_ada, w_in, mla_q_norm, mla_kv_norm, w_mla_uq, w_mla_ukv, gqa_sink, pool_w, pool_scale, w_branch_pool, w_branch_mla, w_branch_gqa, w_out, ln1_g, ln1_b, ln2_g, ln2_b, router_w, router_b, w_gate_up, b_gate_up, w_down, b_down):
    xc = x_prompt.reshape(T_CTX, D_MODEL)
    xl = x_sample.reshape(T_LAT, D_MODEL)
    cond8 =jnp.concatenate([c_ctx[None, :], c, jnp.zeros((SUBLANES - 1 - B_LAT, D_MODEL), F32)], axis=0)
    mod_all = _ada_params(cond8, w_ada, b_ada)

    tab_g = _rope_table(GQA_HD, None)
    tab_q = _rope_table(MLA_ROPE, MLA_NOPE)
    tab_k = _rope_table(MLA_ROPE, 0)

    mod_all = mod_all.reshape(DEPTH, SUBLANES, 1, 6 * D_MODEL)
    wp = _pack_w_in(w_in)
    wuq_p, wk, wv = _pack_mla(w_mla_uq, w_mla_ukv)
    row = lambda a: a.reshape(DEPTH, 1, a.shape[-1])
    q_norm, kv_norm, pool_s = row(mla_q_norm), row(mla_kv_norm), row(pool_scale)
    ln1g, ln1b, ln2g, ln2b = row(ln1_g), row(ln1_b), row(ln2_g), row(ln2_b)
    pool_wb = pool_w.astype(BF16)
    wbp, wbm, wbg, wo = (w.astype(BF16) for w in (w_branch_pool, w_branch_mla, w_branch_gqa, w_out))
    rw = jnp.pad(router_w, ((0, 0), (0, 0), (0, LANES - N_EXPERTS)))
    rb = jnp.pad(router_b, ((0, 0), (0, LANES - N_EXPERTS)), constant_values=-jnp.inf).reshape(DEPTH, 1, LANES)
    sinks = gqa_sink.reshape(DEPTH * GQA_HEADS)
    n_cache = B_LAT * DEPTH * PAST_LEN
    ckv_c = cache_mla_ckv.reshape(n_cache, MLA_KV_LORA)
    kr_c = jnp.pad(cache_mla_krope.reshape(n_cache, MLA_ROPE), ((0, 0), (0, LANES - MLA_ROPE)))
    gk_c = _dup_heads(cache_gqa_k.reshape(n_cache, GQA_KV * GQA_HD)).astype(BF16)
    gv_c = _dup_heads(cache_gqa_v.reshape(n_cache, GQA_KV * GQA_HD)).astype(BF16)

    outs = {"ckv": [], "kr": [], "gk": [], "gv": []}
    for l in range(DEPTH):
        (hp, qm, ckv, kr, km, vm, qg, kg, vg, nk, nv, sg) = _in_proj(
            l, xc, xl, mod_all, wp, q_norm, kv_norm, wuq_p, wk, wv, tab_g, tab_q, tab_k)
        outs["ckv"].append(ckv[:T_CTX].reshape(B_CTX, S_CTX, MLA_KV_LORA))
        outs["kr"].append(kr[:T_CTX, :MLA_ROPE].reshape(B_CTX, S_CTX, MLA_ROPE))
        outs["gk"].append(nk[:T_CTX].reshape(B_CTX, S_CTX, GQA_KV, GQA_HD))
        outs["gv"].append(nv[:T_CTX].reshape(B_CTX, S_CTX, GQA_KV, GQA_HD))

        km_c, vm_c = _expand_cache(l, ckv_c, kr_c, wk, wv)
        ymc = _mla_ctx(qm, km, vm)
        yml = _mla_lat(qm, km, vm, km_c, vm_c)
        ygc = _gqa_ctx(l, sinks, qg, kg, vg)
        ygl = _gqa_lat(l, sinks, qg, kg, vg, gk_c, gv_c)

        x1, u2, ri, rwt, counts = _merge(
            l, xc, xl, mod_all, hp, pool_wb, pool_s, ymc, yml, ygc, ygl, sg, wbp, wbm, wbg, wo, ln1g, ln1b, rw, rb)

        plan = _routing_plan(ri, counts)
        xs = _dispatch(plan, u2)
        ys = _experts(l, plan, xs, w_gate_up, b_gate_up, w_down, b_down)
        xc, xl = _combine(l, plan["dest"], ys, rwt, x1, mod_all, ln2g, ln2b)

    y_prompt = xc.reshape(B_CTX, S_CTX, D_MODEL)
    y_sample = xl.reshape(B_LAT, S_LAT, D_MODEL)
    return (y_prompt, y_sample, jnp.stack(outs["ckv"], axis=1), jnp.stack(outs["kr"], axis=1),
            jnp.stack(outs["gk"], axis=1), jnp.stack(outs["gv"], axis=1))
```

```python
import functools
import math

import jax
import jax.numpy as jnp
from jax import lax
from jax.experimental import pallas as pl
from jax.experimental.pallas import tpu as pltpu

F32 = jnp.float32
BF16 = jnp.bfloat16
I32 = jnp.int32

D_MODEL = 1024
DEPTH = 2
B_CTX, S_CTX = 16, 256
B_LAT, S_LAT = 4, 1024
PAST_LEN = 512
GRID_W = 64
ROPE_THETA = 10000.0
ALPHA_DN = (2 * DEPTH) ** 0.25
LN_EPS = 1e-5
RMS_EPS = 1e-6
NEG_INF = -1e30
POOL_WINDOWS = (2, 4, 8, 16)
POOL_GROUP = 128
POOL_WIDTH = 512
MLA_HEADS = 8
MLA_NOPE = 64
MLA_ROPE = 32
MLA_V = 64
MLA_QK = MLA_NOPE + MLA_ROPE
MLA_Q_LORA = 384
MLA_KV_LORA = 256
MLA_SCALE = 1.0 / math.sqrt(MLA_QK)
GQA_HEADS = 8
GQA_KV = 2
GQA_HD = 64
GQA_GROUP = GQA_HEADS // GQA_KV
GQA_SCALE = 1.0 / math.sqrt(GQA_HD)
WBLK = 128
N_EXPERTS = 32
TOP_K = 4
D_FF = 1024
SWIGLU_LIMIT = 7.0
SWIGLU_ALPHA = 1.702

LANES = 128
SUBLANES = 8
VMEM_LIMIT = 56 * 1024 * 1024

T_CTX = B_CTX * S_CTX
T_LAT = B_LAT * S_LAT
T_ALL = T_CTX + T_LAT
TM = 256
N_TILES = T_ALL // TM
CTX_TILES = T_CTX // TM
LAT_TILES_PER_SEQ = S_LAT // TM
HALO = 16
MLA_LAT_TQ = 512

C_HP = 0
C_CQ = C_HP + POOL_WIDTH
C_CKV = C_CQ + MLA_Q_LORA
C_KR = C_CKV + MLA_KV_LORA
C_GQ = C_KR + LANES
C_GK = C_GQ + GQA_HEADS * GQA_HD
C_GV = C_GK + GQA_KV * LANES
C_G = C_GV + GQA_KV * LANES
C_END = C_G + 3 * D_MODEL

TME = 256
N_PAIRS = T_ALL * TOP_K
NT_E = N_PAIRS // TME + N_EXPERTS
P_ROWS = NT_E * TME


def _cparams(n_axes):
    return pltpu.CompilerParams(dimension_semantics=("arbitrary",) * n_axes, vmem_limit_bytes=VMEM_LIMIT)


def _dot(a, b):
    return jnp.dot(a, b, preferred_element_type=F32)


def _dot_t(a, b):
    return lax.dot_general(a, b, (((1,), (1,)), ((), ())), preferred_element_type=F32)


def _split(x):
    hi = x.astype(BF16)
    lo = (x - hi.astype(F32)).astype(BF16)
    return hi, lo


def _dot3(a, b):
    a_hi, a_lo = _split(a)
    b_hi, b_lo = _split(b)
    return _dot(a_hi, b_hi) + _dot(a_hi, b_lo) + _dot(a_lo, b_hi)


def _sigmoid(x):
    return 1.0 / (1.0 + jnp.exp(-x))


def _layer_norm(z, g, b):
    mu = jnp.mean(z, axis=-1, keepdims=True)
    zc = z - mu
    var = jnp.mean(zc * zc, axis=-1, keepdims=True)
    return zc * lax.rsqrt(var + LN_EPS) * g + b


def _mod_row(i):
    return jnp.where(i < CTX_TILES, 0, 1 + (i - CTX_TILES) // LAT_TILES_PER_SEQ)


def _pos_block(i):
    return jnp.where(i < CTX_TILES, LAT_TILES_PER_SEQ, (i - CTX_TILES) % LAT_TILES_PER_SEQ)


def _const_spec(shape):
    nd = len(shape)
    return pl.BlockSpec(shape, lambda *_: (0,) * nd)


def _layer_spec(layer, shape):
    nd = len(shape)
    return pl.BlockSpec((None,) + tuple(shape), lambda *_: (layer,) + (0,) * nd)


def _ada_kernel(c_ref, w_ref, b_ref, o_ref):
    c = c_ref[...]
    s = c * _sigmoid(c)
    o_ref[0] = _dot3(s, w_ref[0]) + b_ref[0]


def _ada_params(cond8, w_ada, b_ada):
    n_col = 6 * D_MODEL
    blk = 1024
    return pl.pallas_call(
        _ada_kernel,
        grid=(DEPTH, n_col // blk),
        in_specs=[
            pl.BlockSpec((SUBLANES, D_MODEL), lambda l, j: (0, 0)),
            pl.BlockSpec((1, D_MODEL, blk), lambda l, j: (l, 0, j)),
            pl.BlockSpec((1, 1, blk), lambda l, j: (l, 0, j)),
        ],
        out_specs=pl.BlockSpec((1, SUBLANES, blk), lambda l, j: (l, 0, j)),
        out_shape=jax.ShapeDtypeStruct((DEPTH, SUBLANES, n_col), F32),
        compiler_params=_cparams(2),
        name="ada_params",
    )(cond8, w_ada, b_ada.reshape(DEPTH, 1, n_col))


def _rope(x, tab_ref, shift):
    return (x * tab_ref[0]
            + pltpu.roll(x, LANES - shift, 1) * tab_ref[1]
            + pltpu.roll(x, shift, 1) * tab_ref[2])


def _rms(h, g):
    return h * lax.rsqrt(jnp.mean(h * h, axis=-1, keepdims=True) + RMS_EPS) * g


def _in_kernel(xc_ref, xl_ref, mod_ref, wp_ref, qn_ref, kvn_ref, wuq_ref, wk_ref, wv_ref, tg_ref, tq_ref, tk_ref,
               hp_ref, qm_ref, ckv_ref, kr_ref, km_ref, vm_ref, qg_ref, kg_ref, vg_ref, nk_ref, nv_ref, sg_ref):
    x = jnp.where(pl.program_id(0) < CTX_TILES, xc_ref[...], xl_ref[...])
    sh = mod_ref[0, :, 0:D_MODEL]
    sc = mod_ref[0, :, D_MODEL:2 * D_MODEL]
    u = (x * (1.0 + sc) + sh).astype(BF16)

    def proj(lo, hi):
        return _dot(u, wp_ref[:, lo:hi])

    hp_ref[...] = proj(C_HP, C_CQ)

    qn = _rms(proj(C_CQ, C_CKV), qn_ref[...]).astype(BF16)
    q = _dot(qn, wuq_ref[...]) * MLA_SCALE
    for h in range(MLA_HEADS):
        qm_ref[:, h * LANES:(h + 1) * LANES] = _rope(q[:, h * LANES:(h + 1) * LANES], tq_ref, MLA_ROPE // 4).astype(BF16)

    ckv = _rms(proj(C_CKV, C_KR), kvn_ref[...])
    ckv_ref[...] = ckv
    kr = _rope(proj(C_KR, C_GQ), tk_ref, MLA_ROPE // 4)
    kr_ref[...] = kr
    ckv_b = ckv.astype(BF16)
    km_ref[...] = (_dot(ckv_b, wk_ref[0:MLA_KV_LORA, :]) + _dot(kr.astype(BF16), wk_ref[MLA_KV_LORA:, :])).astype(BF16)
    vm_ref[...] = _dot(ckv_b, wv_ref[...]).astype(BF16)

    hq = proj(C_GQ, C_GK) * GQA_SCALE
    for j in range(GQA_HEADS * GQA_HD // LANES):
        qg_ref[:, j * LANES:(j + 1) * LANES] = _rope(hq[:, j * LANES:(j + 1) * LANES], tg_ref, GQA_HD // 4).astype(BF16)
    hk = proj(C_GK, C_GV)
    hv = proj(C_GV, C_G)
    for g in range(GQA_KV):
        kg_ref[:, g * LANES:(g + 1) * LANES] = _rope(hk[:, g * LANES:(g + 1) * LANES], tg_ref, GQA_HD // 4).astype(BF16)
    vg_ref[...] = hv.astype(BF16)
    low_half = lax.broadcasted_iota(I32, (TM, LANES), 1) < GQA_HD
    nk_ref[...] = jnp.where(low_half, hk[:, 0:LANES], hk[:, LANES:2 * LANES])
    nv_ref[...] = jnp.where(low_half, hv[:, 0:LANES], hv[:, LANES:2 * LANES])

    for j in range(3):
        sg_ref[:, j * D_MODEL:(j + 1) * D_MODEL] = _sigmoid(proj(C_G + j * D_MODEL, C_G + (j + 1) * D_MODEL)).astype(BF16)


def _in_proj(layer, xc, xl, mod, wp, qn, kvn, wuq, wk, wv, tab_g, tab_q, tab_k):
    tok = lambda w: pl.BlockSpec((TM, w), lambda i: (i, 0))
    per_layer = functools.partial(_layer_spec, layer)
    x_specs = [pl.BlockSpec((TM, D_MODEL), lambda i: (jnp.minimum(i, CTX_TILES - 1), 0)),
               pl.BlockSpec((TM, D_MODEL), lambda i: (jnp.maximum(i - CTX_TILES, 0), 0))]
    tab = pl.BlockSpec((3, TM, LANES), lambda i: (0, _pos_block(i), 0))
    out_widths = [(POOL_WIDTH, F32), (MLA_HEADS * LANES, BF16), (MLA_KV_LORA, F32), (LANES, F32),
                  (MLA_HEADS * LANES, BF16), (MLA_HEADS * MLA_V, BF16), (GQA_HEADS * GQA_HD, BF16),
                  (GQA_KV * LANES, BF16), (GQA_KV * LANES, BF16), (GQA_KV * GQA_HD, F32), (GQA_KV * GQA_HD, F32),
                  (3 * D_MODEL, BF16)]
    return pl.pallas_call(
        _in_kernel,
        grid=(N_TILES,),
        in_specs=x_specs + [
            pl.BlockSpec((None, 1, 1, 6 * D_MODEL), lambda i: (layer, _mod_row(i), 0, 0)),
            per_layer((D_MODEL, C_END)),
            per_layer((1, MLA_Q_LORA)),
            per_layer((1, MLA_KV_LORA)),
            per_layer((MLA_Q_LORA, MLA_HEADS * LANES)),
            per_layer((MLA_KV_LORA + LANES, MLA_HEADS * LANES)),
            per_layer((MLA_KV_LORA, MLA_HEADS * MLA_V)),
            tab, tab, tab,
        ],
        out_specs=[tok(w) for w, _ in out_widths],
        out_shape=[jax.ShapeDtypeStruct((T_ALL, w), dt) for w, dt in out_widths],
        compiler_params=_cparams(1),
        name="in_proj",
    )(xc, xl, mod, wp, qn, kvn, wuq, wk, wv, tab_g, tab_q, tab_k)


def _expand_kernel(ckv_ref, kr_ref, wk_ref, wv_ref, km_ref, vm_ref):
    ckv_b = ckv_ref[...].astype(BF16)
    km_ref[...] = (_dot(ckv_b, wk_ref[0:MLA_KV_LORA, :]) + _dot(kr_ref[...].astype(BF16), wk_ref[MLA_KV_LORA:, :])).astype(BF16)
    vm_ref[...] = _dot(ckv_b, wv_ref[...]).astype(BF16)


def _cache_block(layer):
    return lambda b, *_: (b * DEPTH + layer, 0)


def _expand_cache(layer, ckv, kr_pad, wk, wv):
    rows = B_LAT * PAST_LEN
    blk = PAST_LEN
    return pl.pallas_call(
        _expand_kernel,
        grid=(B_LAT,),
        in_specs=[
            pl.BlockSpec((blk, MLA_KV_LORA), _cache_block(layer)),
            pl.BlockSpec((blk, LANES), _cache_block(layer)),
            _layer_spec(layer, (MLA_KV_LORA + LANES, MLA_HEADS * LANES)),
            _layer_spec(layer, (MLA_KV_LORA, MLA_HEADS * MLA_V)),
        ],
        out_specs=[pl.BlockSpec((blk, MLA_HEADS * LANES), lambda i: (i, 0)),
                   pl.BlockSpec((blk, MLA_HEADS * MLA_V), lambda i: (i, 0))],
        out_shape=[jax.ShapeDtypeStruct((rows, MLA_HEADS * LANES), BF16),
                   jax.ShapeDtypeStruct((rows, MLA_HEADS * MLA_V), BF16)],
        compiler_params=_cparams(1),
        name="mla_expand_cache",
    )(ckv, kr_pad, wk, wv)


def _lane_chunks(x):
    return [x[:, c:c + LANES] for c in range(0, x.shape[1], LANES)]


def _softmax_sets(ss, extra=None):
    m = jnp.max(functools.reduce(jnp.maximum, [c for s in ss for c in _lane_chunks(s)]), axis=-1, keepdims=True)
    if extra is not None:
        m = jnp.maximum(m, extra)
    ps = [jnp.exp(s - m) for s in ss]
    den = jnp.sum(functools.reduce(jnp.add, [c for p in ps for c in _lane_chunks(p)]), axis=-1, keepdims=True)
    if extra is not None:
        den = den + jnp.exp(extra - m)
    return ps, den


def _mla_kernel(n_sets, q_ref, *refs):
    k_refs = refs[:n_sets]
    v_refs = refs[n_sets:2 * n_sets]
    o_ref = refs[2 * n_sets]
    rows = q_ref.shape[0]
    low_half = lax.broadcasted_iota(I32, (rows, LANES), 1) < MLA_V
    for j in range(MLA_HEADS // 2):
        outs = []
        for h in (2 * j, 2 * j + 1):
            qh = q_ref[:, h * LANES:(h + 1) * LANES]
            ps, den = _softmax_sets([_dot_t(qh, k[:, h * LANES:(h + 1) * LANES]) for k in k_refs])
            o = functools.reduce(jnp.add, [_dot(p.astype(BF16), v[:, j * LANES:(j + 1) * LANES]) for p, v in zip(ps, v_refs)])
            outs.append(o / den)
        o_ref[:, j * LANES:(j + 1) * LANES] = jnp.where(low_half, outs[0], outs[1]).astype(BF16)


def _mla_ctx(qm, km, vm):
    blk = S_CTX
    return pl.pallas_call(
        functools.partial(_mla_kernel, 1),
        grid=(B_CTX,),
        in_specs=[pl.BlockSpec((blk, MLA_HEADS * LANES), lambda b: (b, 0)),
                  pl.BlockSpec((blk, MLA_HEADS * LANES), lambda b: (b, 0)),
                  pl.BlockSpec((blk, MLA_HEADS * MLA_V), lambda b: (b, 0))],
        out_specs=pl.BlockSpec((blk, MLA_HEADS * MLA_V), lambda b: (b, 0)),
        out_shape=jax.ShapeDtypeStruct((T_CTX, MLA_HEADS * MLA_V), BF16),
        compiler_params=_cparams(1),
        name="mla_attn_ctx",
    )(qm, km, vm)


def _mla_lat(qm, km, vm, km_c, vm_c):
    lat0 = T_CTX // S_LAT
    tq = MLA_LAT_TQ
    n_q = S_LAT // tq
    return pl.pallas_call(
        functools.partial(_mla_kernel, 2),
        grid=(B_LAT, n_q),
        in_specs=[pl.BlockSpec((tq, MLA_HEADS * LANES), lambda b, t: (T_CTX // tq + b * n_q + t, 0)),
                  pl.BlockSpec((PAST_LEN, MLA_HEADS * LANES), lambda b, t: (b, 0)),
                  pl.BlockSpec((S_LAT, MLA_HEADS * LANES), lambda b, t: (lat0 + b, 0)),
                  pl.BlockSpec((PAST_LEN, MLA_HEADS * MLA_V), lambda b, t: (b, 0)),
                  pl.BlockSpec((S_LAT, MLA_HEADS * MLA_V), lambda b, t: (lat0 + b, 0))],
        out_specs=pl.BlockSpec((tq, MLA_HEADS * MLA_V), lambda b, t: (b * n_q + t, 0)),
        out_shape=jax.ShapeDtypeStruct((T_LAT, MLA_HEADS * MLA_V), BF16),
        compiler_params=_cparams(2),
        name="mla_attn_lat",
    )(qm, km_c, km, vm_c, vm)


def _gqa_kernel(band, layer, sink_ref, q_ref, *refs):
    n_sets = 4 if band else 1
    k_refs = refs[:n_sets]
    v_refs = refs[n_sets:2 * n_sets]
    o_ref = refs[2 * n_sets]
    rows = q_ref.shape[0]
    stack = 2 if band else 1
    low_half = lax.broadcasted_iota(I32, (rows, LANES), 1) < GQA_HD
    masks = [None] * n_sets
    if band:
        n = pl.program_id(1)
        n_blk = pl.num_programs(1)
        qi = lax.broadcasted_iota(I32, (stack * rows, WBLK), 0) % rows
        kj = lax.broadcasted_iota(I32, (stack * rows, WBLK), 1)
        masks[0] = kj >= qi + jnp.where(n > 0, 0, WBLK)
        masks[2] = kj <= qi - jnp.where(n < n_blk - 1, 0, WBLK)
    zero = jnp.zeros((rows, LANES), BF16)
    for j in range(GQA_HEADS // 2):
        g = j // (GQA_GROUP // 2)
        blk = q_ref[:, j * LANES:(j + 1) * LANES]
        q_heads = [jnp.where(low_half, blk, zero), jnp.where(low_half, zero, blk)]
        outs = []
        for h0 in range(0, 2, stack):
            qs = jnp.concatenate(q_heads[h0:h0 + stack], axis=0)
            sink = jnp.concatenate([jnp.full((rows, 1), sink_ref[layer * GQA_HEADS + 2 * j + h0 + i], F32)
                                    for i in range(stack)], axis=0)
            ss = []
            for k, msk in zip(k_refs, masks):
                s = _dot_t(qs, k[:, g * LANES:(g + 1) * LANES])
                ss.append(s if msk is None else jnp.where(msk, s, NEG_INF))
            ps, den = _softmax_sets(ss, sink)
            o = functools.reduce(jnp.add, [_dot(p.astype(BF16), v[:, g * LANES:(g + 1) * LANES]) for p, v in zip(ps, v_refs)])
            o = o / den
            outs += [o[i * rows:(i + 1) * rows] for i in range(stack)]
        o_ref[:, j * LANES:(j + 1) * LANES] = jnp.where(low_half, outs[0], outs[1]).astype(BF16)


def _gqa_ctx(layer, sink, qg, kg, vg):
    blk = S_CTX
    grid_spec = pltpu.PrefetchScalarGridSpec(
        num_scalar_prefetch=1,
        grid=(B_CTX,),
        in_specs=[pl.BlockSpec((blk, GQA_HEADS * GQA_HD), lambda b, s: (b, 0)),
                  pl.BlockSpec((blk, GQA_KV * LANES), lambda b, s: (b, 0)),
                  pl.BlockSpec((blk, GQA_KV * LANES), lambda b, s: (b, 0))],
        out_specs=pl.BlockSpec((blk, GQA_HEADS * GQA_HD), lambda b, s: (b, 0)),
    )
    return pl.pallas_call(
        functools.partial(_gqa_kernel, False, layer),
        grid_spec=grid_spec,
        out_shape=jax.ShapeDtypeStruct((T_CTX, GQA_HEADS * GQA_HD), BF16),
        compiler_params=_cparams(1),
        name="gqa_attn_ctx",
    )(sink, qg, kg, vg)


def _gqa_lat(layer, sink, qg, kg, vg, kg_c, vg_c):
    nb = S_LAT // WBLK
    first = T_CTX // WBLK

    def blk(off):
        def index_map(b, n, s):
            return (first + b * nb + jnp.clip(n + off, 0, nb - 1), 0)
        return index_map

    kv_w = GQA_KV * LANES
    grid_spec = pltpu.PrefetchScalarGridSpec(
        num_scalar_prefetch=1,
        grid=(B_LAT, nb),
        in_specs=[pl.BlockSpec((WBLK, GQA_HEADS * GQA_HD), blk(0)),
                  pl.BlockSpec((WBLK, kv_w), blk(-1)), pl.BlockSpec((WBLK, kv_w), blk(0)), pl.BlockSpec((WBLK, kv_w), blk(1)),
                  pl.BlockSpec((PAST_LEN, kv_w), _cache_block(layer)),
                  pl.BlockSpec((WBLK, kv_w), blk(-1)), pl.BlockSpec((WBLK, kv_w), blk(0)), pl.BlockSpec((WBLK, kv_w), blk(1)),
                  pl.BlockSpec((PAST_LEN, kv_w), _cache_block(layer))],
        out_specs=pl.BlockSpec((WBLK, GQA_HEADS * GQA_HD), lambda b, n, s: (b * nb + n, 0)),
    )
    return pl.pallas_call(
        functools.partial(_gqa_kernel, True, layer),
        grid_spec=grid_spec,
        out_shape=jax.ShapeDtypeStruct((T_LAT, GQA_HEADS * GQA_HD), BF16),
        compiler_params=_cparams(2),
        name="gqa_attn_lat",
    )(sink, qg, kg, kg, kg, kg_c, vg, vg, vg, vg_c)


def _pool_tile(i, hp_p, hp_c, hp_n, pw_ref, ps_ref):
    is_ctx = i < CTX_TILES
    t4 = (i - CTX_TILES) % LAT_TILES_PER_SEQ
    seq_len = jnp.where(is_ctx, S_CTX, S_LAT)
    base = jnp.where(is_ctx, 0, t4 * TM)

    n_keys = TM + 2 * HALO
    first_key = jnp.where(jnp.logical_or(is_ctx, t4 == 0), HALO, 0)
    end_key = jnp.where(jnp.logical_or(is_ctx, t4 == LAT_TILES_PER_SEQ - 1), HALO + TM, n_keys)
    qi = lax.broadcasted_iota(I32, (TM, n_keys), 0)
    kj = lax.broadcasted_iota(I32, (TM, n_keys), 1)
    rel = kj - HALO - qi
    key_ok = (kj >= first_key) & (kj < end_key)

    cur = hp_c[...]
    keys = jnp.concatenate([hp_p[TM - HALO:TM, :], cur, hp_n[0:HALO, :]], axis=0)
    k_hi, k_lo = _split(keys)
    qpos = base + lax.broadcasted_iota(I32, (TM, 1), 0)
    outs = []
    for g, w in enumerate(POOL_WINDOWS):
        a = jnp.where((rel >= -(w // 2)) & (rel <= w // 2 - 1) & key_ok, 1.0, 0.0).astype(BF16)
        cols = slice(g * POOL_GROUP, (g + 1) * POOL_GROUP)
        s = _dot(a, k_hi[:, cols]) + _dot(a, k_lo[:, cols])
        lo = jnp.maximum(qpos - w // 2, 0)
        hi = jnp.minimum(qpos + w // 2 - 1, seq_len - 1)
        cnt = (hi - lo + 1).astype(F32)
        d = s / cnt - cur[:, cols]
        y = _dot(d.astype(BF16), pw_ref[g]) * ps_ref[:, cols]
        outs.append(y.astype(BF16))
    return jnp.concatenate(outs, axis=1)


def _merge_kernel(xc_ref, xl_ref, mod_ref, hp_p, hp_c, hp_n, pw_ref, ps_ref, ymc_ref, yml_ref, ygc_ref, ygl_ref, sg_ref,
                  wbp_ref, wbm_ref, wbg_ref, wo_ref, lg_ref, lb_ref, rw_ref, rb_ref,
                  x1_ref, u2_ref, ri_ref, rwt_ref, cnt_ref, carry):
    i = pl.program_id(0)

    @pl.when(i == 0)
    def _():
        carry[...] = jnp.zeros_like(carry)

    is_ctx = i < CTX_TILES
    x = jnp.where(is_ctx, xc_ref[...], xl_ref[...])
    yp = _pool_tile(i, hp_p, hp_c, hp_n, pw_ref, ps_ref)
    ym = jnp.where(is_ctx, ymc_ref[...], yml_ref[...])
    yg = jnp.where(is_ctx, ygc_ref[...], ygl_ref[...])
    m = (sg_ref[:, 0:D_MODEL].astype(F32) * _dot(yp, wbp_ref[...])
         + sg_ref[:, D_MODEL:2 * D_MODEL].astype(F32) * _dot(ym, wbm_ref[...])
         + sg_ref[:, 2 * D_MODEL:3 * D_MODEL].astype(F32) * _dot(yg, wbg_ref[...]))
    y = _dot(m.astype(BF16), wo_ref[...])
    gate1 = mod_ref[0, :, 2 * D_MODEL:3 * D_MODEL]
    x1 = _layer_norm(ALPHA_DN * x + gate1 * y, lg_ref[...], lb_ref[...])
    x1_ref[...] = x1
    sh2 = mod_ref[0, :, 3 * D_MODEL:4 * D_MODEL]
    sc2 = mod_ref[0, :, 4 * D_MODEL:5 * D_MODEL]
    u2 = x1 * (1.0 + sc2) + sh2
    u2_ref[...] = u2

    u_hi, u_lo = _split(u2)
    rw_hi, rw_lo = _split(rw_ref[...])
    both = _dot(u_hi, jnp.concatenate([rw_hi, rw_lo], axis=1))
    logits = both[:, 0:LANES] + both[:, LANES:2 * LANES] + _dot(u_lo, rw_hi) + rb_ref[...]
    lane = lax.broadcasted_iota(I32, (TM, LANES), 1)
    lane_f = lane.astype(F32)
    vals, idxs = [], []
    rest = logits
    for _ in range(TOP_K):
        mx = jnp.max(rest, axis=-1, keepdims=True)
        ix = jnp.min(jnp.where(rest == mx, lane_f, float(LANES)), axis=-1, keepdims=True).astype(I32)
        vals.append(mx)
        idxs.append(ix)
        rest = jnp.where(lane == ix, -jnp.inf, rest)
    es = [jnp.exp(v - vals[0]) for v in vals]
    den = functools.reduce(jnp.add, es)

    sel = functools.reduce(jnp.add, [jnp.where(lane == ix, 1.0, 0.0) for ix in idxs])
    r_i = lax.broadcasted_iota(I32, (TM, TM), 0)
    c_i = lax.broadcasted_iota(I32, (TM, TM), 1)
    below = jnp.where(c_i < r_i, 1.0, 0.0).astype(BF16)
    rank = _dot(below, sel.astype(BF16)) + carry[...]
    carry[...] = carry[...] + jnp.sum(sel, axis=0, keepdims=True)
    cnt_ref[...] = carry[...]

    ri = jnp.zeros((TM, LANES), I32)
    rwt = jnp.zeros((TM, LANES), F32)
    for k in range(TOP_K):
        rk = jnp.sum(jnp.where(lane == idxs[k], rank, 0.0), axis=-1, keepdims=True).astype(I32)
        ri = jnp.where(lane == k, idxs[k], ri)
        ri = jnp.where(lane == TOP_K + k, rk, ri)
        rwt = jnp.where(lane == k, es[k] / den, rwt)
    ri_ref[...] = ri
    rwt_ref[...] = rwt


def _tok_spec(width):
    return pl.BlockSpec((TM, width), lambda i, *_: (i, 0))


def _ctx_spec(width):
    return pl.BlockSpec((TM, width), lambda i, *_: (jnp.minimum(i, CTX_TILES - 1), 0))


def _lat_spec(width):
    return pl.BlockSpec((TM, width), lambda i, *_: (jnp.maximum(i - CTX_TILES, 0), 0))


def _mod_spec(layer):
    return pl.BlockSpec((None, 1, 1, 6 * D_MODEL), lambda i, *_: (layer, _mod_row(i), 0, 0))


def _merge(layer, xc, xl, mod, hp, pw, ps, ymc, yml, ygc, ygl, sg, wbp, wbm, wbg, wo, lg, lb, rw, rb):
    tok, ctx, lat = _tok_spec, _ctx_spec, _lat_spec
    per_layer = functools.partial(_layer_spec, layer)
    return pl.pallas_call(
        _merge_kernel,
        grid=(N_TILES,),
        in_specs=[ctx(D_MODEL), lat(D_MODEL),
                  _mod_spec(layer),
                  pl.BlockSpec((TM, POOL_WIDTH), lambda i: (jnp.maximum(i - 1, 0), 0)),
                  tok(POOL_WIDTH),
                  pl.BlockSpec((TM, POOL_WIDTH), lambda i: (jnp.minimum(i + 1, N_TILES - 1), 0)),
                  per_layer((len(POOL_WINDOWS), POOL_GROUP, POOL_GROUP)),
                  per_layer((1, POOL_WIDTH)),
                  ctx(MLA_HEADS * MLA_V), lat(MLA_HEADS * MLA_V), ctx(GQA_HEADS * GQA_HD), lat(GQA_HEADS * GQA_HD),
                  tok(3 * D_MODEL),
                  per_layer((POOL_WIDTH, D_MODEL)), per_layer((MLA_HEADS * MLA_V, D_MODEL)),
                  per_layer((GQA_HEADS * GQA_HD, D_MODEL)),
                  per_layer((D_MODEL, D_MODEL)), per_layer((1, D_MODEL)), per_layer((1, D_MODEL)),
                  per_layer((D_MODEL, LANES)), per_layer((1, LANES))],
        out_specs=[tok(D_MODEL), tok(D_MODEL), tok(LANES), tok(LANES), _const_spec((1, LANES))],
        out_shape=[jax.ShapeDtypeStruct((T_ALL, D_MODEL), F32), jax.ShapeDtypeStruct((T_ALL, D_MODEL), F32),
                   jax.ShapeDtypeStruct((T_ALL, LANES), I32), jax.ShapeDtypeStruct((T_ALL, LANES), F32),
                   jax.ShapeDtypeStruct((1, LANES), F32)],
        scratch_shapes=[pltpu.VMEM((1, LANES), F32)],
        compiler_params=_cparams(1),
        name="merge_route",
    )(xc, xl, mod, hp, hp, hp, pw, ps, ymc, yml, ygc, ygl, sg, wbp, wbm, wbg, wo, lg, lb, rw, rb)


def _row_copy(src_ref, src_row, dst_ref, dst_row, sem):
    return pltpu.make_async_copy(src_ref.at[pl.ds(src_row, 1), :], dst_ref.at[pl.ds(dst_row, 1), :], sem)


ROWS_PER_ISSUE = 4
WAITS_PER_TRIP = 64


def _drain_rows(wait_one, n_rows):
    def trip(_, c):
        for _ in range(WAITS_PER_TRIP):
            wait_one()
        return c

    lax.fori_loop(0, n_rows // WAITS_PER_TRIP, trip, 0)


def _dispatch_kernel(dest_ref, zrow_ref, nu_ref, u_ref, xs_ref, zbuf, sem, zsem):
    i = pl.program_id(0)
    base = i * (TM * TOP_K)

    @pl.when(i == 0)
    def _():
        zbuf[...] = jnp.zeros_like(zbuf)

        def zero_tile(row):
            if not isinstance(row, int):
                row = pl.multiple_of(row, TME)
            return pltpu.make_async_copy(zbuf, xs_ref.at[pl.ds(row, TME), :], zsem)

        for e in range(N_EXPERTS):
            @pl.when(zrow_ref[e] >= 0)
            def _():
                zero_tile(zrow_ref[e]).start()

        def start_tail(j, c):
            zero_tile(j * TME).start()
            return c

        lax.fori_loop(nu_ref[0], NT_E, start_tail, 0)

        for e in range(N_EXPERTS):
            @pl.when(zrow_ref[e] >= 0)
            def _():
                zero_tile(0).wait()

        def wait_tail(j, c):
            zero_tile(0).wait()
            return c

        lax.fori_loop(nu_ref[0], NT_E, wait_tail, 0)

    def issue(g, c):
        for r in range(ROWS_PER_ISSUE):
            t = g * ROWS_PER_ISSUE + r
            for k in range(TOP_K):
                _row_copy(u_ref, t, xs_ref, dest_ref[base + t * TOP_K + k], sem).start(priority=k % 2)
        return c

    lax.fori_loop(0, TM // ROWS_PER_ISSUE, issue, 0)
    _drain_rows(lambda: _row_copy(u_ref, 0, xs_ref, 0, sem).wait(), TM * TOP_K)


def _dispatch(plan, u2):
    grid_spec = pltpu.PrefetchScalarGridSpec(
        num_scalar_prefetch=3,
        grid=(N_TILES,),
        in_specs=[pl.BlockSpec((TM, D_MODEL), lambda i, *_: (i, 0))],
        out_specs=pl.BlockSpec(memory_space=pl.ANY),
        scratch_shapes=[pltpu.VMEM((TME, D_MODEL), F32), pltpu.SemaphoreType.DMA(()), pltpu.SemaphoreType.DMA(())],
    )
    return pl.pallas_call(
        _dispatch_kernel,
        grid_spec=grid_spec,
        out_shape=jax.ShapeDtypeStruct((P_ROWS, D_MODEL), F32),
        compiler_params=_cparams(1),
        name="moe_dispatch",
    )(plan["dest"], plan["zero_row"], plan["n_used"], u2)


def _expert_kernel(layer, te_ref, nu_ref, slot_ref, first_ref, next_ref, half_ref,
                   xs_ref, wgu_hbm, bgu_ref, wd_hbm, bd_ref, ys_ref, wgu_f, wd_f, wgu_s, wd_s, sems):
    i = pl.program_id(0)

    def weight_copies(e, s):
        return (pltpu.make_async_copy(wgu_hbm.at[layer, e], wgu_f.at[s], sems.at[0, s]),
                pltpu.make_async_copy(wd_hbm.at[layer, e], wd_f.at[s], sems.at[1, s]))

    @pl.when(first_ref[i] == 1)
    def _():
        s = slot_ref[i]

        @pl.when(i == 0)
        def _():
            for cp in weight_copies(te_ref[i], s):
                cp.start()

        for cp in weight_copies(te_ref[i], s):
            cp.wait()

        @pl.when(next_ref[i] >= 0)
        def _():
            for cp in weight_copies(next_ref[i], 1 - s):
                cp.start()

        wgu_s[...] = wgu_f[s].astype(BF16)
        wd_s[...] = wd_f[s].astype(BF16)

    def mlp(rows):
        h = _dot(xs_ref[0:rows, :].astype(BF16), wgu_s[...]) + bgu_ref[0, 0]
        glu = jnp.minimum(h[:, 0:D_FF], SWIGLU_LIMIT)
        lin = jnp.clip(h[:, D_FF:2 * D_FF], -SWIGLU_LIMIT, SWIGLU_LIMIT)
        a = glu * _sigmoid(SWIGLU_ALPHA * glu) * (lin + 1.0)
        ys_ref[0:rows, :] = _dot(a.astype(BF16), wd_s[...]) + bd_ref[0, 0]

    in_use = i < nu_ref[0]

    @pl.when(jnp.logical_and(in_use, half_ref[i] == 0))
    def _():
        mlp(TME)

    @pl.when(jnp.logical_and(in_use, half_ref[i] == 1))
    def _():
        mlp(TME // 2)
        ys_ref[TME // 2:TME, :] = jnp.zeros((TME // 2, D_MODEL), F32)

    @pl.when(i >= nu_ref[0])
    def _():
        ys_ref[...] = jnp.zeros_like(ys_ref)


def _experts(layer, plan, xs, w_gate_up, b_gate_up, w_down, b_down):
    row = lambda i, te, nu, *_: (jnp.minimum(i, nu[0] - 1), 0)
    bias = lambda i, te, *_: (layer, te[i], 0, 0)
    grid_spec = pltpu.PrefetchScalarGridSpec(
        num_scalar_prefetch=6,
        grid=(NT_E,),
        in_specs=[pl.BlockSpec((TME, D_MODEL), row),
                  pl.BlockSpec(memory_space=pl.ANY),
                  pl.BlockSpec((1, 1, 1, 2 * D_FF), bias),
                  pl.BlockSpec(memory_space=pl.ANY),
                  pl.BlockSpec((1, 1, 1, D_MODEL), bias)],
        out_specs=pl.BlockSpec((TME, D_MODEL), lambda i, *_: (i, 0)),
        scratch_shapes=[pltpu.VMEM((2, D_MODEL, 2 * D_FF), F32), pltpu.VMEM((2, D_FF, D_MODEL), F32),
                        pltpu.VMEM((D_MODEL, 2 * D_FF), BF16), pltpu.VMEM((D_FF, D_MODEL), BF16),
                        pltpu.SemaphoreType.DMA((2, 2))],
    )
    return pl.pallas_call(
        functools.partial(_expert_kernel, layer),
        grid_spec=grid_spec,
        out_shape=jax.ShapeDtypeStruct((P_ROWS, D_MODEL), F32),
        compiler_params=_cparams(1),
        name="moe_experts",
    )(plan["tile_expert"], plan["n_used"], plan["tile_slot"], plan["tile_first"], plan["tile_next"], plan["tile_half"],
      xs, w_gate_up,
      b_gate_up.reshape(DEPTH, N_EXPERTS, 1, 2 * D_FF), w_down, b_down.reshape(DEPTH, N_EXPERTS, 1, D_MODEL))


def _combine_kernel(dest_ref, ys_ref, rwt_ref, x1_ref, mod_ref, lg_ref, lb_ref, oc_ref, ol_ref, buf, sems):
    i = pl.program_id(0)
    slot = jnp.bitwise_and(i, 1)

    def gather(tile, s):
        base = tile * (TM * TOP_K)

        def issue(g, c):
            for r in range(ROWS_PER_ISSUE):
                t = g * ROWS_PER_ISSUE + r
                for k in range(TOP_K):
                    _row_copy(ys_ref, dest_ref[base + t * TOP_K + k], buf.at[s, k], t, sems.at[s]).start(priority=k % 2)
            return c

        lax.fori_loop(0, TM // ROWS_PER_ISSUE, issue, 0)

    @pl.when(i == 0)
    def _():
        gather(0, 0)

    @pl.when(i + 1 < pl.num_programs(0))
    def _():
        gather(i + 1, 1 - slot)

    _drain_rows(lambda: _row_copy(ys_ref, 0, buf.at[slot, 0], 0, sems.at[slot]).wait(), TM * TOP_K)

    moe = functools.reduce(jnp.add, [rwt_ref[:, k:k + 1] * buf[slot, k] for k in range(TOP_K)])
    gate2 = mod_ref[0, :, 5 * D_MODEL:6 * D_MODEL]
    out = _layer_norm(ALPHA_DN * x1_ref[...] + gate2 * moe, lg_ref[...], lb_ref[...])

    @pl.when(i < CTX_TILES)
    def _():
        oc_ref[...] = out

    @pl.when(i >= CTX_TILES)
    def _():
        ol_ref[...] = out


def _combine(layer, dest, ys, rwt, x1, mod, lg, lb):
    grid_spec = pltpu.PrefetchScalarGridSpec(
        num_scalar_prefetch=1,
        grid=(N_TILES,),
        in_specs=[pl.BlockSpec(memory_space=pl.ANY),
                  pl.BlockSpec((TM, LANES), lambda i, d: (i, 0)),
                  pl.BlockSpec((TM, D_MODEL), lambda i, d: (i, 0)),
                  _mod_spec(layer),
                  _layer_spec(layer, (1, D_MODEL)),
                  _layer_spec(layer, (1, D_MODEL))],
        out_specs=[_ctx_spec(D_MODEL), _lat_spec(D_MODEL)],
        scratch_shapes=[pltpu.VMEM((2, TOP_K, TM, D_MODEL), F32), pltpu.SemaphoreType.DMA((2,))],
    )
    return pl.pallas_call(
        _combine_kernel,
        grid_spec=grid_spec,
        out_shape=[jax.ShapeDtypeStruct((T_CTX, D_MODEL), F32), jax.ShapeDtypeStruct((T_LAT, D_MODEL), F32)],
        compiler_params=_cparams(1),
        name="moe_combine",
    )(dest, ys, rwt, x1, mod, lg, lb)


def _pad_heads(w, n_heads, width):
    lead = w.shape[:-1]
    w = w.reshape(lead + (n_heads, width))
    pad = ((0, 0),) * (len(lead) + 1) + ((0, LANES - width),)
    return jnp.pad(w, pad).reshape(lead + (n_heads * LANES,))


def _dup_heads(w):
    lead = w.shape[:-1]
    w = w.reshape(lead + (GQA_KV, 1, GQA_HD))
    return jnp.concatenate([w, w], axis=-2).reshape(lead + (GQA_KV * LANES,))


def _pack_w_in(w):
    cuts = [0, 512, 896, 1152, 1184, 1696, 1824, 1952, 5024]
    hp, cq, ckv, kr, q, k, v, g = [w[..., a:b] for a, b in zip(cuts[:-1], cuts[1:])]
    packed = jnp.concatenate([hp, cq, ckv, _pad_heads(kr, 1, MLA_ROPE), q, _dup_heads(k), _dup_heads(v), g], axis=-1)
    return packed.astype(BF16)


def _pack_mla(wuq, wukv):
    wuq_p = _pad_heads(wuq, MLA_HEADS, MLA_QK).astype(BF16)
    kv = wukv.reshape(DEPTH, MLA_KV_LORA, MLA_HEADS, MLA_NOPE + MLA_V)
    wk_top = _pad_heads(kv[..., :MLA_NOPE].reshape(DEPTH, MLA_KV_LORA, MLA_HEADS * MLA_NOPE), MLA_HEADS, MLA_NOPE)
    place = jnp.zeros((LANES, MLA_HEADS, LANES), F32)
    r = jnp.arange(MLA_ROPE)
    place = place.at[r, :, MLA_NOPE + r].set(1.0).reshape(1, LANES, MLA_HEADS * LANES)
    wk = jnp.concatenate([wk_top, jnp.broadcast_to(place, (DEPTH,) + place.shape[1:])], axis=1).astype(BF16)
    wv = kv[..., MLA_NOPE:].reshape(DEPTH, MLA_KV_LORA, MLA_HEADS * MLA_V).astype(BF16)
    return wuq_p, wk, wv


def _rope_table(dim, lane0):
    quarter = dim // 4
    pos = jnp.arange(S_LAT)
    rows = (pos // GRID_W).astype(F32)
    cols = (pos % GRID_W).astype(F32)
    freqs = jnp.power(ROPE_THETA, -jnp.arange(quarter, dtype=F32) / quarter)
    ang_r = rows[:, None] * freqs[None, :]
    ang_c = cols[:, None] * freqs[None, :]
    zero = jnp.zeros((S_LAT, quarter), F32)
    cos = jnp.concatenate([jnp.cos(ang_r), jnp.cos(ang_r), jnp.cos(ang_c), jnp.cos(ang_c)], axis=1)
    s_up = jnp.concatenate([-jnp.sin(ang_r), zero, -jnp.sin(ang_c), zero], axis=1)
    s_dn = jnp.concatenate([zero, jnp.sin(ang_r), zero, jnp.sin(ang_c)], axis=1)

    def place(t, fill):
        if lane0 is None:
            return jnp.tile(t, (1, LANES // dim))
        full = jnp.full((S_LAT, LANES), fill, F32)
        return full.at[:, lane0:lane0 + dim].set(t)

    planes = jnp.stack([place(cos, 1.0), place(s_up, 0.0), place(s_dn, 0.0)])
    ident = jnp.stack([jnp.ones((TM, LANES), F32), jnp.zeros((TM, LANES), F32), jnp.zeros((TM, LANES), F32)])
    return jnp.concatenate([planes, ident], axis=1)


TILE_LANES = 2 * LANES
ROW_EXPERT, ROW_SLOT, ROW_FIRST, ROW_NEXT, ROW_ZERO, ROW_USED, ROW_HALF = range(7)


def _plan_kernel(ri_ref, cnt_ref, dest_ref, tiles_ref):
    lane = lax.broadcasted_iota(I32, (1, LANES), 1)
    r_i = lax.broadcasted_iota(I32, (LANES, LANES), 0)
    c_i = lax.broadcasted_iota(I32, (LANES, LANES), 1)
    upto = jnp.where(r_i <= c_i, 1.0, 0.0).astype(BF16)

    def prefix(row):
        return _dot(jnp.broadcast_to(row, (SUBLANES, LANES)).astype(BF16), upto)[0:1, :]

    cnt = cnt_ref[...]
    t_cnt = jnp.floor((cnt + (TME - 1)) * (1.0 / TME))
    t_end = prefix(t_cnt)
    t_start = t_end - t_cnt
    n_used = jnp.max(t_end, axis=-1, keepdims=True)

    tok_lane = lax.broadcasted_iota(I32, ri_ref.shape, 1)
    ri = ri_ref[...]
    dest = jnp.zeros(ri_ref.shape, I32)
    for k in range(TOP_K):
        start_k = jnp.sum(jnp.where(tok_lane == ri[:, k:k + 1], t_start, 0.0), axis=-1, keepdims=True)
        dest_k = start_k.astype(I32) * TME + ri[:, TOP_K + k:TOP_K + k + 1]
        dest = jnp.where(tok_lane == k, dest_k, dest)
    dest_ref[...] = dest

    @pl.when(pl.program_id(0) == 0)
    def _():
        eye = jnp.where(r_i == c_i, 1.0, 0.0).astype(BF16)

        def to_col(row):
            return _dot_t(eye, jnp.broadcast_to(row, (SUBLANES, LANES)).astype(BF16))[:, 0:1]

        e_col = lax.broadcasted_iota(I32, (LANES, 1), 0)
        real_e = e_col < N_EXPERTS
        tile = lax.broadcasted_iota(I32, (LANES, TILE_LANES), 1).astype(F32)
        end_col = to_col(t_end)
        expert = jnp.sum(jnp.where(real_e & (end_col <= tile), 1.0, 0.0), axis=0, keepdims=True)
        last = jnp.sum(jnp.where(real_e & (end_col <= n_used - 1.0), 1.0, 0.0), axis=0, keepdims=True)
        expert = jnp.minimum(expert, last)
        mine = e_col.astype(F32) == expert

        def per_tile(col):
            return jnp.sum(jnp.where(mine, col, 0.0), axis=0, keepdims=True)

        has = t_cnt > 0.0
        ordinal = prefix(jnp.where(has, 1.0, 0.0)) - 1.0
        ord_col = to_col(ordinal)
        slot = per_tile(ord_col - 2.0 * jnp.floor(ord_col * 0.5))
        tile_row = tile[0:1, :]
        first = jnp.where((tile_row < n_used) & (tile_row == per_tile(to_col(t_start))), 1.0, 0.0)
        later = jnp.where(has & (lane > e_col), lane.astype(F32), float(LANES))
        nxt_col = jnp.min(later, axis=-1, keepdims=True)
        nxt_col = jnp.where(nxt_col == float(LANES), -1.0, nxt_col)
        nxt = per_tile(nxt_col)
        zero_row = jnp.where(has, (t_end - 1.0) * TME, -1.0)
        zero_row = jnp.concatenate([zero_row, jnp.full((1, TILE_LANES - LANES), -1.0, F32)], axis=1)
        used = jnp.broadcast_to(n_used, (1, TILE_LANES))
        short_tail = jnp.where(has & (cnt - (t_cnt - 1.0) * TME <= TME // 2), 1.0, 0.0)
        is_last = (tile_row < n_used) & (tile_row == per_tile(end_col) - 1.0)
        half = jnp.where(is_last, per_tile(to_col(short_tail)), 0.0)
        pad = jnp.zeros((SUBLANES - 7, TILE_LANES), F32)
        tiles_ref[...] = jnp.concatenate([expert, slot, first, nxt, zero_row, used, half, pad], axis=0).astype(I32)


def _routing_plan(ri, counts):
    rows = 4 * TM
    dest, tiles = pl.pallas_call(
        _plan_kernel,
        grid=(T_ALL // rows,),
        in_specs=[pl.BlockSpec((rows, LANES), lambda i: (i, 0)), _const_spec((1, LANES))],
        out_specs=[pl.BlockSpec((rows, LANES), lambda i: (i, 0)), _const_spec((SUBLANES, TILE_LANES))],
        out_shape=[jax.ShapeDtypeStruct((T_ALL, LANES), I32), jax.ShapeDtypeStruct((SUBLANES, TILE_LANES), I32)],
        compiler_params=_cparams(1),
        name="moe_plan",
    )(ri, counts)
    return {"dest": dest[:, 0:TOP_K].reshape(-1), "tile_expert": tiles[ROW_EXPERT, :NT_E],
            "n_used": tiles[ROW_USED, 0:1], "tile_slot": tiles[ROW_SLOT, :NT_E], "tile_first": tiles[ROW_FIRST, :NT_E],
            "tile_next": tiles[ROW_NEXT, :NT_E], "tile_half": tiles[ROW_HALF, :NT_E],
            "zero_row": tiles[ROW_ZERO, :N_EXPERTS]}


def kernel(x_prompt, x_sample, c, cache_mla_ckv, cache_mla_krope, cache_gqa_k, cache_gqa_v, c_ctx, w_ada, b_ada, w_in, mla_q_norm, mla_kv_norm, w_mla_uq, w_mla_ukv, gqa_sink, pool_w, pool_scale, w_branch_pool, w_branch_mla, w_branch_gqa, w_out, ln1_g, ln1_b, ln2_g, ln2_b, router_w, router_b, w_gate_up, b_gate_up, w_down, b_down):
    xc = x_prompt.reshape(T_CTX, D_MODEL)
    xl = x_sample.reshape(T_LAT, D_MODEL)
    cond8 = jnp.concatenate([c_ctx[None, :], c, jnp.zeros((SUBLANES - 1 - B_LAT, D_MODEL), F32)], axis=0)
    mod_all = _ada_params(cond8, w_ada, b_ada)

    tab_g = _rope_table(GQA_HD, None)
    tab_q = _rope_table(MLA_ROPE, MLA_NOPE)
    tab_k = _rope_table(MLA_ROPE, 0)

    mod_all = mod_all.reshape(DEPTH, SUBLANES, 1, 6 * D_MODEL)
    wp = _pack_w_in(w_in)
    wuq_p, wk, wv = _pack_mla(w_mla_uq, w_mla_ukv)
    row = lambda a: a.reshape(DEPTH, 1, a.shape[-1])
    q_norm, kv_norm, pool_s = row(mla_q_norm), row(mla_kv_norm), row(pool_scale)
    ln1g, ln1b, ln2g, ln2b = row(ln1_g), row(ln1_b), row(ln2_g), row(ln2_b)
    pool_wb = pool_w.astype(BF16)
    wbp, wbm, wbg, wo = (w.astype(BF16) for w in (w_branch_pool, w_branch_mla, w_branch_gqa, w_out))
    rw = jnp.pad(router_w, ((0, 0), (0, 0), (0, LANES - N_EXPERTS)))
    rb = jnp.pad(router_b, ((0, 0), (0, LANES - N_EXPERTS)), constant_values=-jnp.inf).reshape(DEPTH, 1, LANES)
    sinks = gqa_sink.reshape(DEPTH * GQA_HEADS)
    n_cache = B_LAT * DEPTH * PAST_LEN
    ckv_c = cache_mla_ckv.reshape(n_cache, MLA_KV_LORA)
    kr_c = jnp.pad(cache_mla_krope.reshape(n_cache, MLA_ROPE), ((0, 0), (0, LANES - MLA_ROPE)))
    gk_c = _dup_heads(cache_gqa_k.reshape(n_cache, GQA_KV * GQA_HD)).astype(BF16)
    gv_c = _dup_heads(cache_gqa_v.reshape(n_cache, GQA_KV * GQA_HD)).astype(BF16)

    outs = {"ckv": [], "kr": [], "gk": [], "gv": []}
    for l in range(DEPTH):
        (hp, qm, ckv, kr, km, vm, qg, kg, vg, nk, nv, sg) = _in_proj(
            l, xc, xl, mod_all, wp, q_norm, kv_norm, wuq_p, wk, wv, tab_g, tab_q, tab_k)
        outs["ckv"].append(ckv[:T_CTX].reshape(B_CTX, S_CTX, MLA_KV_LORA))
        outs["kr"].append(kr[:T_CTX, :MLA_ROPE].reshape(B_CTX, S_CTX, MLA_ROPE))
        outs["gk"].append(nk[:T_CTX].reshape(B_CTX, S_CTX, GQA_KV, GQA_HD))
        outs["gv"].append(nv[:T_CTX].reshape(B_CTX, S_CTX, GQA_KV, GQA_HD))

        km_c, vm_c = _expand_cache(l, ckv_c, kr_c, wk, wv)
        ymc = _mla_ctx(qm, km, vm)
        yml = _mla_lat(qm, km, vm, km_c, vm_c)
        ygc = _gqa_ctx(l, sinks, qg, kg, vg)
        ygl = _gqa_lat(l, sinks, qg, kg, vg, gk_c, gv_c)

        x1, u2, ri, rwt, counts = _merge(
            l, xc, xl, mod_all, hp, pool_wb, pool_s, ymc, yml, ygc, ygl, sg, wbp, wbm, wbg, wo, ln1g, ln1b, rw, rb)

        plan = _routing_plan(ri, counts)
        xs = _dispatch(plan, u2)
        ys = _experts(l, plan, xs, w_gate_up, b_gate_up, w_down, b_down)
        xc, xl = _combine(l, plan["dest"], ys, rwt, x1, mod_all, ln2g, ln2b)

    y_prompt = xc.reshape(B_CTX, S_CTX, D_MODEL)
    y_sample = xl.reshape(B_LAT, S_LAT, D_MODEL)
    return (y_prompt, y_sample, jnp.stack(outs["ckv"], axis=1), jnp.stack(outs["kr"], axis=1),
            jnp.stack(outs["gk"], axis=1), jnp.stack(outs["gv"], axis=1))
```

```python
import functools
import math

import jax
import jax.numpy as jnp
from jax import lax
from jax.experimental import pallas as pl
from jax.experimental.pallas import tpu as pltpu

F32 = jnp.float32
BF16 = jnp.bfloat16
I32 = jnp.int32

D_MODEL = 1024
DEPTH = 2
B_CTX, S_CTX = 16, 256
B_LAT, S_LAT = 4, 1024
PAST_LEN = 512
GRID_W = 64
ROPE_THETA = 10000.0
ALPHA_DN = (2 * DEPTH) ** 0.25
LN_EPS = 1e-5
RMS_EPS = 1e-6
NEG_INF = -1e30
POOL_WINDOWS = (2, 4, 8, 16)
POOL_GROUP = 128
POOL_WIDTH = 512
MLA_HEADS = 8
MLA_NOPE = 64
MLA_ROPE = 32
MLA_V = 64
MLA_QK = MLA_NOPE + MLA_ROPE
MLA_Q_LORA = 384
MLA_KV_LORA = 256
MLA_SCALE = 1.0 / math.sqrt(MLA_QK)
GQA_HEADS = 8
GQA_KV = 2
GQA_HD = 64
GQA_GROUP = GQA_HEADS // GQA_KV
GQA_SCALE = 1.0 / math.sqrt(GQA_HD)
WINDOW = 128
WBLK = 128
N_EXPERTS = 32
TOP_K = 4
D_FF = 1024
SWIGLU_LIMIT = 7.0
SWIGLU_ALPHA = 1.702

LANES = 128
SUBLANES = 8
VMEM_LIMIT = 56 * 1024 * 1024

T_CTX = B_CTX * S_CTX
T_LAT = B_LAT * S_LAT
T_ALL = T_CTX + T_LAT
TM = 256
N_TILES = T_ALL // TM
CTX_TILES = T_CTX // TM
LAT_TILES_PER_SEQ = S_LAT // TM
HALO = 16
GQA_LAT_ROWS = 256
MLA_LAT_TQ = 512

C_HP = 0
C_CQ = C_HP + POOL_WIDTH
C_CKV = C_CQ + MLA_Q_LORA
C_KR = C_CKV + MLA_KV_LORA
C_GQ = C_KR + LANES
C_GK = C_GQ + GQA_HEADS * GQA_HD
C_GV = C_GK + GQA_KV * LANES
C_G = C_GV + GQA_KV * LANES
C_END = C_G + 3 * D_MODEL

TME = 256
N_PAIRS = T_ALL * TOP_K
NT_E = N_PAIRS // TME + N_EXPERTS
P_ROWS = NT_E * TME


def _cparams(n_axes):
    return pltpu.CompilerParams(dimension_semantics=("arbitrary",) * n_axes, vmem_limit_bytes=VMEM_LIMIT)


def _dot(a, b):
    return jnp.dot(a, b, preferred_element_type=F32)


def _dot_t(a, b):
    return lax.dot_general(a, b, (((1,), (1,)), ((), ())), preferred_element_type=F32)


def _split(x):
    hi = x.astype(BF16)
    lo = (x - hi.astype(F32)).astype(BF16)
    return hi, lo


def _dot3(a, b):
    a_hi, a_lo = _split(a)
    b_hi, b_lo = _split(b)
    return _dot(a_hi, b_hi) + _dot(a_hi, b_lo) + _dot(a_lo, b_hi)


def _sigmoid(x):
    return 1.0 / (1.0 + jnp.exp(-x))


def _layer_norm(z, g, b):
    mu = jnp.mean(z, axis=-1, keepdims=True)
    zc = z - mu
    var = jnp.mean(zc * zc, axis=-1, keepdims=True)
    return zc * lax.rsqrt(var + LN_EPS) * g + b


def _mod_row(i):
    return jnp.where(i < CTX_TILES, 0, 1 + (i - CTX_TILES) // LAT_TILES_PER_SEQ)


def _pos_block(i):
    return jnp.where(i < CTX_TILES, LAT_TILES_PER_SEQ, (i - CTX_TILES) % LAT_TILES_PER_SEQ)


def _const_spec(shape):
    nd = len(shape)
    return pl.BlockSpec(shape, lambda *_: (0,) * nd)


def _layer_spec(layer, shape):
    nd = len(shape)
    return pl.BlockSpec((None,) + tuple(shape), lambda *_: (layer,) + (0,) * nd)


def _ada_kernel(c_ref, w_ref, b_ref, o_ref):
    c = c_ref[...]
    s = c * _sigmoid(c)
    o_ref[0] = _dot3(s, w_ref[0]) + b_ref[0]


def _ada_params(cond8, w_ada, b_ada):
    n_col = 6 * D_MODEL
    blk = 1024
    return pl.pallas_call(
        _ada_kernel,
        grid=(DEPTH, n_col // blk),
        in_specs=[
            pl.BlockSpec((SUBLANES, D_MODEL), lambda l, j: (0, 0)),
            pl.BlockSpec((1, D_MODEL, blk), lambda l, j: (l, 0, j)),
            pl.BlockSpec((1, 1, blk), lambda l, j: (l, 0, j)),
        ],
        out_specs=pl.BlockSpec((1, SUBLANES, blk), lambda l, j: (l, 0, j)),
        out_shape=jax.ShapeDtypeStruct((DEPTH, SUBLANES, n_col), F32),
        compiler_params=_cparams(2),
        name="ada_params",
    )(cond8, w_ada, b_ada.reshape(DEPTH, 1, n_col))


def _rope(x, tab_ref, shift):
    return (x * tab_ref[0]
            + pltpu.roll(x, LANES - shift, 1) * tab_ref[1]
            + pltpu.roll(x, shift, 1) * tab_ref[2])


def _rms(h, g):
    return h * lax.rsqrt(jnp.mean(h * h, axis=-1, keepdims=True) + RMS_EPS) * g


def _in_kernel(xc_ref, xl_ref, mod_ref, wp_ref, qn_ref, kvn_ref, wuq_ref, wk_ref, wv_ref, tg_ref, tq_ref, tk_ref,
               hp_ref, qm_ref, ckv_ref, kr_ref, km_ref, vm_ref, qg_ref, kg_ref, vg_ref, nk_ref, nv_ref, sg_ref):
    x = jnp.where(pl.program_id(0) < CTX_TILES, xc_ref[...], xl_ref[...])
    sh = mod_ref[0, :, 0:D_MODEL]
    sc = mod_ref[0, :, D_MODEL:2 * D_MODEL]
    u = (x * (1.0 + sc) + sh).astype(BF16)

    def proj(lo, hi):
        return _dot(u, wp_ref[:, lo:hi])

    hp_ref[...] = proj(C_HP, C_CQ)

    qn = _rms(proj(C_CQ, C_CKV), qn_ref[...]).astype(BF16)
    q = _dot(qn, wuq_ref[...]) * MLA_SCALE
    for h in range(MLA_HEADS):
        qm_ref[:, h * LANES:(h + 1) * LANES] = _rope(q[:, h * LANES:(h + 1) * LANES], tq_ref, MLA_ROPE // 4).astype(BF16)

    ckv = _rms(proj(C_CKV, C_KR), kvn_ref[...])
    ckv_ref[...] = ckv
    kr = _rope(proj(C_KR, C_GQ), tk_ref, MLA_ROPE // 4)
    kr_ref[...] = kr
    ckv_b = ckv.astype(BF16)
    km_ref[...] = (_dot(ckv_b, wk_ref[0:MLA_KV_LORA, :]) + _dot(kr.astype(BF16), wk_ref[MLA_KV_LORA:, :])).astype(BF16)
    vm_ref[...] = _dot(ckv_b, wv_ref[...]).astype(BF16)

    hq = proj(C_GQ, C_GK) * GQA_SCALE
    for j in range(GQA_HEADS * GQA_HD // LANES):
        qg_ref[:, j * LANES:(j + 1) * LANES] = _rope(hq[:, j * LANES:(j + 1) * LANES], tg_ref, GQA_HD // 4).astype(BF16)
    hk = proj(C_GK, C_GV)
    hv = proj(C_GV, C_G)
    for g in range(GQA_KV):
        kg_ref[:, g * LANES:(g + 1) * LANES] = _rope(hk[:, g * LANES:(g + 1) * LANES], tg_ref, GQA_HD // 4).astype(BF16)
    vg_ref[...] = hv.astype(BF16)
    low_half = lax.broadcasted_iota(I32, (TM, LANES), 1) < GQA_HD
    nk_ref[...] = jnp.where(low_half, hk[:, 0:LANES], hk[:, LANES:2 * LANES])
    nv_ref[...] = jnp.where(low_half, hv[:, 0:LANES], hv[:, LANES:2 * LANES])

    for j in range(3):
        sg_ref[:, j * D_MODEL:(j + 1) * D_MODEL] = _sigmoid(proj(C_G + j * D_MODEL, C_G + (j + 1) * D_MODEL)).astype(BF16)


def _in_proj(layer, xc, xl, mod, wp, qn, kvn, wuq, wk, wv, tab_g, tab_q, tab_k):
    tok = lambda w: pl.BlockSpec((TM, w), lambda i: (i, 0))
    per_layer = functools.partial(_layer_spec, layer)
    x_specs = [pl.BlockSpec((TM, D_MODEL), lambda i: (jnp.minimum(i, CTX_TILES - 1), 0)),
               pl.BlockSpec((TM, D_MODEL), lambda i: (jnp.maximum(i - CTX_TILES, 0), 0))]
    tab = pl.BlockSpec((3, TM, LANES), lambda i: (0, _pos_block(i), 0))
    out_widths = [(POOL_WIDTH, F32), (MLA_HEADS * LANES, BF16), (MLA_KV_LORA, F32), (LANES, F32),
                  (MLA_HEADS * LANES, BF16), (MLA_HEADS * MLA_V, BF16), (GQA_HEADS * GQA_HD, BF16),
                  (GQA_KV * LANES, BF16), (GQA_KV * LANES, BF16), (GQA_KV * GQA_HD, F32), (GQA_KV * GQA_HD, F32),
                  (3 * D_MODEL, BF16)]
    return pl.pallas_call(
        _in_kernel,
        grid=(N_TILES,),
        in_specs=x_specs + [
            pl.BlockSpec((None, 1, 1, 6 * D_MODEL), lambda i: (layer, _mod_row(i), 0, 0)),
            per_layer((D_MODEL, C_END)),
            per_layer((1, MLA_Q_LORA)),
            per_layer((1, MLA_KV_LORA)),
            per_layer((MLA_Q_LORA, MLA_HEADS * LANES)),
            per_layer((MLA_KV_LORA + LANES, MLA_HEADS * LANES)),
            per_layer((MLA_KV_LORA, MLA_HEADS * MLA_V)),
            tab, tab, tab,
        ],
        out_specs=[tok(w) for w, _ in out_widths],
        out_shape=[jax.ShapeDtypeStruct((T_ALL, w), dt) for w, dt in out_widths],
        compiler_params=_cparams(1),
        name="in_proj",
    )(xc, xl, mod, wp, qn, kvn, wuq, wk, wv, tab_g, tab_q, tab_k)


def _expand_kernel(ckv_ref, kr_ref, wk_ref, wv_ref, km_ref, vm_ref):
    ckv_b = ckv_ref[...].astype(BF16)
    km_ref[...] = (_dot(ckv_b, wk_ref[0:MLA_KV_LORA, :]) + _dot(kr_ref[...].astype(BF16), wk_ref[MLA_KV_LORA:, :])).astype(BF16)
    vm_ref[...] = _dot(ckv_b, wv_ref[...]).astype(BF16)


def _cache_block(layer):
    return lambda b, *_: (b * DEPTH + layer, 0)


def _expand_cache(layer, ckv, kr_pad, wk, wv):
    rows = B_LAT * PAST_LEN
    blk = PAST_LEN
    return pl.pallas_call(
        _expand_kernel,
        grid=(B_LAT,),
        in_specs=[
            pl.BlockSpec((blk, MLA_KV_LORA), _cache_block(layer)),
            pl.BlockSpec((blk, LANES), _cache_block(layer)),
            _layer_spec(layer, (MLA_KV_LORA + LANES, MLA_HEADS * LANES)),
            _layer_spec(layer, (MLA_KV_LORA, MLA_HEADS * MLA_V)),
        ],
        out_specs=[pl.BlockSpec((blk, MLA_HEADS * LANES), lambda i: (i, 0)),
                   pl.BlockSpec((blk, MLA_HEADS * MLA_V), lambda i: (i, 0))],
        out_shape=[jax.ShapeDtypeStruct((rows, MLA_HEADS * LANES), BF16),
                   jax.ShapeDtypeStruct((rows, MLA_HEADS * MLA_V), BF16)],
        compiler_params=_cparams(1),
        name="mla_expand_cache",
    )(ckv, kr_pad, wk, wv)


def _lane_chunks(x):
    return [x[:, c:c + LANES] for c in range(0, x.shape[1], LANES)]


def _softmax_sets(ss, extra=None):
    m = jnp.max(functools.reduce(jnp.maximum, [c for s in ss for c in _lane_chunks(s)]), axis=-1, keepdims=True)
    if extra is not None:
        m = jnp.maximum(m, extra)
    ps = [jnp.exp(s - m) for s in ss]
    den = jnp.sum(functools.reduce(jnp.add, [c for p in ps for c in _lane_chunks(p)]), axis=-1, keepdims=True)
    if extra is not None:
        den = den + jnp.exp(extra - m)
    return ps, den


def _mla_kernel(n_sets, q_ref, *refs):
    k_refs = refs[:n_sets]
    v_refs = refs[n_sets:2 * n_sets]
    o_ref = refs[2 * n_sets]
    rows = q_ref.shape[0]
    low_half = lax.broadcasted_iota(I32, (rows, LANES), 1) < MLA_V
    for j in range(MLA_HEADS // 2):
        outs = []
        for h in (2 * j, 2 * j + 1):
            qh = q_ref[:, h * LANES:(h + 1) * LANES]
            ps, den = _softmax_sets([_dot_t(qh, k[:, h * LANES:(h + 1) * LANES]) for k in k_refs])
            o = functools.reduce(jnp.add, [_dot(p.astype(BF16), v[:, j * LANES:(j + 1) * LANES]) for p, v in zip(ps, v_refs)])
            outs.append(o / den)
        o_ref[:, j * LANES:(j + 1) * LANES] = jnp.where(low_half, outs[0], outs[1]).astype(BF16)


def _mla_ctx(qm, km, vm):
    blk = S_CTX
    return pl.pallas_call(
        functools.partial(_mla_kernel, 1),
        grid=(B_CTX,),
        in_specs=[pl.BlockSpec((blk, MLA_HEADS * LANES), lambda b: (b, 0)),
                  pl.BlockSpec((blk, MLA_HEADS * LANES), lambda b: (b, 0)),
                  pl.BlockSpec((blk, MLA_HEADS * MLA_V), lambda b: (b, 0))],
        out_specs=pl.BlockSpec((blk, MLA_HEADS * MLA_V), lambda b: (b, 0)),
        out_shape=jax.ShapeDtypeStruct((T_CTX, MLA_HEADS * MLA_V), BF16),
        compiler_params=_cparams(1),
        name="mla_attn_ctx",
    )(qm, km, vm)


def _mla_lat(qm, km, vm, km_c, vm_c):
    lat0 = T_CTX // S_LAT
    tq = MLA_LAT_TQ
    n_q = S_LAT // tq
    return pl.pallas_call(
        functools.partial(_mla_kernel, 2),
        grid=(B_LAT, n_q),
        in_specs=[pl.BlockSpec((tq, MLA_HEADS * LANES), lambda b, t: (T_CTX // tq + b * n_q + t, 0)),
                  pl.BlockSpec((PAST_LEN, MLA_HEADS * LANES), lambda b, t: (b, 0)),
                  pl.BlockSpec((S_LAT, MLA_HEADS * LANES), lambda b, t: (lat0 + b, 0)),
                  pl.BlockSpec((PAST_LEN, MLA_HEADS * MLA_V), lambda b, t: (b, 0)),
                  pl.BlockSpec((S_LAT, MLA_HEADS * MLA_V), lambda b, t: (lat0 + b, 0))],
        out_specs=pl.BlockSpec((tq, MLA_HEADS * MLA_V), lambda b, t: (b * n_q + t, 0)),
        out_shape=jax.ShapeDtypeStruct((T_LAT, MLA_HEADS * MLA_V), BF16),
        compiler_params=_cparams(2),
        name="mla_attn_lat",
    )(qm, km_c, km, vm_c, vm)


def _gqa_kernel(band, layer, sink_ref, q_ref, *refs):
    rows = q_ref.shape[0]
    n_band = rows // WBLK + 2 if band else 0
    n_sets = n_band + 1
    k_refs = refs[:n_sets]
    v_refs = refs[n_sets:2 * n_sets]
    o_ref = refs[2 * n_sets]
    stack = 2 if rows <= WBLK else 1
    low_half = lax.broadcasted_iota(I32, (rows, LANES), 1) < GQA_HD
    masks = [None] * n_sets
    if band:
        step = pl.program_id(1)
        n_blk = pl.num_programs(1) * (rows // WBLK)
        qi = lax.broadcasted_iota(I32, (stack * rows, WBLK), 0) % rows
        kj = lax.broadcasted_iota(I32, (stack * rows, WBLK), 1)
        for b in range(n_band):
            blk_id = step * (rows // WBLK) + b - 1
            missing = jnp.logical_or(blk_id < 0, blk_id >= n_blk)
            always_inside = (1 <= b <= rows // WBLK and (b - 1) * WBLK + WBLK - 1 <= WINDOW
                             and (b - 1) * WBLK - (rows - 1) >= -WINDOW)
            if always_inside:
                continue
            rel = (b - 1) * WBLK + kj - qi + jnp.where(missing, 4 * rows, 0)
            masks[b] = (rel >= -WINDOW) & (rel <= WINDOW)
    zero = jnp.zeros((rows, LANES), BF16)
    for j in range(GQA_HEADS // 2):
        g = j // (GQA_GROUP // 2)
        blk = q_ref[:, j * LANES:(j + 1) * LANES]
        q_heads = [jnp.where(low_half, blk, zero), jnp.where(low_half, zero, blk)]
        outs = []
        for h0 in range(0, 2, stack):
            qs = jnp.concatenate(q_heads[h0:h0 + stack], axis=0)
            sink = jnp.concatenate([jnp.full((rows, 1), sink_ref[layer * GQA_HEADS + 2 * j + h0 + i], F32)
                                    for i in range(stack)], axis=0)
            ss = []
            for k, msk in zip(k_refs, masks):
                s = _dot_t(qs, k[:, g * LANES:(g + 1) * LANES])
                ss.append(s if msk is None else jnp.where(msk, s, NEG_INF))
            ps, den = _softmax_sets(ss, sink)
            o = functools.reduce(jnp.add, [_dot(p.astype(BF16), v[:, g * LANES:(g + 1) * LANES]) for p, v in zip(ps, v_refs)])
            o = o / den
            outs += [o[i * rows:(i + 1) * rows] for i in range(stack)]
        o_ref[:, j * LANES:(j + 1) * LANES] = jnp.where(low_half, outs[0], outs[1]).astype(BF16)


def _gqa_ctx(layer, sink, qg, kg, vg):
    blk = S_CTX
    grid_spec = pltpu.PrefetchScalarGridSpec(
        num_scalar_prefetch=1,
        grid=(B_CTX,),
        in_specs=[pl.BlockSpec((blk, GQA_HEADS * GQA_HD), lambda b, s: (b, 0)),
                  pl.BlockSpec((blk, GQA_KV * LANES), lambda b, s: (b, 0)),
                  pl.BlockSpec((blk, GQA_KV * LANES), lambda b, s: (b, 0))],
        out_specs=pl.BlockSpec((blk, GQA_HEADS * GQA_HD), lambda b, s: (b, 0)),
    )
    return pl.pallas_call(
        functools.partial(_gqa_kernel, False, layer),
        grid_spec=grid_spec,
        out_shape=jax.ShapeDtypeStruct((T_CTX, GQA_HEADS * GQA_HD), BF16),
        compiler_params=_cparams(1),
        name="gqa_attn_ctx",
    )(sink, qg, kg, vg)


def _gqa_lat(layer, sink, qg, kg, vg, kg_c, vg_c):
    rows = GQA_LAT_ROWS
    nq = rows // WBLK
    n_band = nq + 2
    nb = S_LAT // WBLK
    first = T_CTX // WBLK

    def band_blk(off):
        def index_map(b, n, s):
            return (first + b * nb + jnp.clip(n * nq + off, 0, nb - 1), 0)
        return index_map

    kv_w = GQA_KV * LANES
    band_specs = [pl.BlockSpec((WBLK, kv_w), band_blk(j - 1)) for j in range(n_band)]
    grid_spec = pltpu.PrefetchScalarGridSpec(
        num_scalar_prefetch=1,
        grid=(B_LAT, S_LAT // rows),
        in_specs=[pl.BlockSpec((rows, GQA_HEADS * GQA_HD), lambda b, n, s: (T_CTX // rows + b * (S_LAT // rows) + n, 0))]
                 + band_specs + [pl.BlockSpec((PAST_LEN, kv_w), _cache_block(layer))]
                 + band_specs + [pl.BlockSpec((PAST_LEN, kv_w), _cache_block(layer))],
        out_specs=pl.BlockSpec((rows, GQA_HEADS * GQA_HD), lambda b, n, s: (b * (S_LAT // rows) + n, 0)),
    )
    return pl.pallas_call(
        functools.partial(_gqa_kernel, True, layer),
        grid_spec=grid_spec,
        out_shape=jax.ShapeDtypeStruct((T_LAT, GQA_HEADS * GQA_HD), BF16),
        compiler_params=_cparams(2),
        name="gqa_attn_lat",
    )(sink, qg, *([kg] * n_band), kg_c, *([vg] * n_band), vg_c)


def _pool_tile(i, hp_p, hp_c, hp_n, pw_ref, ps_ref):
    is_ctx = i < CTX_TILES
    t4 = (i - CTX_TILES) % LAT_TILES_PER_SEQ
    seq_len = jnp.where(is_ctx, S_CTX, S_LAT)
    base = jnp.where(is_ctx, 0, t4 * TM)

    n_keys = TM + 2 * HALO
    first_key = jnp.where(jnp.logical_or(is_ctx, t4 == 0), HALO, 0)
    end_key = jnp.where(jnp.logical_or(is_ctx, t4 == LAT_TILES_PER_SEQ - 1), HALO + TM, n_keys)
    qi = lax.broadcasted_iota(I32, (TM, n_keys), 0)
    kj = lax.broadcasted_iota(I32, (TM, n_keys), 1)
    rel = kj - HALO - qi
    key_ok = (kj >= first_key) & (kj < end_key)

    cur = hp_c[...]
    keys = jnp.concatenate([hp_p[TM - HALO:TM, :], cur, hp_n[0:HALO, :]], axis=0)
    k_hi, k_lo = _split(keys)
    qpos = base + lax.broadcasted_iota(I32, (TM, 1), 0)
    outs = []
    for g, w in enumerate(POOL_WINDOWS):
        a = jnp.where((rel >= -(w // 2)) & (rel <= w // 2 - 1) & key_ok, 1.0, 0.0).astype(BF16)
        cols = slice(g * POOL_GROUP, (g + 1) * POOL_GROUP)
        s = _dot(a, k_hi[:, cols]) + _dot(a, k_lo[:, cols])
        lo = jnp.maximum(qpos - w // 2, 0)
        hi = jnp.minimum(qpos + w // 2 - 1, seq_len - 1)
        cnt = (hi - lo + 1).astype(F32)
        d = s / cnt - cur[:, cols]
        y = _dot(d.astype(BF16), pw_ref[g]) * ps_ref[:, cols]
        outs.append(y.astype(BF16))
    return jnp.concatenate(outs, axis=1)


def _merge_kernel(xc_ref, xl_ref, mod_ref, hp_p, hp_c, hp_n, pw_ref, ps_ref, ymc_ref, yml_ref, ygc_ref, ygl_ref, sg_ref,
                  wbp_ref, wbm_ref, wbg_ref, wo_ref, lg_ref, lb_ref, rw_ref, rb_ref,
                  x1_ref, u2_ref, ri_ref, rwt_ref, cnt_ref, carry):
    i = pl.program_id(0)

    @pl.when(i == 0)
    def _():
        carry[...] = jnp.zeros_like(carry)

    is_ctx = i < CTX_TILES
    x = jnp.where(is_ctx, xc_ref[...], xl_ref[...])
    yp = _pool_tile(i, hp_p, hp_c, hp_n, pw_ref, ps_ref)
    ym = jnp.where(is_ctx, ymc_ref[...], yml_ref[...])
    yg = jnp.where(is_ctx, ygc_ref[...], ygl_ref[...])
    m = (sg_ref[:, 0:D_MODEL].astype(F32) * _dot(yp, wbp_ref[...])
         + sg_ref[:, D_MODEL:2 * D_MODEL].astype(F32) * _dot(ym, wbm_ref[...])
         + sg_ref[:, 2 * D_MODEL:3 * D_MODEL].astype(F32) * _dot(yg, wbg_ref[...]))
    y = _dot(m.astype(BF16), wo_ref[...])
    gate1 = mod_ref[0, :, 2 * D_MODEL:3 * D_MODEL]
    x1 = _layer_norm(ALPHA_DN * x + gate1 * y, lg_ref[...], lb_ref[...])
    x1_ref[...] = x1
    sh2 = mod_ref[0, :, 3 * D_MODEL:4 * D_MODEL]
    sc2 = mod_ref[0, :, 4 * D_MODEL:5 * D_MODEL]
    u2 = x1 * (1.0 + sc2) + sh2
    u2_ref[...] = u2

    u_hi, u_lo = _split(u2)
    rw_hi, rw_lo = _split(rw_ref[...])
    both = _dot(u_hi, jnp.concatenate([rw_hi, rw_lo], axis=1))
    logits = both[:, 0:LANES] + both[:, LANES:2 * LANES] + _dot(u_lo, rw_hi) + rb_ref[...]
    lane = lax.broadcasted_iota(I32, (TM, LANES), 1)
    lane_f = lane.astype(F32)
    vals, idxs = [], []
    rest = logits
    for _ in range(TOP_K):
        mx = jnp.max(rest, axis=-1, keepdims=True)
        ix = jnp.min(jnp.where(rest == mx, lane_f, float(LANES)), axis=-1, keepdims=True).astype(I32)
        vals.append(mx)
        idxs.append(ix)
        rest = jnp.where(lane == ix, -jnp.inf, rest)
    es = [jnp.exp(v - vals[0]) for v in vals]
    den = functools.reduce(jnp.add, es)

    sel = functools.reduce(jnp.add, [jnp.where(lane == ix, 1.0, 0.0) for ix in idxs])
    r_i = lax.broadcasted_iota(I32, (TM, TM), 0)
    c_i = lax.broadcasted_iota(I32, (TM, TM), 1)
    below = jnp.where(c_i < r_i, 1.0, 0.0).astype(BF16)
    rank = _dot(below, sel.astype(BF16)) + carry[...]
    carry[...] = carry[...] + jnp.sum(sel, axis=0, keepdims=True)
    cnt_ref[...] = carry[...]

    ri = jnp.zeros((TM, LANES), I32)
    rwt = jnp.zeros((TM, LANES), F32)
    for k in range(TOP_K):
        rk = jnp.sum(jnp.where(lane == idxs[k], rank, 0.0), axis=-1, keepdims=True).astype(I32)
        ri = jnp.where(lane == k, idxs[k], ri)
        ri = jnp.where(lane == TOP_K + k, rk, ri)
        rwt = jnp.where(lane == k, es[k] / den, rwt)
    ri_ref[...] = ri
    rwt_ref[...] = rwt


def _tok_spec(width):
    return pl.BlockSpec((TM, width), lambda i, *_: (i, 0))


def _ctx_spec(width):
    return pl.BlockSpec((TM, width), lambda i, *_: (jnp.minimum(i, CTX_TILES - 1), 0))


def _lat_spec(width):
    return pl.BlockSpec((TM, width), lambda i, *_: (jnp.maximum(i - CTX_TILES, 0), 0))


def _mod_spec(layer):
    return pl.BlockSpec((None, 1, 1, 6 * D_MODEL), lambda i, *_: (layer, _mod_row(i), 0, 0))


def _merge(layer, xc, xl, mod, hp, pw, ps, ymc, yml, ygc, ygl, sg, wbp, wbm, wbg, wo, lg, lb, rw, rb):
    tok, ctx, lat = _tok_spec, _ctx_spec, _lat_spec
    per_layer = functools.partial(_layer_spec, layer)
    return pl.pallas_call(
        _merge_kernel,
        grid=(N_TILES,),
        in_specs=[ctx(D_MODEL), lat(D_MODEL),
                  _mod_spec(layer),
                  pl.BlockSpec((TM, POOL_WIDTH), lambda i: (jnp.maximum(i - 1, 0), 0)),
                  tok(POOL_WIDTH),
                  pl.BlockSpec((TM, POOL_WIDTH), lambda i: (jnp.minimum(i + 1, N_TILES - 1), 0)),
                  per_layer((len(POOL_WINDOWS), POOL_GROUP, POOL_GROUP)),
                  per_layer((1, POOL_WIDTH)),
                  ctx(MLA_HEADS * MLA_V), lat(MLA_HEADS * MLA_V), ctx(GQA_HEADS * GQA_HD), lat(GQA_HEADS * GQA_HD),
                  tok(3 * D_MODEL),
                  per_layer((POOL_WIDTH, D_MODEL)), per_layer((MLA_HEADS * MLA_V, D_MODEL)),
                  per_layer((GQA_HEADS * GQA_HD, D_MODEL)),
                  per_layer((D_MODEL, D_MODEL)), per_layer((1, D_MODEL)), per_layer((1, D_MODEL)),
                  per_layer((D_MODEL, LANES)), per_layer((1, LANES))],
        out_specs=[tok(D_MODEL), tok(D_MODEL), tok(LANES), tok(LANES), _const_spec((1, LANES))],
        out_shape=[jax.ShapeDtypeStruct((T_ALL, D_MODEL), F32), jax.ShapeDtypeStruct((T_ALL, D_MODEL), F32),
                   jax.ShapeDtypeStruct((T_ALL, LANES), I32), jax.ShapeDtypeStruct((T_ALL, LANES), F32),
                   jax.ShapeDtypeStruct((1, LANES), F32)],
        scratch_shapes=[pltpu.VMEM((1, LANES), F32)],
        compiler_params=_cparams(1),
        name="merge_route",
    )(xc, xl, mod, hp, hp, hp, pw, ps, ymc, yml, ygc, ygl, sg, wbp, wbm, wbg, wo, lg, lb, rw, rb)


def _row_copy(src_ref, src_row, dst_ref, dst_row, sem):
    return pltpu.make_async_copy(src_ref.at[pl.ds(src_row, 1), :], dst_ref.at[pl.ds(dst_row, 1), :], sem)


ROWS_PER_ISSUE = 4
WAITS_PER_TRIP = 64


def _drain_rows(wait_one, n_rows):
    def trip(_, c):
        for _ in range(WAITS_PER_TRIP):
            wait_one()
        return c

    lax.fori_loop(0, n_rows // WAITS_PER_TRIP, trip, 0)


def _dispatch_kernel(dest_ref, zrow_ref, nu_ref, u_ref, xs_ref, zbuf, sem, zsem):
    i = pl.program_id(0)
    base = i * (TM * TOP_K)

    @pl.when(i == 0)
    def _():
        zbuf[...] = jnp.zeros_like(zbuf)

        def zero_tile(row):
            if not isinstance(row, int):
                row = pl.multiple_of(row, TME)
            return pltpu.make_async_copy(zbuf, xs_ref.at[pl.ds(row, TME), :], zsem)

        for e in range(N_EXPERTS):
            @pl.when(zrow_ref[e] >= 0)
            def _():
                zero_tile(zrow_ref[e]).start()

        def start_tail(j, c):
            zero_tile(j * TME).start()
            return c

        lax.fori_loop(nu_ref[0], NT_E, start_tail, 0)

        for e in range(N_EXPERTS):
            @pl.when(zrow_ref[e] >= 0)
            def _():
                zero_tile(0).wait()

        def wait_tail(j, c):
            zero_tile(0).wait()
            return c

        lax.fori_loop(nu_ref[0], NT_E, wait_tail, 0)

    def issue(g, c):
        for r in range(ROWS_PER_ISSUE):
            t = g * ROWS_PER_ISSUE + r
            for k in range(TOP_K):
                _row_copy(u_ref, t, xs_ref, dest_ref[base + t * TOP_K + k], sem).start(priority=k % 2)
        return c

    lax.fori_loop(0, TM // ROWS_PER_ISSUE, issue, 0)
    _drain_rows(lambda: _row_copy(u_ref, 0, xs_ref, 0, sem).wait(), TM * TOP_K)


def _dispatch(plan, u2):
    grid_spec = pltpu.PrefetchScalarGridSpec(
        num_scalar_prefetch=3,
        grid=(N_TILES,),
        in_specs=[pl.BlockSpec((TM, D_MODEL), lambda i, *_: (i, 0))],
        out_specs=pl.BlockSpec(memory_space=pl.ANY),
        scratch_shapes=[pltpu.VMEM((TME, D_MODEL), F32), pltpu.SemaphoreType.DMA(()), pltpu.SemaphoreType.DMA(())],
    )
    return pl.pallas_call(
        _dispatch_kernel,
        grid_spec=grid_spec,
        out_shape=jax.ShapeDtypeStruct((P_ROWS, D_MODEL), F32),
        compiler_params=_cparams(1),
        name="moe_dispatch",
    )(plan["dest"], plan["zero_row"], plan["n_used"], u2)


def _expert_kernel(layer, te_ref, nu_ref, slot_ref, first_ref, next_ref, half_ref,
                   xs_ref, wgu_hbm, bgu_ref, wd_hbm, bd_ref, ys_ref, wgu_f, wd_f, wgu_s, wd_s, sems):
    i = pl.program_id(0)

    def weight_copies(e, s):
        return (pltpu.make_async_copy(wgu_hbm.at[layer, e], wgu_f.at[s], sems.at[0, s]),
                pltpu.make_async_copy(wd_hbm.at[layer, e], wd_f.at[s], sems.at[1, s]))

    @pl.when(first_ref[i] == 1)
    def _():
        s = slot_ref[i]

        @pl.when(i == 0)
        def _():
            for cp in weight_copies(te_ref[i], s):
                cp.start()

        for cp in weight_copies(te_ref[i], s):
            cp.wait()

        @pl.when(next_ref[i] >= 0)
        def _():
            for cp in weight_copies(next_ref[i], 1 - s):
                cp.start()

        wgu_s[...] = wgu_f[s].astype(BF16)
        wd_s[...] = wd_f[s].astype(BF16)

    def mlp(rows):
        h = _dot(xs_ref[0:rows, :].astype(BF16), wgu_s[...]) + bgu_ref[0, 0]
        glu = jnp.minimum(h[:, 0:D_FF], SWIGLU_LIMIT)
        lin = jnp.clip(h[:, D_FF:2 * D_FF], -SWIGLU_LIMIT, SWIGLU_LIMIT)
        a = glu * _sigmoid(SWIGLU_ALPHA * glu) * (lin + 1.0)
        ys_ref[0:rows, :] = _dot(a.astype(BF16), wd_s[...]) + bd_ref[0, 0]

    in_use = i < nu_ref[0]

    @pl.when(jnp.logical_and(in_use, half_ref[i] == 0))
    def _():
        mlp(TME)

    @pl.when(jnp.logical_and(in_use, half_ref[i] == 1))
    def _():
        mlp(TME // 2)
        ys_ref[TME // 2:TME, :] = jnp.zeros((TME // 2, D_MODEL), F32)

    @pl.when(i >= nu_ref[0])
    def _():
        ys_ref[...] = jnp.zeros_like(ys_ref)


def _experts(layer, plan, xs, w_gate_up, b_gate_up, w_down, b_down):
    row = lambda i, te, nu, *_: (jnp.minimum(i, nu[0] - 1), 0)
    bias = lambda i, te, *_: (layer, te[i], 0, 0)
    grid_spec = pltpu.PrefetchScalarGridSpec(
        num_scalar_prefetch=6,
        grid=(NT_E,),
        in_specs=[pl.BlockSpec((TME, D_MODEL), row),
                  pl.BlockSpec(memory_space=pl.ANY),
                  pl.BlockSpec((1, 1, 1, 2 * D_FF), bias),
                  pl.BlockSpec(memory_space=pl.ANY),
                  pl.BlockSpec((1, 1, 1, D_MODEL), bias)],
        out_specs=pl.BlockSpec((TME, D_MODEL), lambda i, *_: (i, 0)),
        scratch_shapes=[pltpu.VMEM((2, D_MODEL, 2 * D_FF), F32), pltpu.VMEM((2, D_FF, D_MODEL), F32),
                        pltpu.VMEM((D_MODEL, 2 * D_FF), BF16), pltpu.VMEM((D_FF, D_MODEL), BF16),
                        pltpu.SemaphoreType.DMA((2, 2))],
    )
    return pl.pallas_call(
        functools.partial(_expert_kernel, layer),
        grid_spec=grid_spec,
        out_shape=jax.ShapeDtypeStruct((P_ROWS, D_MODEL), F32),
        compiler_params=_cparams(1),
        name="moe_experts",
    )(plan["tile_expert"], plan["n_used"], plan["tile_slot"], plan["tile_first"], plan["tile_next"], plan["tile_half"],
      xs, w_gate_up,
      b_gate_up.reshape(DEPTH, N_EXPERTS, 1, 2 * D_FF), w_down, b_down.reshape(DEPTH, N_EXPERTS, 1, D_MODEL))


def _combine_kernel(dest_ref, ys_ref, rwt_ref, x1_ref, mod_ref, lg_ref, lb_ref, oc_ref, ol_ref, buf, sems):
    i = pl.program_id(0)
    slot = jnp.bitwise_and(i, 1)

    def gather(tile, s):
        base = tile * (TM * TOP_K)

        def issue(g, c):
            for r in range(ROWS_PER_ISSUE):
                t = g * ROWS_PER_ISSUE + r
                for k in range(TOP_K):
                    _row_copy(ys_ref, dest_ref[base + t * TOP_K + k], buf.at[s, k], t, sems.at[s]).start(priority=k % 2)
            return c

        lax.fori_loop(0, TM // ROWS_PER_ISSUE, issue, 0)

    @pl.when(i == 0)
    def _():
        gather(0, 0)

    @pl.when(i + 1 < pl.num_programs(0))
    def _():
        gather(i + 1, 1 - slot)

    _drain_rows(lambda: _row_copy(ys_ref, 0, buf.at[slot, 0], 0, sems.at[slot]).wait(), TM * TOP_K)

    moe = functools.reduce(jnp.add, [rwt_ref[:, k:k + 1] * buf[slot, k] for k in range(TOP_K)])
    gate2 = mod_ref[0, :, 5 * D_MODEL:6 * D_MODEL]
    out = _layer_norm(ALPHA_DN * x1_ref[...] + gate2 * moe, lg_ref[...], lb_ref[...])

    @pl.when(i < CTX_TILES)
    def _():
        oc_ref[...] = out

    @pl.when(i >= CTX_TILES)
    def _():
        ol_ref[...] = out


def _combine(layer, dest, ys, rwt, x1, mod, lg, lb):
    grid_spec = pltpu.PrefetchScalarGridSpec(
        num_scalar_prefetch=1,
        grid=(N_TILES,),
        in_specs=[pl.BlockSpec(memory_space=pl.ANY),
                  pl.BlockSpec((TM, LANES), lambda i, d: (i, 0)),
                  pl.BlockSpec((TM, D_MODEL), lambda i, d: (i, 0)),
                  _mod_spec(layer),
                  _layer_spec(layer, (1, D_MODEL)),
                  _layer_spec(layer, (1, D_MODEL))],
        out_specs=[_ctx_spec(D_MODEL), _lat_spec(D_MODEL)],
        scratch_shapes=[pltpu.VMEM((2, TOP_K, TM, D_MODEL), F32), pltpu.SemaphoreType.DMA((2,))],
    )
    return pl.pallas_call(
        _combine_kernel,
        grid_spec=grid_spec,
        out_shape=[jax.ShapeDtypeStruct((T_CTX, D_MODEL), F32), jax.ShapeDtypeStruct((T_LAT, D_MODEL), F32)],
        compiler_params=_cparams(1),
        name="moe_combine",
    )(dest, ys, rwt, x1, mod, lg, lb)


def _pad_heads(w, n_heads, width):
    lead = w.shape[:-1]
    w = w.reshape(lead + (n_heads, width))
    pad = ((0, 0),) * (len(lead) + 1) + ((0, LANES - width),)
    return jnp.pad(w, pad).reshape(lead + (n_heads * LANES,))


def _dup_heads(w):
    lead = w.shape[:-1]
    w = w.reshape(lead + (GQA_KV, 1, GQA_HD))
    return jnp.concatenate([w, w], axis=-2).reshape(lead + (GQA_KV * LANES,))


def _pack_w_in(w):
    cuts = [0, 512, 896, 1152, 1184, 1696, 1824, 1952, 5024]
    hp, cq, ckv, kr, q, k, v, g = [w[..., a:b] for a, b in zip(cuts[:-1], cuts[1:])]
    packed = jnp.concatenate([hp, cq, ckv, _pad_heads(kr, 1, MLA_ROPE), q, _dup_heads(k), _dup_heads(v), g], axis=-1)
    return packed.astype(BF16)


def _pack_mla(wuq, wukv):
    wuq_p = _pad_heads(wuq, MLA_HEADS, MLA_QK).astype(BF16)
    kv = wukv.reshape(DEPTH, MLA_KV_LORA, MLA_HEADS, MLA_NOPE + MLA_V)
    wk_top = _pad_heads(kv[..., :MLA_NOPE].reshape(DEPTH, MLA_KV_LORA, MLA_HEADS * MLA_NOPE), MLA_HEADS, MLA_NOPE)
    place = jnp.zeros((LANES, MLA_HEADS, LANES), F32)
    r = jnp.arange(MLA_ROPE)
    place = place.at[r, :, MLA_NOPE + r].set(1.0).reshape(1, LANES, MLA_HEADS * LANES)
    wk = jnp.concatenate([wk_top, jnp.broadcast_to(place, (DEPTH,) + place.shape[1:])], axis=1).astype(BF16)
    wv = kv[..., MLA_NOPE:].reshape(DEPTH, MLA_KV_LORA, MLA_HEADS * MLA_V).astype(BF16)
    return wuq_p, wk, wv


def _rope_table(dim, lane0):
    quarter = dim // 4
    pos = jnp.arange(S_LAT)
    rows = (pos // GRID_W).astype(F32)
    cols = (pos % GRID_W).astype(F32)
    freqs = jnp.power(ROPE_THETA, -jnp.arange(quarter, dtype=F32) / quarter)
    ang_r = rows[:, None] * freqs[None, :]
    ang_c = cols[:, None] * freqs[None, :]
    zero = jnp.zeros((S_LAT, quarter), F32)
    cos = jnp.concatenate([jnp.cos(ang_r), jnp.cos(ang_r), jnp.cos(ang_c), jnp.cos(ang_c)], axis=1)
    s_up = jnp.concatenate([-jnp.sin(ang_r), zero, -jnp.sin(ang_c), zero], axis=1)
    s_dn = jnp.concatenate([zero, jnp.sin(ang_r), zero, jnp.sin(ang_c)], axis=1)

    def place(t, fill):
        if lane0 is None:
            return jnp.tile(t, (1, LANES // dim))
        full = jnp.full((S_LAT, LANES), fill, F32)
        return full.at[:, lane0:lane0 + dim].set(t)

    planes = jnp.stack([place(cos, 1.0), place(s_up, 0.0), place(s_dn, 0.0)])
    ident = jnp.stack([jnp.ones((TM, LANES), F32), jnp.zeros((TM, LANES), F32), jnp.zeros((TM, LANES), F32)])
    return jnp.concatenate([planes, ident], axis=1)


TILE_LANES = 2 * LANES
ROW_EXPERT, ROW_SLOT, ROW_FIRST, ROW_NEXT, ROW_ZERO, ROW_USED, ROW_HALF = range(7)


def _plan_kernel(ri_ref, cnt_ref, dest_ref, tiles_ref):
    lane = lax.broadcasted_iota(I32, (1, LANES), 1)
    r_i = lax.broadcasted_iota(I32, (LANES, LANES), 0)
    c_i = lax.broadcasted_iota(I32, (LANES, LANES), 1)
    upto = jnp.where(r_i <= c_i, 1.0, 0.0).astype(BF16)

    def prefix(row):
        return _dot(jnp.broadcast_to(row, (SUBLANES, LANES)).astype(BF16), upto)[0:1, :]

    cnt = cnt_ref[...]
    t_cnt = jnp.floor((cnt + (TME - 1)) * (1.0 / TME))
    t_end = prefix(t_cnt)
    t_start = t_end - t_cnt
    n_used = jnp.max(t_end, axis=-1, keepdims=True)

    tok_lane = lax.broadcasted_iota(I32, ri_ref.shape, 1)
    ri = ri_ref[...]
    dest = jnp.zeros(ri_ref.shape, I32)
    for k in range(TOP_K):
        start_k = jnp.sum(jnp.where(tok_lane == ri[:, k:k + 1], t_start, 0.0), axis=-1, keepdims=True)
        dest_k = start_k.astype(I32) * TME + ri[:, TOP_K + k:TOP_K + k + 1]
        dest = jnp.where(tok_lane == k, dest_k, dest)
    dest_ref[...] = dest

    @pl.when(pl.program_id(0) == 0)
    def _():
        eye = jnp.where(r_i == c_i, 1.0, 0.0).astype(BF16)

        def to_col(row):
            return _dot_t(eye, jnp.broadcast_to(row, (SUBLANES, LANES)).astype(BF16))[:, 0:1]

        e_col = lax.broadcasted_iota(I32, (LANES, 1), 0)
        real_e = e_col < N_EXPERTS
        tile = lax.broadcasted_iota(I32, (LANES, TILE_LANES), 1).astype(F32)
        end_col = to_col(t_end)
        expert = jnp.sum(jnp.where(real_e & (end_col <= tile), 1.0, 0.0), axis=0, keepdims=True)
        last = jnp.sum(jnp.where(real_e & (end_col <= n_used - 1.0), 1.0, 0.0), axis=0, keepdims=True)
        expert = jnp.minimum(expert, last)
        mine = e_col.astype(F32) == expert

        def per_tile(col):
            return jnp.sum(jnp.where(mine, col, 0.0), axis=0, keepdims=True)

        has = t_cnt > 0.0
        ordinal = prefix(jnp.where(has, 1.0, 0.0)) - 1.0
        ord_col = to_col(ordinal)
        slot = per_tile(ord_col - 2.0 * jnp.floor(ord_col * 0.5))
        tile_row = tile[0:1, :]
        first = jnp.where((tile_row < n_used) & (tile_row == per_tile(to_col(t_start))), 1.0, 0.0)
        later = jnp.where(has & (lane > e_col), lane.astype(F32), float(LANES))
        nxt_col = jnp.min(later, axis=-1, keepdims=True)
        nxt_col = jnp.where(nxt_col == float(LANES), -1.0, nxt_col)
        nxt = per_tile(nxt_col)
        zero_row = jnp.where(has, (t_end - 1.0) * TME, -1.0)
        zero_row = jnp.concatenate([zero_row, jnp.full((1, TILE_LANES - LANES), -1.0, F32)], axis=1)
        used = jnp.broadcast_to(n_used, (1, TILE_LANES))
        short_tail = jnp.where(has & (cnt - (t_cnt - 1.0) * TME <= TME // 2), 1.0, 0.0)
        is_last = (tile_row < n_used) & (tile_row == per_tile(end_col) - 1.0)
        half = jnp.where(is_last, per_tile(to_col(short_tail)), 0.0)
        pad = jnp.zeros((SUBLANES - 7, TILE_LANES), F32)
        tiles_ref[...] = jnp.concatenate([expert, slot, first, nxt, zero_row, used, half, pad], axis=0).astype(I32)


def _routing_plan(ri, counts):
    rows = 4 * TM
    dest, tiles = pl.pallas_call(
        _plan_kernel,
        grid=(T_ALL // rows,),
        in_specs=[pl.BlockSpec((rows, LANES), lambda i: (i, 0)), _const_spec((1, LANES))],
        out_specs=[pl.BlockSpec((rows, LANES), lambda i: (i, 0)), _const_spec((SUBLANES, TILE_LANES))],
        out_shape=[jax.ShapeDtypeStruct((T_ALL, LANES), I32), jax.ShapeDtypeStruct((SUBLANES, TILE_LANES), I32)],
        compiler_params=_cparams(1),
        name="moe_plan",
    )(ri, counts)
    return {"dest": dest[:, 0:TOP_K].reshape(-1), "tile_expert": tiles[ROW_EXPERT, :NT_E],
            "n_used": tiles[ROW_USED, 0:1], "tile_slot": tiles[ROW_SLOT, :NT_E], "tile_first": tiles[ROW_FIRST, :NT_E],
            "tile_next": tiles[ROW_NEXT, :NT_E], "tile_half": tiles[ROW_HALF, :NT_E],
            "zero_row": tiles[ROW_ZERO, :N_EXPERTS]}


def kernel(x_prompt, x_sample, c, cache_mla_ckv, cache_mla_krope, cache_gqa_k, cache_gqa_v, c_ctx, w_ada, b_ada, w_in, mla_q_norm, mla_kv_norm, w_mla_uq, w_mla_ukv, gqa_sink, pool_w, pool_scale, w_branch_pool, w_branch_mla, w_branch_gqa, w_out, ln1_g, ln1_b, ln2_g, ln2_b, router_w, router_b, w_gate_up, b_gate_up, w_down, b_down):
    xc = x_prompt.reshape(T_CTX, D_MODEL)
    xl = x_sample.reshape(T_LAT, D_MODEL)
    cond8 = jnp.concatenate([c_ctx[None, :], c, jnp.zeros((SUBLANES - 1 - B_LAT, D_MODEL), F32)], axis=0)
    mod_all = _ada_params(cond8, w_ada, b_ada)

    tab_g = _rope_table(GQA_HD, None)
    tab_q = _rope_table(MLA_ROPE, MLA_NOPE)
    tab_k = _rope_table(MLA_ROPE, 0)

    mod_all = mod_all.reshape(DEPTH, SUBLANES, 1, 6 * D_MODEL)
    wp = _pack_w_in(w_in)
    wuq_p, wk, wv = _pack_mla(w_mla_uq, w_mla_ukv)
    row = lambda a: a.reshape(DEPTH, 1, a.shape[-1])
    q_norm, kv_norm, pool_s = row(mla_q_norm), row(mla_kv_norm), row(pool_scale)
    ln1g, ln1b, ln2g, ln2b = row(ln1_g), row(ln1_b), row(ln2_g), row(ln2_b)
    pool_wb = pool_w.astype(BF16)
    wbp, wbm, wbg, wo = (w.astype(BF16) for w in (w_branch_pool, w_branch_mla, w_branch_gqa, w_out))
    rw = jnp.pad(router_w, ((0, 0), (0, 0), (0, LANES - N_EXPERTS)))
    rb = jnp.pad(router_b, ((0, 0), (0, LANES - N_EXPERTS)), constant_values=-jnp.inf).reshape(DEPTH, 1, LANES)
    sinks = gqa_sink.reshape(DEPTH * GQA_HEADS)
    n_cache = B_LAT * DEPTH * PAST_LEN
    ckv_c = cache_mla_ckv.reshape(n_cache, MLA_KV_LORA)
    kr_c = jnp.pad(cache_mla_krope.reshape(n_cache, MLA_ROPE), ((0, 0), (0, LANES - MLA_ROPE)))
    gk_c = _dup_heads(cache_gqa_k.reshape(n_cache, GQA_KV * GQA_HD)).astype(BF16)
    gv_c = _dup_heads(cache_gqa_v.reshape(n_cache, GQA_KV * GQA_HD)).astype(BF16)

    outs = {"ckv": [], "kr": [], "gk": [], "gv": []}
    for l in range(DEPTH):
        (hp, qm, ckv, kr, km, vm, qg, kg, vg, nk, nv, sg) = _in_proj(
            l, xc, xl, mod_all, wp, q_norm, kv_norm, wuq_p, wk, wv, tab_g, tab_q, tab_k)
        outs["ckv"].append(ckv[:T_CTX].reshape(B_CTX, S_CTX, MLA_KV_LORA))
        outs["kr"].append(kr[:T_CTX, :MLA_ROPE].reshape(B_CTX, S_CTX, MLA_ROPE))
        outs["gk"].append(nk[:T_CTX].reshape(B_CTX, S_CTX, GQA_KV, GQA_HD))
        outs["gv"].append(nv[:T_CTX].reshape(B_CTX, S_CTX, GQA_KV, GQA_HD))

        km_c, vm_c = _expand_cache(l, ckv_c, kr_c, wk, wv)
        ymc = _mla_ctx(qm, km, vm)
        yml = _mla_lat(qm, km, vm, km_c, vm_c)
        ygc = _gqa_ctx(l, sinks, qg, kg, vg)
        ygl = _gqa_lat(l, sinks, qg, kg, vg, gk_c, gv_c)

        x1, u2, ri, rwt, counts = _merge(
            l, xc, xl, mod_all, hp, pool_wb, pool_s, ymc, yml, ygc, ygl, sg, wbp, wbm, wbg, wo, ln1g, ln1b, rw, rb)

        plan = _routing_plan(ri, counts)
        xs = _dispatch(plan, u2)
        ys = _experts(l, plan, xs, w_gate_up, b_gate_up, w_down, b_down)
        xc, xl = _combine(l, plan["dest"], ys, rwt, x1, mod_all, ln2g, ln2b)

    y_prompt = xc.reshape(B_CTX, S_CTX, D_MODEL)
    y_sample = xl.reshape(B_LAT, S_LAT, D_MODEL)
    return (y_prompt, y_sample, jnp.stack(outs["ckv"], axis=1), jnp.stack(outs["kr"], axis=1),
            jnp.stack(outs["gk"], axis=1), jnp.stack(outs["gv"], axis=1))
```

```python
import functools
import math

import jax
import jax.numpy as jnp
from jax import lax
from jax.experimental import pallas as pl
from jax.experimental.pallas import tpu as pltpu

F32 = jnp.float32
BF16 = jnp.bfloat16
I32 = jnp.int32

D_MODEL = 1024
DEPTH = 2
B_CTX, S_CTX = 16, 256
B_LAT, S_LAT = 4, 1024
PAST_LEN = 512
GRID_W = 64
ROPE_THETA = 10000.0
ALPHA_DN = (2 * DEPTH) ** 0.25
LN_EPS = 1e-5
RMS_EPS = 1e-6
NEG_INF = -1e30
POOL_WINDOWS = (2, 4, 8, 16)
POOL_GROUP = 128
POOL_WIDTH = 512
MLA_HEADS = 8
MLA_NOPE = 64
MLA_ROPE = 32
MLA_V = 64
MLA_QK = MLA_NOPE + MLA_ROPE
MLA_Q_LORA = 384
MLA_KV_LORA = 256
MLA_SCALE = 1.0 / math.sqrt(MLA_QK)
GQA_HEADS = 8
GQA_KV = 2
GQA_HD = 64
GQA_GROUP = GQA_HEADS // GQA_KV
GQA_SCALE = 1.0 / math.sqrt(GQA_HD)
WINDOW = 128
WBLK = 128
N_EXPERTS = 32
TOP_K = 4
D_FF = 1024
SWIGLU_LIMIT = 7.0
SWIGLU_ALPHA = 1.702

LANES = 128
SUBLANES = 8

T_CTX = B_CTX * S_CTX
T_LAT = B_LAT * S_LAT
T_ALL = T_CTX + T_LAT
TM = 256
N_TILES = T_ALL // TM
CTX_TILES = T_CTX // TM
LAT_TILES_PER_SEQ = S_LAT // TM
HALO = 16
GQA_LAT_ROWS = 256
MLA_LAT_TQ = 512

C_HP = 0
C_CQ = C_HP + POOL_WIDTH
C_CKV = C_CQ + MLA_Q_LORA
C_KR = C_CKV + MLA_KV_LORA
C_GQ = C_KR + LANES
C_GK = C_GQ + GQA_HEADS * GQA_HD
C_GV = C_GK + GQA_KV * LANES
C_G = C_GV + GQA_KV * LANES
C_END = C_G + 3 * D_MODEL

TME = 256
N_PAIRS = T_ALL * TOP_K
NT_E = N_PAIRS // TME + N_EXPERTS
P_ROWS = NT_E * TME


MIB = 1024 * 1024


def _cparams(n_axes, vmem_mib=16):
    return pltpu.CompilerParams(dimension_semantics=("arbitrary",) * n_axes, vmem_limit_bytes=vmem_mib * MIB)


def _dot(a, b):
    return jnp.dot(a, b, preferred_element_type=F32)


def _dot_t(a, b):
    return lax.dot_general(a, b, (((1,), (1,)), ((), ())), preferred_element_type=F32)


def _split(x):
    hi = x.astype(BF16)
    lo = (x - hi.astype(F32)).astype(BF16)
    return hi, lo


def _dot3(a, b):
    a_hi, a_lo = _split(a)
    b_hi, b_lo = _split(b)
    return _dot(a_hi, b_hi) + _dot(a_hi, b_lo) + _dot(a_lo, b_hi)


def _sigmoid(x):
    return 1.0 / (1.0 + jnp.exp(-x))


def _layer_norm(z, g, b):
    mu = jnp.mean(z, axis=-1, keepdims=True)
    zc = z - mu
    var = jnp.mean(zc * zc, axis=-1, keepdims=True)
    return zc * lax.rsqrt(var + LN_EPS) * g + b


def _mod_row(i):
    return jnp.where(i < CTX_TILES, 0, 1 + (i - CTX_TILES) // LAT_TILES_PER_SEQ)


def _pos_block(i):
    return jnp.where(i < CTX_TILES, LAT_TILES_PER_SEQ, (i - CTX_TILES) % LAT_TILES_PER_SEQ)


def _const_spec(shape):
    nd = len(shape)
    return pl.BlockSpec(shape, lambda *_: (0,) * nd)


def _layer_spec(layer, shape):
    nd = len(shape)
    return pl.BlockSpec((None,) + tuple(shape), lambda *_: (layer,) + (0,) * nd)


def _ada_kernel(c_ref, w_ref, b_ref, o_ref):
    c = c_ref[...]
    s = c * _sigmoid(c)
    o_ref[0] = _dot3(s, w_ref[0]) + b_ref[0]


def _ada_params(cond8, w_ada, b_ada):
    n_col = 6 * D_MODEL
    blk = 1024
    return pl.pallas_call(
        _ada_kernel,
        grid=(DEPTH, n_col // blk),
        in_specs=[
            pl.BlockSpec((SUBLANES, D_MODEL), lambda l, j: (0, 0)),
            pl.BlockSpec((1, D_MODEL, blk), lambda l, j: (l, 0, j)),
            pl.BlockSpec((1, 1, blk), lambda l, j: (l, 0, j)),
        ],
        out_specs=pl.BlockSpec((1, SUBLANES, blk), lambda l, j: (l, 0, j)),
        out_shape=jax.ShapeDtypeStruct((DEPTH, SUBLANES, n_col), F32),
        compiler_params=_cparams(2, vmem_mib=32),
        name="ada_params",
    )(cond8, w_ada, b_ada.reshape(DEPTH, 1, n_col))


def _rope(x, tab_ref, shift):
    return (x * tab_ref[0]
            + pltpu.roll(x, LANES - shift, 1) * tab_ref[1]
            + pltpu.roll(x, shift, 1) * tab_ref[2])


def _rms(h, g):
    return h * lax.rsqrt(jnp.mean(h * h, axis=-1, keepdims=True) + RMS_EPS) * g


def _in_kernel(xc_ref, xl_ref, mod_ref, wp_ref, qn_ref, kvn_ref, wuq_ref, wk_ref, wv_ref, tg_ref, tq_ref, tk_ref,
               hp_ref, qm_ref, ckv_ref, kr_ref, km_ref, vm_ref, qg_ref, kg_ref, vg_ref, nk_ref, nv_ref, sg_ref):
    x = jnp.where(pl.program_id(0) < CTX_TILES, xc_ref[...], xl_ref[...])
    sh = mod_ref[0, :, 0:D_MODEL]
    sc = mod_ref[0, :, D_MODEL:2 * D_MODEL]
    u = (x * (1.0 + sc) + sh).astype(BF16)

    def proj(lo, hi):
        return _dot(u, wp_ref[:, lo:hi])

    hp_ref[...] = proj(C_HP, C_CQ)

    qn = _rms(proj(C_CQ, C_CKV), qn_ref[...]).astype(BF16)
    q = _dot(qn, wuq_ref[...]) * MLA_SCALE
    for h in range(MLA_HEADS):
        qm_ref[:, h * LANES:(h + 1) * LANES] = _rope(q[:, h * LANES:(h + 1) * LANES], tq_ref, MLA_ROPE // 4).astype(BF16)

    ckv = _rms(proj(C_CKV, C_KR), kvn_ref[...])
    ckv_ref[...] = ckv
    kr = _rope(proj(C_KR, C_GQ), tk_ref, MLA_ROPE // 4)
    kr_ref[...] = kr
    ckv_b = ckv.astype(BF16)
    km_ref[...] = (_dot(ckv_b, wk_ref[0:MLA_KV_LORA, :]) + _dot(kr.astype(BF16), wk_ref[MLA_KV_LORA:, :])).astype(BF16)
    vm_ref[...] = _dot(ckv_b, wv_ref[...]).astype(BF16)

    hq = proj(C_GQ, C_GK) * GQA_SCALE
    for j in range(GQA_HEADS * GQA_HD // LANES):
        qg_ref[:, j * LANES:(j + 1) * LANES] = _rope(hq[:, j * LANES:(j + 1) * LANES], tg_ref, GQA_HD // 4).astype(BF16)
    hk = proj(C_GK, C_GV)
    hv = proj(C_GV, C_G)
    for g in range(GQA_KV):
        kg_ref[:, g * LANES:(g + 1) * LANES] = _rope(hk[:, g * LANES:(g + 1) * LANES], tg_ref, GQA_HD // 4).astype(BF16)
    vg_ref[...] = hv.astype(BF16)
    low_half = lax.broadcasted_iota(I32, (TM, LANES), 1) < GQA_HD
    nk_ref[...] = jnp.where(low_half, hk[:, 0:LANES], hk[:, LANES:2 * LANES])
    nv_ref[...] = jnp.where(low_half, hv[:, 0:LANES], hv[:, LANES:2 * LANES])

    for j in range(3):
        sg_ref[:, j * D_MODEL:(j + 1) * D_MODEL] = _sigmoid(proj(C_G + j * D_MODEL, C_G + (j + 1) * D_MODEL)).astype(BF16)


def _in_proj(layer, xc, xl, mod, wp, qn, kvn, wuq, wk, wv, tab_g, tab_q, tab_k):
    tok = lambda w: pl.BlockSpec((TM, w), lambda i: (i, 0))
    per_layer = functools.partial(_layer_spec, layer)
    x_specs = [pl.BlockSpec((TM, D_MODEL), lambda i: (jnp.minimum(i, CTX_TILES - 1), 0)),
               pl.BlockSpec((TM, D_MODEL), lambda i: (jnp.maximum(i - CTX_TILES, 0), 0))]
    tab = pl.BlockSpec((3, TM, LANES), lambda i: (0, _pos_block(i), 0))
    out_widths = [(POOL_WIDTH, F32), (MLA_HEADS * LANES, BF16), (MLA_KV_LORA, F32), (LANES, F32),
                  (MLA_HEADS * LANES, BF16), (MLA_HEADS * MLA_V, BF16), (GQA_HEADS * GQA_HD, BF16),
                  (GQA_KV * LANES, BF16), (GQA_KV * LANES, BF16), (GQA_KV * GQA_HD, F32), (GQA_KV * GQA_HD, F32),
                  (3 * D_MODEL, BF16)]
    return pl.pallas_call(
        _in_kernel,
        grid=(N_TILES,),
        in_specs=x_specs + [
            pl.BlockSpec((None, 1, 1, 6 * D_MODEL), lambda i: (layer, _mod_row(i), 0, 0)),
            per_layer((D_MODEL, C_END)),
            per_layer((1, MLA_Q_LORA)),
            per_layer((1, MLA_KV_LORA)),
            per_layer((MLA_Q_LORA, MLA_HEADS * LANES)),
            per_layer((MLA_KV_LORA + LANES, MLA_HEADS * LANES)),
            per_layer((MLA_KV_LORA, MLA_HEADS * MLA_V)),
            tab, tab, tab,
        ],
        out_specs=[tok(w) for w, _ in out_widths],
        out_shape=[jax.ShapeDtypeStruct((T_ALL, w), dt) for w, dt in out_widths],
        compiler_params=_cparams(1, vmem_mib=48),
        name="in_proj",
    )(xc, xl, mod, wp, qn, kvn, wuq, wk, wv, tab_g, tab_q, tab_k)


def _expand_kernel(ckv_ref, kr_ref, wk_ref, wv_ref, km_ref, vm_ref):
    ckv_b = ckv_ref[...].astype(BF16)
    km_ref[...] = (_dot(ckv_b, wk_ref[0:MLA_KV_LORA, :]) + _dot(kr_ref[...].astype(BF16), wk_ref[MLA_KV_LORA:, :])).astype(BF16)
    vm_ref[...] = _dot(ckv_b, wv_ref[...]).astype(BF16)


def _cache_block(layer):
    return lambda b, *_: (b * DEPTH + layer, 0)


def _expand_cache(layer, ckv, kr_pad, wk, wv):
    rows = B_LAT * PAST_LEN
    blk = PAST_LEN
    return pl.pallas_call(
        _expand_kernel,
        grid=(B_LAT,),
        in_specs=[
            pl.BlockSpec((blk, MLA_KV_LORA), _cache_block(layer)),
            pl.BlockSpec((blk, LANES), _cache_block(layer)),
            _layer_spec(layer, (MLA_KV_LORA + LANES, MLA_HEADS * LANES)),
            _layer_spec(layer, (MLA_KV_LORA, MLA_HEADS * MLA_V)),
        ],
        out_specs=[pl.BlockSpec((blk, MLA_HEADS * LANES), lambda i: (i, 0)),
                   pl.BlockSpec((blk, MLA_HEADS * MLA_V), lambda i: (i, 0))],
        out_shape=[jax.ShapeDtypeStruct((rows, MLA_HEADS * LANES), BF16),
                   jax.ShapeDtypeStruct((rows, MLA_HEADS * MLA_V), BF16)],
        compiler_params=_cparams(1),
        name="mla_expand_cache",
    )(ckv, kr_pad, wk, wv)


def _lane_chunks(x):
    return [x[:, c:c + LANES] for c in range(0, x.shape[1], LANES)]


def _softmax_sets(ss, extra=None):
    m = jnp.max(functools.reduce(jnp.maximum, [c for s in ss for c in _lane_chunks(s)]), axis=-1, keepdims=True)
    if extra is not None:
        m = jnp.maximum(m, extra)
    ps = [jnp.exp(s - m) for s in ss]
    den = jnp.sum(functools.reduce(jnp.add, [c for p in ps for c in _lane_chunks(p)]), axis=-1, keepdims=True)
    if extra is not None:
        den = den + jnp.exp(extra - m)
    return ps, den


def _mla_kernel(n_sets, q_ref, *refs):
    k_refs = refs[:n_sets]
    v_refs = refs[n_sets:2 * n_sets]
    o_ref = refs[2 * n_sets]
    rows = q_ref.shape[0]
    low_half = lax.broadcasted_iota(I32, (rows, LANES), 1) < MLA_V
    for j in range(MLA_HEADS // 2):
        outs = []
        for h in (2 * j, 2 * j + 1):
            qh = q_ref[:, h * LANES:(h + 1) * LANES]
            ps, den = _softmax_sets([_dot_t(qh, k[:, h * LANES:(h + 1) * LANES]) for k in k_refs])
            o = functools.reduce(jnp.add, [_dot(p.astype(BF16), v[:, j * LANES:(j + 1) * LANES]) for p, v in zip(ps, v_refs)])
            outs.append(o / den)
        o_ref[:, j * LANES:(j + 1) * LANES] = jnp.where(low_half, outs[0], outs[1]).astype(BF16)


def _mla_ctx(qm, km, vm):
    blk = S_CTX
    return pl.pallas_call(
        functools.partial(_mla_kernel, 1),
        grid=(B_CTX,),
        in_specs=[pl.BlockSpec((blk, MLA_HEADS * LANES), lambda b: (b, 0)),
                  pl.BlockSpec((blk, MLA_HEADS * LANES), lambda b: (b, 0)),
                  pl.BlockSpec((blk, MLA_HEADS * MLA_V), lambda b: (b, 0))],
        out_specs=pl.BlockSpec((blk, MLA_HEADS * MLA_V), lambda b: (b, 0)),
        out_shape=jax.ShapeDtypeStruct((T_CTX, MLA_HEADS * MLA_V), BF16),
        compiler_params=_cparams(1),
        name="mla_attn_ctx",
    )(qm, km, vm)


def _mla_lat(qm, km, vm, km_c, vm_c):
    lat0 = T_CTX // S_LAT
    tq = MLA_LAT_TQ
    n_q = S_LAT // tq
    return pl.pallas_call(
        functools.partial(_mla_kernel, 2),
        grid=(B_LAT, n_q),
        in_specs=[pl.BlockSpec((tq, MLA_HEADS * LANES), lambda b, t: (T_CTX // tq + b * n_q + t, 0)),
                  pl.BlockSpec((PAST_LEN, MLA_HEADS * LANES), lambda b, t: (b, 0)),
                  pl.BlockSpec((S_LAT, MLA_HEADS * LANES), lambda b, t: (lat0 + b, 0)),
                  pl.BlockSpec((PAST_LEN, MLA_HEADS * MLA_V), lambda b, t: (b, 0)),
                  pl.BlockSpec((S_LAT, MLA_HEADS * MLA_V), lambda b, t: (lat0 + b, 0))],
        out_specs=pl.BlockSpec((tq, MLA_HEADS * MLA_V), lambda b, t: (b * n_q + t, 0)),
        out_shape=jax.ShapeDtypeStruct((T_LAT, MLA_HEADS * MLA_V), BF16),
        compiler_params=_cparams(2, vmem_mib=48),
        name="mla_attn_lat",
    )(qm, km_c, km, vm_c, vm)


def _gqa_kernel(band, layer, sink_ref, q_ref, *refs):
    rows = q_ref.shape[0]
    n_band = rows // WBLK + 2 if band else 0
    n_sets = n_band + 1
    k_refs = refs[:n_sets]
    v_refs = refs[n_sets:2 * n_sets]
    o_ref = refs[2 * n_sets]
    stack = 2 if rows <= WBLK else 1
    low_half = lax.broadcasted_iota(I32, (rows, LANES), 1) < GQA_HD
    masks = [None] * n_sets
    if band:
        step = pl.program_id(1)
        n_blk = pl.num_programs(1) * (rows // WBLK)
        qi = lax.broadcasted_iota(I32, (stack * rows, WBLK), 0) % rows
        kj = lax.broadcasted_iota(I32, (stack * rows, WBLK), 1)
        for b in range(n_band):
            blk_id = step * (rows // WBLK) + b - 1
            missing = jnp.logical_or(blk_id < 0, blk_id >= n_blk)
            always_inside = (1 <= b <= rows // WBLK and (b - 1) * WBLK + WBLK - 1 <= WINDOW
                             and (b - 1) * WBLK - (rows - 1) >= -WINDOW)
            if always_inside:
                continue
            rel = (b - 1) * WBLK + kj - qi + jnp.where(missing, 4 * rows, 0)
            masks[b] = (rel >= -WINDOW) & (rel <= WINDOW)
    zero = jnp.zeros((rows, LANES), BF16)
    for j in range(GQA_HEADS // 2):
        g = j // (GQA_GROUP // 2)
        blk = q_ref[:, j * LANES:(j + 1) * LANES]
        q_heads = [jnp.where(low_half, blk, zero), jnp.where(low_half, zero, blk)]
        outs = []
        for h0 in range(0, 2, stack):
            qs = jnp.concatenate(q_heads[h0:h0 + stack], axis=0)
            sink = jnp.concatenate([jnp.full((rows, 1), sink_ref[layer * GQA_HEADS + 2 * j + h0 + i], F32)
                                    for i in range(stack)], axis=0)
            ss = []
            for k, msk in zip(k_refs, masks):
                s = _dot_t(qs, k[:, g * LANES:(g + 1) * LANES])
                ss.append(s if msk is None else jnp.where(msk, s, NEG_INF))
            ps, den = _softmax_sets(ss, sink)
            o = functools.reduce(jnp.add, [_dot(p.astype(BF16), v[:, g * LANES:(g + 1) * LANES]) for p, v in zip(ps, v_refs)])
            o = o / den
            outs += [o[i * rows:(i + 1) * rows] for i in range(stack)]
        o_ref[:, j * LANES:(j + 1) * LANES] = jnp.where(low_half, outs[0], outs[1]).astype(BF16)


def _gqa_ctx(layer, sink, qg, kg, vg):
    blk = S_CTX
    grid_spec = pltpu.PrefetchScalarGridSpec(
        num_scalar_prefetch=1,
        grid=(B_CTX,),
        in_specs=[pl.BlockSpec((blk, GQA_HEADS * GQA_HD), lambda b, s: (b, 0)),
                  pl.BlockSpec((blk, GQA_KV * LANES), lambda b, s: (b, 0)),
                  pl.BlockSpec((blk, GQA_KV * LANES), lambda b, s: (b, 0))],
        out_specs=pl.BlockSpec((blk, GQA_HEADS * GQA_HD), lambda b, s: (b, 0)),
    )
    return pl.pallas_call(
        functools.partial(_gqa_kernel, False, layer),
        grid_spec=grid_spec,
        out_shape=jax.ShapeDtypeStruct((T_CTX, GQA_HEADS * GQA_HD), BF16),
        compiler_params=_cparams(1),
        name="gqa_attn_ctx",
    )(sink, qg, kg, vg)


def _gqa_lat(layer, sink, qg, kg, vg, kg_c, vg_c):
    rows = GQA_LAT_ROWS
    nq = rows // WBLK
    n_band = nq + 2
    nb = S_LAT // WBLK
    first = T_CTX // WBLK

    def band_blk(off):
        def index_map(b, n, s):
            return (first + b * nb + jnp.clip(n * nq + off, 0, nb - 1), 0)
        return index_map

    kv_w = GQA_KV * LANES
    band_specs = [pl.BlockSpec((WBLK, kv_w), band_blk(j - 1)) for j in range(n_band)]
    grid_spec = pltpu.PrefetchScalarGridSpec(
        num_scalar_prefetch=1,
        grid=(B_LAT, S_LAT // rows),
        in_specs=[pl.BlockSpec((rows, GQA_HEADS * GQA_HD), lambda b, n, s: (T_CTX // rows + b * (S_LAT // rows) + n, 0))]
                 + band_specs + [pl.BlockSpec((PAST_LEN, kv_w), _cache_block(layer))]
                 + band_specs + [pl.BlockSpec((PAST_LEN, kv_w), _cache_block(layer))],
        out_specs=pl.BlockSpec((rows, GQA_HEADS * GQA_HD), lambda b, n, s: (b * (S_LAT // rows) + n, 0)),
    )
    return pl.pallas_call(
        functools.partial(_gqa_kernel, True, layer),
        grid_spec=grid_spec,
        out_shape=jax.ShapeDtypeStruct((T_LAT, GQA_HEADS * GQA_HD), BF16),
        compiler_params=_cparams(2),
        name="gqa_attn_lat",
    )(sink, qg, *([kg] * n_band), kg_c, *([vg] * n_band), vg_c)


def _pool_tile(i, hp_p, hp_c, hp_n, pw_ref, ps_ref):
    is_ctx = i < CTX_TILES
    t4 = (i - CTX_TILES) % LAT_TILES_PER_SEQ
    seq_len = jnp.where(is_ctx, S_CTX, S_LAT)
    base = jnp.where(is_ctx, 0, t4 * TM)

    n_keys = TM + 2 * HALO
    first_key = jnp.where(jnp.logical_or(is_ctx, t4 == 0), HALO, 0)
    end_key = jnp.where(jnp.logical_or(is_ctx, t4 == LAT_TILES_PER_SEQ - 1), HALO + TM, n_keys)
    qi = lax.broadcasted_iota(I32, (TM, n_keys), 0)
    kj = lax.broadcasted_iota(I32, (TM, n_keys), 1)
    rel = kj - HALO - qi
    key_ok = (kj >= first_key) & (kj < end_key)

    cur = hp_c[...]
    keys = jnp.concatenate([hp_p[TM - HALO:TM, :], cur, hp_n[0:HALO, :]], axis=0)
    k_hi, k_lo = _split(keys)
    qpos = base + lax.broadcasted_iota(I32, (TM, 1), 0)
    outs = []
    for g, w in enumerate(POOL_WINDOWS):
        a = jnp.where((rel >= -(w // 2)) & (rel <= w // 2 - 1) & key_ok, 1.0, 0.0).astype(BF16)
        cols = slice(g * POOL_GROUP, (g + 1) * POOL_GROUP)
        s = _dot(a, k_hi[:, cols]) + _dot(a, k_lo[:, cols])
        lo = jnp.maximum(qpos - w // 2, 0)
        hi = jnp.minimum(qpos + w // 2 - 1, seq_len - 1)
        cnt = (hi - lo + 1).astype(F32)
        d = s / cnt - cur[:, cols]
        y = _dot(d.astype(BF16), pw_ref[g]) * ps_ref[:, cols]
        outs.append(y.astype(BF16))
    return jnp.concatenate(outs, axis=1)


def _merge_kernel(xc_ref, xl_ref, mod_ref, hp_p, hp_c, hp_n, pw_ref, ps_ref, ymc_ref, yml_ref, ygc_ref, ygl_ref, sg_ref,
                  wbp_ref, wbm_ref, wbg_ref, wo_ref, lg_ref, lb_ref, rw_ref, rb_ref,
                  x1_ref, u2_ref, ri_ref, rwt_ref, cnt_ref, carry):
    i = pl.program_id(0)

    @pl.when(i == 0)
    def _():
        carry[...] = jnp.zeros_like(carry)

    is_ctx = i < CTX_TILES
    x = jnp.where(is_ctx, xc_ref[...], xl_ref[...])
    yp = _pool_tile(i, hp_p, hp_c, hp_n, pw_ref, ps_ref)
    ym = jnp.where(is_ctx, ymc_ref[...], yml_ref[...])
    yg = jnp.where(is_ctx, ygc_ref[...], ygl_ref[...])
    m = (sg_ref[:, 0:D_MODEL].astype(F32) * _dot(yp, wbp_ref[...])
         + sg_ref[:, D_MODEL:2 * D_MODEL].astype(F32) * _dot(ym, wbm_ref[...])
         + sg_ref[:, 2 * D_MODEL:3 * D_MODEL].astype(F32) * _dot(yg, wbg_ref[...]))
    y = _dot(m.astype(BF16), wo_ref[...])
    gate1 = mod_ref[0, :, 2 * D_MODEL:3 * D_MODEL]
    x1 = _layer_norm(ALPHA_DN * x + gate1 * y, lg_ref[...], lb_ref[...])
    x1_ref[...] = x1
    sh2 = mod_ref[0, :, 3 * D_MODEL:4 * D_MODEL]
    sc2 = mod_ref[0, :, 4 * D_MODEL:5 * D_MODEL]
    u2 = x1 * (1.0 + sc2) + sh2
    u2_ref[...] = u2

    u_hi, u_lo = _split(u2)
    rw_hi, rw_lo = _split(rw_ref[...])
    both = _dot(u_hi, jnp.concatenate([rw_hi, rw_lo], axis=1))
    logits = both[:, 0:LANES] + both[:, LANES:2 * LANES] + _dot(u_lo, rw_hi) + rb_ref[...]
    lane = lax.broadcasted_iota(I32, (TM, LANES), 1)
    lane_f = lane.astype(F32)
    vals, idxs = [], []
    rest = logits
    for _ in range(TOP_K):
        mx = jnp.max(rest, axis=-1, keepdims=True)
        ix = jnp.min(jnp.where(rest == mx, lane_f, float(LANES)), axis=-1, keepdims=True).astype(I32)
        vals.append(mx)
        idxs.append(ix)
        rest = jnp.where(lane == ix, -jnp.inf, rest)
    es = [jnp.exp(v - vals[0]) for v in vals]
    den = functools.reduce(jnp.add, es)

    sel = functools.reduce(jnp.add, [jnp.where(lane == ix, 1.0, 0.0) for ix in idxs])
    r_i = lax.broadcasted_iota(I32, (TM, TM), 0)
    c_i = lax.broadcasted_iota(I32, (TM, TM), 1)
    below = jnp.where(c_i < r_i, 1.0, 0.0).astype(BF16)
    rank = _dot(below, sel.astype(BF16)) + carry[...]
    carry[...] = carry[...] + jnp.sum(sel, axis=0, keepdims=True)
    cnt_ref[...] = carry[...]

    ri = jnp.zeros((TM, LANES), I32)
    rwt = jnp.zeros((TM, LANES), F32)
    for k in range(TOP_K):
        rk = jnp.sum(jnp.where(lane == idxs[k], rank, 0.0), axis=-1, keepdims=True).astype(I32)
        ri = jnp.where(lane == k, idxs[k], ri)
        ri = jnp.where(lane == TOP_K + k, rk, ri)
        rwt = jnp.where(lane == k, es[k] / den, rwt)
    ri_ref[...] = ri
    rwt_ref[...] = rwt


def _tok_spec(width):
    return pl.BlockSpec((TM, width), lambda i, *_: (i, 0))


def _ctx_spec(width):
    return pl.BlockSpec((TM, width), lambda i, *_: (jnp.minimum(i, CTX_TILES - 1), 0))


def _lat_spec(width):
    return pl.BlockSpec((TM, width), lambda i, *_: (jnp.maximum(i - CTX_TILES, 0), 0))


def _mod_spec(layer):
    return pl.BlockSpec((None, 1, 1, 6 * D_MODEL), lambda i, *_: (layer, _mod_row(i), 0, 0))


def _merge(layer, xc, xl, mod, hp, pw, ps, ymc, yml, ygc, ygl, sg, wbp, wbm, wbg, wo, lg, lb, rw, rb):
    tok, ctx, lat = _tok_spec, _ctx_spec, _lat_spec
    per_layer = functools.partial(_layer_spec, layer)
    return pl.pallas_call(
        _merge_kernel,
        grid=(N_TILES,),
        in_specs=[ctx(D_MODEL), lat(D_MODEL),
                  _mod_spec(layer),
                  pl.BlockSpec((TM, POOL_WIDTH), lambda i: (jnp.maximum(i - 1, 0), 0)),
                  tok(POOL_WIDTH),
                  pl.BlockSpec((TM, POOL_WIDTH), lambda i: (jnp.minimum(i + 1, N_TILES - 1), 0)),
                  per_layer((len(POOL_WINDOWS), POOL_GROUP, POOL_GROUP)),
                  per_layer((1, POOL_WIDTH)),
                  ctx(MLA_HEADS * MLA_V), lat(MLA_HEADS * MLA_V), ctx(GQA_HEADS * GQA_HD), lat(GQA_HEADS * GQA_HD),
                  tok(3 * D_MODEL),
                  per_layer((POOL_WIDTH, D_MODEL)), per_layer((MLA_HEADS * MLA_V, D_MODEL)),
                  per_layer((GQA_HEADS * GQA_HD, D_MODEL)),
                  per_layer((D_MODEL, D_MODEL)), per_layer((1, D_MODEL)), per_layer((1, D_MODEL)),
                  per_layer((D_MODEL, LANES)), per_layer((1, LANES))],
        out_specs=[tok(D_MODEL), tok(D_MODEL), tok(LANES), tok(LANES), _const_spec((1, LANES))],
        out_shape=[jax.ShapeDtypeStruct((T_ALL, D_MODEL), F32), jax.ShapeDtypeStruct((T_ALL, D_MODEL), F32),
                   jax.ShapeDtypeStruct((T_ALL, LANES), I32), jax.ShapeDtypeStruct((T_ALL, LANES), F32),
                   jax.ShapeDtypeStruct((1, LANES), F32)],
        scratch_shapes=[pltpu.VMEM((1, LANES), F32)],
        compiler_params=_cparams(1, vmem_mib=40),
        name="merge_route",
    )(xc, xl, mod, hp, hp, hp, pw, ps, ymc, yml, ygc, ygl, sg, wbp, wbm, wbg, wo, lg, lb, rw, rb)


def _row_copy(src_ref, src_row, dst_ref, dst_row, sem):
    return pltpu.make_async_copy(src_ref.at[pl.ds(src_row, 1), :], dst_ref.at[pl.ds(dst_row, 1), :], sem)


ROWS_PER_ISSUE = 4
WAITS_PER_TRIP = 64


def _drain_rows(wait_one, n_rows):
    def trip(_, c):
        for _ in range(WAITS_PER_TRIP):
            wait_one()
        return c

    lax.fori_loop(0, n_rows // WAITS_PER_TRIP, trip, 0)


def _dispatch_kernel(dest_ref, zrow_ref, nu_ref, u_ref, xs_ref, zbuf, sem, zsem):
    i = pl.program_id(0)
    base = i * (TM * TOP_K)

    @pl.when(i == 0)
    def _():
        zbuf[...] = jnp.zeros_like(zbuf)

        def zero_tile(row):
            if not isinstance(row, int):
                row = pl.multiple_of(row, TME)
            return pltpu.make_async_copy(zbuf, xs_ref.at[pl.ds(row, TME), :], zsem)

        for e in range(N_EXPERTS):
            @pl.when(zrow_ref[e] >= 0)
            def _():
                zero_tile(zrow_ref[e]).start()

        def start_tail(j, c):
            zero_tile(j * TME).start()
            return c

        lax.fori_loop(nu_ref[0], NT_E, start_tail, 0)

        for e in range(N_EXPERTS):
            @pl.when(zrow_ref[e] >= 0)
            def _():
                zero_tile(0).wait()

        def wait_tail(j, c):
            zero_tile(0).wait()
            return c

        lax.fori_loop(nu_ref[0], NT_E, wait_tail, 0)

    def issue(g, c):
        for r in range(ROWS_PER_ISSUE):
            t = g * ROWS_PER_ISSUE + r
            for k in range(TOP_K):
                _row_copy(u_ref, t, xs_ref, dest_ref[base + t * TOP_K + k], sem).start(priority=k % 2)
        return c

    lax.fori_loop(0, TM // ROWS_PER_ISSUE, issue, 0)
    _drain_rows(lambda: _row_copy(u_ref, 0, xs_ref, 0, sem).wait(), TM * TOP_K)


def _dispatch(plan, u2):
    grid_spec = pltpu.PrefetchScalarGridSpec(
        num_scalar_prefetch=3,
        grid=(N_TILES,),
        in_specs=[pl.BlockSpec((TM, D_MODEL), lambda i, *_: (i, 0))],
        out_specs=pl.BlockSpec(memory_space=pl.ANY),
        scratch_shapes=[pltpu.VMEM((TME, D_MODEL), F32), pltpu.SemaphoreType.DMA(()), pltpu.SemaphoreType.DMA(())],
    )
    return pl.pallas_call(
        _dispatch_kernel,
        grid_spec=grid_spec,
        out_shape=jax.ShapeDtypeStruct((P_ROWS, D_MODEL), F32),
        compiler_params=_cparams(1),
        name="moe_dispatch",
    )(plan["dest"], plan["zero_row"], plan["n_used"], u2)


def _expert_kernel(layer, te_ref, nu_ref, slot_ref, first_ref, next_ref, half_ref,
                   xs_ref, wgu_hbm, bgu_ref, wd_hbm, bd_ref, ys_ref, wgu_f, wd_f, wgu_s, wd_s, sems):
    i = pl.program_id(0)

    def weight_copies(e, s):
        return (pltpu.make_async_copy(wgu_hbm.at[layer, e], wgu_f.at[s], sems.at[0, s]),
                pltpu.make_async_copy(wd_hbm.at[layer, e], wd_f.at[s], sems.at[1, s]))

    @pl.when(first_ref[i] == 1)
    def _():
        s = slot_ref[i]

        @pl.when(i == 0)
        def _():
            for cp in weight_copies(te_ref[i], s):
                cp.start()

        for cp in weight_copies(te_ref[i], s):
            cp.wait()

        @pl.when(next_ref[i] >= 0)
        def _():
            for cp in weight_copies(next_ref[i], 1 - s):
                cp.start()

        wgu_s[...] = wgu_f[s].astype(BF16)
        wd_s[...] = wd_f[s].astype(BF16)

    def mlp(rows):
        h = _dot(xs_ref[0:rows, :].astype(BF16), wgu_s[...]) + bgu_ref[0, 0]
        glu = jnp.minimum(h[:, 0:D_FF], SWIGLU_LIMIT)
        lin = jnp.clip(h[:, D_FF:2 * D_FF], -SWIGLU_LIMIT, SWIGLU_LIMIT)
        a = glu * _sigmoid(SWIGLU_ALPHA * glu) * (lin + 1.0)
        ys_ref[0:rows, :] = _dot(a.astype(BF16), wd_s[...]) + bd_ref[0, 0]

    in_use = i < nu_ref[0]

    @pl.when(jnp.logical_and(in_use, half_ref[i] == 0))
    def _():
        mlp(TME)

    @pl.when(jnp.logical_and(in_use, half_ref[i] == 1))
    def _():
        mlp(TME // 2)
        ys_ref[TME // 2:TME, :] = jnp.zeros((TME // 2, D_MODEL), F32)

    @pl.when(i >= nu_ref[0])
    def _():
        ys_ref[...] = jnp.zeros_like(ys_ref)


def _experts(layer, plan, xs, w_gate_up, b_gate_up, w_down, b_down):
    row = lambda i, te, nu, *_: (jnp.minimum(i, nu[0] - 1), 0)
    bias = lambda i, te, *_: (layer, te[i], 0, 0)
    grid_spec = pltpu.PrefetchScalarGridSpec(
        num_scalar_prefetch=6,
        grid=(NT_E,),
        in_specs=[pl.BlockSpec((TME, D_MODEL), row),
                  pl.BlockSpec(memory_space=pl.ANY),
                  pl.BlockSpec((1, 1, 1, 2 * D_FF), bias),
                  pl.BlockSpec(memory_space=pl.ANY),
                  pl.BlockSpec((1, 1, 1, D_MODEL), bias)],
        out_specs=pl.BlockSpec((TME, D_MODEL), lambda i, *_: (i, 0)),
        scratch_shapes=[pltpu.VMEM((2, D_MODEL, 2 * D_FF), F32), pltpu.VMEM((2, D_FF, D_MODEL), F32),
                        pltpu.VMEM((D_MODEL, 2 * D_FF), BF16), pltpu.VMEM((D_FF, D_MODEL), BF16),
                        pltpu.SemaphoreType.DMA((2, 2))],
    )
    return pl.pallas_call(
        functools.partial(_expert_kernel, layer),
        grid_spec=grid_spec,
        out_shape=jax.ShapeDtypeStruct((P_ROWS, D_MODEL), F32),
        compiler_params=_cparams(1, vmem_mib=48),
        name="moe_experts",
    )(plan["tile_expert"], plan["n_used"], plan["tile_slot"], plan["tile_first"], plan["tile_next"], plan["tile_half"],
      xs, w_gate_up,
      b_gate_up.reshape(DEPTH, N_EXPERTS, 1, 2 * D_FF), w_down, b_down.reshape(DEPTH, N_EXPERTS, 1, D_MODEL))


def _combine_kernel(dest_ref, ys_ref, rwt_ref, x1_ref, mod_ref, lg_ref, lb_ref, oc_ref, ol_ref, buf, sems):
    i = pl.program_id(0)
    slot = jnp.bitwise_and(i, 1)

    def gather(tile, s):
        base = tile * (TM * TOP_K)

        def issue(g, c):
            for r in range(ROWS_PER_ISSUE):
                t = g * ROWS_PER_ISSUE + r
                for k in range(TOP_K):
                    _row_copy(ys_ref, dest_ref[base + t * TOP_K + k], buf.at[s, k], t, sems.at[s]).start(priority=k % 2)
            return c

        lax.fori_loop(0, TM // ROWS_PER_ISSUE, issue, 0)

    @pl.when(i == 0)
    def _():
        gather(0, 0)

    @pl.when(i + 1 < pl.num_programs(0))
    def _():
        gather(i + 1, 1 - slot)

    _drain_rows(lambda: _row_copy(ys_ref, 0, buf.at[slot, 0], 0, sems.at[slot]).wait(), TM * TOP_K)

    moe = functools.reduce(jnp.add, [rwt_ref[:, k:k + 1] * buf[slot, k] for k in range(TOP_K)])
    gate2 = mod_ref[0, :, 5 * D_MODEL:6 * D_MODEL]
    out = _layer_norm(ALPHA_DN * x1_ref[...] + gate2 * moe, lg_ref[...], lb_ref[...])

    @pl.when(i < CTX_TILES)
    def _():
        oc_ref[...] = out

    @pl.when(i >= CTX_TILES)
    def _():
        ol_ref[...] = out


def _combine(layer, dest, ys, rwt, x1, mod, lg, lb):
    grid_spec = pltpu.PrefetchScalarGridSpec(
        num_scalar_prefetch=1,
        grid=(N_TILES,),
        in_specs=[pl.BlockSpec(memory_space=pl.ANY),
                  pl.BlockSpec((TM, LANES), lambda i, d: (i, 0)),
                  pl.BlockSpec((TM, D_MODEL), lambda i, d: (i, 0)),
                  _mod_spec(layer),
                  _layer_spec(layer, (1, D_MODEL)),
                  _layer_spec(layer, (1, D_MODEL))],
        out_specs=[_ctx_spec(D_MODEL), _lat_spec(D_MODEL)],
        scratch_shapes=[pltpu.VMEM((2, TOP_K, TM, D_MODEL), F32), pltpu.SemaphoreType.DMA((2,))],
    )
    return pl.pallas_call(
        _combine_kernel,
        grid_spec=grid_spec,
        out_shape=[jax.ShapeDtypeStruct((T_CTX, D_MODEL), F32), jax.ShapeDtypeStruct((T_LAT, D_MODEL), F32)],
        compiler_params=_cparams(1, vmem_mib=32),
        name="moe_combine",
    )(dest, ys, rwt, x1, mod, lg, lb)


def _pad_heads(w, n_heads, width):
    lead = w.shape[:-1]
    w = w.reshape(lead + (n_heads, width))
    pad = ((0, 0),) * (len(lead) + 1) + ((0, LANES - width),)
    return jnp.pad(w, pad).reshape(lead + (n_heads * LANES,))


def _dup_heads(w):
    lead = w.shape[:-1]
    w = w.reshape(lead + (GQA_KV, 1, GQA_HD))
    return jnp.concatenate([w, w], axis=-2).reshape(lead + (GQA_KV * LANES,))


def _pack_w_in(w):
    cuts = [0, 512, 896, 1152, 1184, 1696, 1824, 1952, 5024]
    hp, cq, ckv, kr, q, k, v, g = [w[..., a:b] for a, b in zip(cuts[:-1], cuts[1:])]
    packed = jnp.concatenate([hp, cq, ckv, _pad_heads(kr, 1, MLA_ROPE), q, _dup_heads(k), _dup_heads(v), g], axis=-1)
    return packed.astype(BF16)


def _pack_mla(wuq, wukv):
    wuq_p = _pad_heads(wuq, MLA_HEADS, MLA_QK).astype(BF16)
    kv = wukv.reshape(DEPTH, MLA_KV_LORA, MLA_HEADS, MLA_NOPE + MLA_V)
    wk_top = _pad_heads(kv[..., :MLA_NOPE].reshape(DEPTH, MLA_KV_LORA, MLA_HEADS * MLA_NOPE), MLA_HEADS, MLA_NOPE)
    place = jnp.zeros((LANES, MLA_HEADS, LANES), F32)
    r = jnp.arange(MLA_ROPE)
    place = place.at[r, :, MLA_NOPE + r].set(1.0).reshape(1, LANES, MLA_HEADS * LANES)
    wk = jnp.concatenate([wk_top, jnp.broadcast_to(place, (DEPTH,) + place.shape[1:])], axis=1).astype(BF16)
    wv = kv[..., MLA_NOPE:].reshape(DEPTH, MLA_KV_LORA, MLA_HEADS * MLA_V).astype(BF16)
    return wuq_p, wk, wv


def _rope_table(dim, lane0):
    quarter = dim // 4
    pos = jnp.arange(S_LAT)
    rows = (pos // GRID_W).astype(F32)
    cols = (pos % GRID_W).astype(F32)
    freqs = jnp.power(ROPE_THETA, -jnp.arange(quarter, dtype=F32) / quarter)
    ang_r = rows[:, None] * freqs[None, :]
    ang_c = cols[:, None] * freqs[None, :]
    zero = jnp.zeros((S_LAT, quarter), F32)
    cos = jnp.concatenate([jnp.cos(ang_r), jnp.cos(ang_r), jnp.cos(ang_c), jnp.cos(ang_c)], axis=1)
    s_up = jnp.concatenate([-jnp.sin(ang_r), zero, -jnp.sin(ang_c), zero], axis=1)
    s_dn = jnp.concatenate([zero, jnp.sin(ang_r), zero, jnp.sin(ang_c)], axis=1)

    def place(t, fill):
        if lane0 is None:
            return jnp.tile(t, (1, LANES // dim))
        full = jnp.full((S_LAT, LANES), fill, F32)
        return full.at[:, lane0:lane0 + dim].set(t)

    planes = jnp.stack([place(cos, 1.0), place(s_up, 0.0), place(s_dn, 0.0)])
    ident = jnp.stack([jnp.ones((TM, LANES), F32), jnp.zeros((TM, LANES), F32), jnp.zeros((TM, LANES), F32)])
    return jnp.concatenate([planes, ident], axis=1)


TILE_LANES = 2 * LANES
ROW_EXPERT, ROW_SLOT, ROW_FIRST, ROW_NEXT, ROW_ZERO, ROW_USED, ROW_HALF = range(7)


def _plan_kernel(ri_ref, cnt_ref, dest_ref, tiles_ref):
    lane = lax.broadcasted_iota(I32, (1, LANES), 1)
    r_i = lax.broadcasted_iota(I32, (LANES, LANES), 0)
    c_i = lax.broadcasted_iota(I32, (LANES, LANES), 1)
    upto = jnp.where(r_i <= c_i, 1.0, 0.0).astype(BF16)

    def prefix(row):
        return _dot(jnp.broadcast_to(row, (SUBLANES, LANES)).astype(BF16), upto)[0:1, :]

    cnt = cnt_ref[...]
    t_cnt = jnp.floor((cnt + (TME - 1)) * (1.0 / TME))
    t_end = prefix(t_cnt)
    t_start = t_end - t_cnt
    n_used = jnp.max(t_end, axis=-1, keepdims=True)

    tok_lane = lax.broadcasted_iota(I32, ri_ref.shape, 1)
    ri = ri_ref[...]
    dest = jnp.zeros(ri_ref.shape, I32)
    for k in range(TOP_K):
        start_k = jnp.sum(jnp.where(tok_lane == ri[:, k:k + 1], t_start, 0.0), axis=-1, keepdims=True)
        dest_k = start_k.astype(I32) * TME + ri[:, TOP_K + k:TOP_K + k + 1]
        dest = jnp.where(tok_lane == k, dest_k, dest)
    dest_ref[...] = dest

    @pl.when(pl.program_id(0) == 0)
    def _():
        eye = jnp.where(r_i == c_i, 1.0, 0.0).astype(BF16)

        def to_col(row):
            return _dot_t(eye, jnp.broadcast_to(row, (SUBLANES, LANES)).astype(BF16))[:, 0:1]

        e_col = lax.broadcasted_iota(I32, (LANES, 1), 0)
        real_e = e_col < N_EXPERTS
        tile = lax.broadcasted_iota(I32, (LANES, TILE_LANES), 1).astype(F32)
        end_col = to_col(t_end)
        expert = jnp.sum(jnp.where(real_e & (end_col <= tile), 1.0, 0.0), axis=0, keepdims=True)
        last = jnp.sum(jnp.where(real_e & (end_col <= n_used - 1.0), 1.0, 0.0), axis=0, keepdims=True)
        expert = jnp.minimum(expert, last)
        mine = e_col.astype(F32) == expert

        def per_tile(col):
            return jnp.sum(jnp.where(mine, col, 0.0), axis=0, keepdims=True)

        has = t_cnt > 0.0
        ordinal = prefix(jnp.where(has, 1.0, 0.0)) - 1.0
        ord_col = to_col(ordinal)
        slot = per_tile(ord_col - 2.0 * jnp.floor(ord_col * 0.5))
        tile_row = tile[0:1, :]
        first = jnp.where((tile_row < n_used) & (tile_row == per_tile(to_col(t_start))), 1.0, 0.0)
        later = jnp.where(has & (lane > e_col), lane.astype(F32), float(LANES))
        nxt_col = jnp.min(later, axis=-1, keepdims=True)
        nxt_col = jnp.where(nxt_col == float(LANES), -1.0, nxt_col)
        nxt = per_tile(nxt_col)
        zero_row = jnp.where(has, (t_end - 1.0) * TME, -1.0)
        zero_row = jnp.concatenate([zero_row, jnp.full((1, TILE_LANES - LANES), -1.0, F32)], axis=1)
        used = jnp.broadcast_to(n_used, (1, TILE_LANES))
        short_tail = jnp.where(has & (cnt - (t_cnt - 1.0) * TME <= TME // 2), 1.0, 0.0)
        is_last = (tile_row < n_used) & (tile_row == per_tile(end_col) - 1.0)
        half = jnp.where(is_last, per_tile(to_col(short_tail)), 0.0)
        pad = jnp.zeros((SUBLANES - 7, TILE_LANES), F32)
        tiles_ref[...] = jnp.concatenate([expert, slot, first, nxt, zero_row, used, half, pad], axis=0).astype(I32)


def _routing_plan(ri, counts):
    rows = 4 * TM
    dest, tiles = pl.pallas_call(
        _plan_kernel,
        grid=(T_ALL // rows,),
        in_specs=[pl.BlockSpec((rows, LANES), lambda i: (i, 0)), _const_spec((1, LANES))],
        out_specs=[pl.BlockSpec((rows, LANES), lambda i: (i, 0)), _const_spec((SUBLANES, TILE_LANES))],
        out_shape=[jax.ShapeDtypeStruct((T_ALL, LANES), I32), jax.ShapeDtypeStruct((SUBLANES, TILE_LANES), I32)],
        compiler_params=_cparams(1),
        name="moe_plan",
    )(ri, counts)
    return {"dest": dest[:, 0:TOP_K].reshape(-1), "tile_expert": tiles[ROW_EXPERT, :NT_E],
            "n_used": tiles[ROW_USED, 0:1], "tile_slot": tiles[ROW_SLOT, :NT_E], "tile_first": tiles[ROW_FIRST, :NT_E],
            "tile_next": tiles[ROW_NEXT, :NT_E], "tile_half": tiles[ROW_HALF, :NT_E],
            "zero_row": tiles[ROW_ZERO, :N_EXPERTS]}


def kernel(x_prompt, x_sample, c, cache_mla_ckv, cache_mla_krope, cache_gqa_k, cache_gqa_v, c_ctx, w_ada, b_ada, w_in, mla_q_norm, mla_kv_norm, w_mla_uq, w_mla_ukv, gqa_sink, pool_w, pool_scale, w_branch_pool, w_branch_mla, w_branch_gqa, w_out, ln1_g, ln1_b, ln2_g, ln2_b, router_w, router_b, w_gate_up, b_gate_up, w_down, b_down):
    xc = x_prompt.reshape(T_CTX, D_MODEL)
    xl = x_sample.reshape(T_LAT, D_MODEL)
    cond8 = jnp.concatenate([c_ctx[None, :], c, jnp.zeros((SUBLANES - 1 - B_LAT, D_MODEL), F32)], axis=0)
    mod_all = _ada_params(cond8, w_ada, b_ada)

    tab_g = _rope_table(GQA_HD, None)
    tab_q = _rope_table(MLA_ROPE, MLA_NOPE)
    tab_k = _rope_table(MLA_ROPE, 0)

    mod_all = mod_all.reshape(DEPTH, SUBLANES, 1, 6 * D_MODEL)
    wp = _pack_w_in(w_in)
    wuq_p, wk, wv = _pack_mla(w_mla_uq, w_mla_ukv)
    row = lambda a: a.reshape(DEPTH, 1, a.shape[-1])
    q_norm, kv_norm, pool_s = row(mla_q_norm), row(mla_kv_norm), row(pool_scale)
    ln1g, ln1b, ln2g, ln2b = row(ln1_g), row(ln1_b), row(ln2_g), row(ln2_b)
    pool_wb = pool_w.astype(BF16)
    wbp, wbm, wbg, wo = (w.astype(BF16) for w in (w_branch_pool, w_branch_mla, w_branch_gqa, w_out))
    rw = jnp.pad(router_w, ((0, 0), (0, 0), (0, LANES - N_EXPERTS)))
    rb = jnp.pad(router_b, ((0, 0), (0, LANES - N_EXPERTS)), constant_values=-jnp.inf).reshape(DEPTH, 1, LANES)
    sinks = gqa_sink.reshape(DEPTH * GQA_HEADS)
    n_cache = B_LAT * DEPTH * PAST_LEN
    ckv_c = cache_mla_ckv.reshape(n_cache, MLA_KV_LORA)
    kr_c = jnp.pad(cache_mla_krope.reshape(n_cache, MLA_ROPE), ((0, 0), (0, LANES - MLA_ROPE)))
    gk_c = _dup_heads(cache_gqa_k.reshape(n_cache, GQA_KV * GQA_HD)).astype(BF16)
    gv_c = _dup_heads(cache_gqa_v.reshape(n_cache, GQA_KV * GQA_HD)).astype(BF16)

    outs = {"ckv": [], "kr": [], "gk": [], "gv": []}
    for l in range(DEPTH):
        (hp, qm, ckv, kr, km, vm, qg, kg, vg, nk, nv, sg) = _in_proj(
            l, xc, xl, mod_all, wp, q_norm, kv_norm, wuq_p, wk, wv, tab_g, tab_q, tab_k)
        outs["ckv"].append(ckv[:T_CTX].reshape(B_CTX, S_CTX, MLA_KV_LORA))
        outs["kr"].append(kr[:T_CTX, :MLA_ROPE].reshape(B_CTX, S_CTX, MLA_ROPE))
        outs["gk"].append(nk[:T_CTX].reshape(B_CTX, S_CTX, GQA_KV, GQA_HD))
        outs["gv"].append(nv[:T_CTX].reshape(B_CTX, S_CTX, GQA_KV, GQA_HD))

        km_c, vm_c = _expand_cache(l, ckv_c, kr_c, wk, wv)
        ymc = _mla_ctx(qm, km, vm)
        yml = _mla_lat(qm, km, vm, km_c, vm_c)
        ygc = _gqa_ctx(l, sinks, qg, kg, vg)
        ygl = _gqa_lat(l, sinks, qg, kg, vg, gk_c, gv_c)

        x1, u2, ri, rwt, counts = _merge(
            l, xc, xl, mod_all, hp, pool_wb, pool_s, ymc, yml, ygc, ygl, sg, wbp, wbm, wbg, wo, ln1g, ln1b, rw, rb)

        plan = _routing_plan(ri, counts)
        xs = _dispatch(plan, u2)
        ys = _experts(l, plan, xs, w_gate_up, b_gate_up, w_down, b_down)
        xc, xl = _combine(l, plan["dest"], ys, rwt, x1, mod_all, ln2g, ln2b)

    y_prompt = xc.reshape(B_CTX, S_CTX, D_MODEL)
    y_sample = xl.reshape(B_LAT, S_LAT, D_MODEL)
    return (y_prompt, y_sample, jnp.stack(outs["ckv"], axis=1), jnp.stack(outs["kr"], axis=1),
            jnp.stack(outs["gk"], axis=1), jnp.stack(outs["gv"], axis=1))
```
